```python
import math
import jax
import jax.numpy as jnp
from jax import lax
import numpy as np

D_MODEL = 1024
BATCH = 2
SEQ = 8192
DEPTH = 2

LRU_WIDTH = 512
LRU_BLOCKS = 8
LRU_BD = LRU_WIDTH // LRU_BLOCKS
LRU_C = 8.0
CONV_W = 4
DIFF_HEADS = 4
DIFF_HD = 64
DIFF_QBLK = 128
SWA_HEADS = 8
SWA_KV_HEADS = 2
SWA_GROUP = SWA_HEADS // SWA_KV_HEADS
SWA_HD = 64
WINDOW = 128
SWA_BLK = WINDOW
N_BRANCHES = 3
BRANCH_WIDTH = 512
REL_BUCKETS = 32
REL_MAX_DIST = 128
N_ATTN_HEADS = DIFF_HEADS + SWA_HEADS
N_EXPERTS = 32
TOP_K = 4
D_FF = 1024
SWIGLU_LIMIT = 7.0
SWIGLU_ALPHA = 1.702
MOE_BLK = 128
NORM_EPS = 1e-6
NEG_INF = -1e30

IN_SPLITS = (LRU_WIDTH, LRU_WIDTH,
             DIFF_HEADS * 2 * DIFF_HD, DIFF_HEADS * 2 * DIFF_HD, DIFF_HEADS * 2 * DIFF_HD,
             SWA_HEADS * SWA_HD, SWA_KV_HEADS * SWA_HD, SWA_KV_HEADS * SWA_HD,
             N_BRANCHES * D_MODEL)
IN_WIDTH = sum(IN_SPLITS)

kernel_name = 'hybrid_rglru_diffattn_swa_moe_block'


def rms_norm(x, gain):
    xf = x.astype(jnp.float32)
    y = xf * lax.rsqrt(jnp.mean(xf * xf, axis=-1, keepdims=True) + NORM_EPS)
    return (y * gain.astype(jnp.float32)).astype(x.dtype)


def t5_bucket(dist):
    exact = REL_BUCKETS // 2
    is_small = dist < exact
    log_ratio = jnp.log(jnp.maximum(dist, 1).astype(jnp.float32) / exact) / math.log(REL_MAX_DIST / exact)
    large = exact + (log_ratio * (REL_BUCKETS - exact)).astype(jnp.int32)
    return jnp.where(is_small, dist, jnp.minimum(large, REL_BUCKETS - 1))


def rglru_branch(xr, gr, conv_w, conv_b, wa, ba, wx, bx, lam):
    B, S, C = xr.shape
    xp = jnp.pad(xr, ((0, 0), (CONV_W - 1, 0), (0, 0)))
    xc = conv_b + sum(xp[:, tap:tap + S] * conv_w[tap] for tap in range(CONV_W))
    xh = xc.reshape(B, S, LRU_BLOCKS, LRU_BD)
    gate_a = jnp.einsum('bshi,hij->bshj', xh, wa).reshape(B, S, C) + ba
    gate_x = jnp.einsum('bshi,hij->bshj', xh, wx).reshape(B, S, C) + bx
    r = jax.nn.sigmoid(gate_a.astype(jnp.float32))
    i = jax.nn.sigmoid(gate_x.astype(jnp.float32))
    log_a = -LRU_C * r * jax.nn.softplus(-lam.astype(jnp.float32))
    a = jnp.exp(log_a)
    mult = jnp.sqrt(-jnp.expm1(2.0 * log_a))
    b = xc.astype(jnp.float32) * i * mult

    def combine(left, right):
        a1, b1 = left
        a2, b2 = right
        return a1 * a2, a2 * b1 + b2

    _, h = lax.associative_scan(combine, (a, b), axis=1)
    return (h * jax.nn.gelu(gr.astype(jnp.float32), approximate=True)).astype(xr.dtype)


def diff_attention(q, k, v, q_gain, k_gain, lam_vecs, subln_gain, bias_table, lambda_init):
    B, S = q.shape[:2]
    nb = S // DIFF_QBLK
    q = rms_norm(q, q_gain)
    k = rms_norm(k, k_gain)
    lv = lam_vecs.astype(jnp.float32)
    lam = jnp.exp(jnp.sum(lv[0] * lv[1])) - jnp.exp(jnp.sum(lv[2] * lv[3])) + lambda_init
    bias_by_dist = bias_table[t5_bucket(jnp.arange(S))][:, :DIFF_HEADS].astype(jnp.float32)
    k_pos = jnp.arange(S)
    q_blocks = jnp.moveaxis(q.reshape(B, nb, DIFF_QBLK, DIFF_HEADS, 2, DIFF_HD), 1, 0)

    def attend_block(args):
        q_blk, blk = args
        dist = (blk * DIFF_QBLK + jnp.arange(DIFF_QBLK))[:, None] - k_pos[None, :]
        bias = jnp.moveaxis(bias_by_dist[jnp.maximum(dist, 0)], -1, 0)
        s = jnp.einsum('bqhmd,bkhmd->bhmqk', q_blk, k).astype(jnp.float32) * DIFF_HD ** -0.5
        s = jnp.where(dist >= 0, s + bias[None, :, None], NEG_INF)
        p = jax.nn.softmax(s, axis=-1)
        attn = (p[:, :, 0] - lam * p[:, :, 1]).astype(v.dtype)
        return jnp.einsum('bhqk,bkhe->bqhe', attn, v)

    o = lax.map(attend_block, (q_blocks, jnp.arange(nb)))
    o = jnp.moveaxis(o, 0, 1).reshape(B, S, DIFF_HEADS, 2 * DIFF_HD)
    o = rms_norm(o, subln_gain) * (1.0 - lambda_init)
    return o.reshape(B, S, DIFF_HEADS * 2 * DIFF_HD)


def swa_attention(q, k, v, q_gain, k_gain, sinks, bias_table):
    B, S = q.shape[:2]
    nb = S // SWA_BLK
    q = rms_norm(q, q_gain).reshape(B, nb, SWA_BLK, SWA_KV_HEADS, SWA_GROUP, SWA_HD)
    k = rms_norm(k, k_gain).reshape(B, nb, SWA_BLK, SWA_KV_HEADS, SWA_HD)
    v = v.reshape(B, nb, SWA_BLK, SWA_KV_HEADS, SWA_HD)

    def band(t):
        prev = jnp.pad(t, ((0, 0), (1, 0), (0, 0), (0, 0), (0, 0)))[:, :-1]
        return jnp.concatenate([prev, t], axis=2)

    k_band, v_band = band(k), band(v)
    qi = jnp.arange(SWA_BLK)[:, None]
    kj = jnp.arange(2 * SWA_BLK)[None, :]
    dist = SWA_BLK + qi - kj
    in_window = (dist >= 0) & (dist < WINDOW)
    has_prev = (jnp.arange(nb) > 0)[:, None, None] | (kj >= SWA_BLK)[None]
    mask = in_window[None] & has_prev
    bias = bias_table[t5_bucket(jnp.clip(dist, 0, WINDOW - 1))][..., DIFF_HEADS:].astype(jnp.float32)
    bias = bias.reshape(SWA_BLK, 2 * SWA_BLK, SWA_KV_HEADS, SWA_GROUP).transpose(2, 3, 0, 1)
    s = jnp.einsum('bnqhgd,bnkhd->bnhgqk', q, k_band).astype(jnp.float32) * SWA_HD ** -0.5 + bias
    s = jnp.where(mask[None, :, None, None], s, NEG_INF)
    sink = jnp.broadcast_to(sinks.astype(jnp.float32).reshape(SWA_KV_HEADS, SWA_GROUP, 1, 1),
                            s.shape[:-1] + (1,))
    p = jax.nn.softmax(jnp.concatenate([s, sink], axis=-1), axis=-1)[..., :-1]
    o = jnp.einsum('bnhgqk,bnkhd->bnqhgd', p.astype(v.dtype), v_band)
    return o.reshape(B, S, SWA_HEADS * SWA_HD)


def hybrid_mixer(h, w_in, conv_w, conv_b, lru_wa, lru_ba, lru_wx, lru_bx, lru_lambda,
                 diff_qnorm, diff_knorm, diff_lambda, diff_subln,
                 swa_qnorm, swa_knorm, swa_sinks, rel_bias, w_branch, w_out, lambda_init):
    B, S, D = h.shape
    proj = h @ w_in
    offsets = []
    acc = 0
    for width in IN_SPLITS[:-1]:
        acc += width
        offsets.append(acc)
    xr, gr, dq, dk, dv, sq, sk, sv, gl = jnp.split(proj, offsets, axis=-1)
    o_lru = rglru_branch(xr, gr, conv_w, conv_b, lru_wa, lru_ba, lru_wx, lru_bx, lru_lambda)
    o_diff = diff_attention(dq.reshape(B, S, DIFF_HEADS, 2, DIFF_HD),
                            dk.reshape(B, S, DIFF_HEADS, 2, DIFF_HD),
                            dv.reshape(B, S, DIFF_HEADS, 2 * DIFF_HD),
                            diff_qnorm, diff_knorm, diff_lambda, diff_subln, rel_bias, lambda_init)
    o_swa = swa_attention(sq.reshape(B, S, SWA_HEADS, SWA_HD),
                          sk.reshape(B, S, SWA_KV_HEADS, SWA_HD),
                          sv.reshape(B, S, SWA_KV_HEADS, SWA_HD),
                          swa_qnorm, swa_knorm, swa_sinks, rel_bias)
    branches = jnp.stack([o_lru, o_diff, o_swa], axis=2)
    branch_proj = jnp.einsum('bsnc,ncd->bsnd', branches, w_branch)
    gates = jax.nn.sigmoid(gl).reshape(B, S, N_BRANCHES, D)
    merged = jnp.einsum('bsnd,bsnd->bsd', gates, branch_proj)
    return merged @ w_out


def moe_ffn(h, w_router, b_router, w1, b1, w2, b2):
    B, S, D = h.shape
    T = B * S
    xt = h.reshape(T, D)
    logits = xt.astype(jnp.float32) @ w_router.astype(jnp.float32) + b_router.astype(jnp.float32)
    top_v, top_e = lax.top_k(logits, TOP_K)
    weights = jax.nn.softmax(top_v, axis=-1).astype(xt.dtype)
    n_pairs = T * TOP_K
    flat_e = top_e.reshape(-1)
    order = jnp.argsort(flat_e)
    sorted_e = flat_e[order]
    counts = jnp.bincount(flat_e, length=N_EXPERTS)
    starts = jnp.cumsum(counts) - counts
    padded = (counts + MOE_BLK - 1) // MOE_BLK * MOE_BLK
    pad_ends = jnp.cumsum(padded)
    pad_starts = pad_ends - padded
    dest_sorted = pad_starts[sorted_e] + (jnp.arange(n_pairs) - starts[sorted_e])
    n_rows = (n_pairs + MOE_BLK - 1) // MOE_BLK * MOE_BLK + N_EXPERTS * MOE_BLK
    n_blocks = n_rows // MOE_BLK
    row_token = jnp.full((n_rows,), T, jnp.int32).at[dest_sorted].set((order // TOP_K).astype(jnp.int32))
    dest = jnp.zeros_like(order).at[order].set(dest_sorted)
    block_expert = jnp.minimum(jnp.searchsorted(pad_ends, jnp.arange(n_blocks) * MOE_BLK, side='right'),
                               N_EXPERTS - 1)
    x_pad = jnp.concatenate([xt, jnp.zeros((1, D), xt.dtype)], axis=0)

    def expert_block(args):
        tok, e = args
        hcat = x_pad[tok] @ w1[e] + b1[e]
        x_glu = jnp.minimum(hcat[:, ::2], SWIGLU_LIMIT)
        x_lin = jnp.clip(hcat[:, 1::2], -SWIGLU_LIMIT, SWIGLU_LIMIT)
        act = x_glu * jax.nn.sigmoid(SWIGLU_ALPHA * x_glu) * (x_lin + 1.0)
        return act @ w2[e] + b2[e]

    y = lax.map(expert_block, (row_token.reshape(n_blocks, MOE_BLK), block_expert)).reshape(n_rows, D)
    out = jnp.sum(y[dest].reshape(T, TOP_K, D) * weights[..., None], axis=1)
    return out.reshape(B, S, D)


def setup_inputs(seed: int = 0) -> dict:
    key = jax.random.key(seed)
    keys = jax.random.split(key, 40)
    counter = [0]

    def next_key():
        k = keys[counter[0]]
        counter[0] += 1
        return k

    def nrm(shape, scale):
        return jax.random.normal(next_key(), shape, jnp.float32) * scale

    L = DEPTH
    D = D_MODEL
    x = nrm((BATCH, SEQ, D), 1.0)
    c = nrm((BATCH, D), 1.0)
    w_ada = nrm((L, D, 6 * D), 0.5 * D ** -0.5)
    b_ada = nrm((L, 6 * D), 0.02)
    norm_mix = 1.0 + nrm((L, D), 0.02)
    norm_ffn = 1.0 + nrm((L, D), 0.02)
    w_in = nrm((L, D, IN_WIDTH), D ** -0.5)
    conv_w = nrm((L, CONV_W, LRU_WIDTH), CONV_W ** -0.5)
    conv_b = nrm((L, LRU_WIDTH), 0.02)
    lru_wa = nrm((L, LRU_BLOCKS, LRU_BD, LRU_BD), LRU_BD ** -0.5)
    lru_ba = nrm((L, LRU_WIDTH), 0.02)
    lru_wx = nrm((L, LRU_BLOCKS, LRU_BD, LRU_BD), LRU_BD ** -0.5)
    lru_bx = nrm((L, LRU_WIDTH), 0.02)
    radius = jax.random.uniform(next_key(), (L, LRU_WIDTH), jnp.float32, 0.9, 0.999)
    lru_lambda = -jnp.log(jnp.expm1(-jnp.log(radius) / LRU_C))
    diff_qnorm = 1.0 + nrm((L, DIFF_HD), 0.02)
    diff_knorm = 1.0 + nrm((L, DIFF_HD), 0.02)
    diff_lambda = nrm((L, 4, DIFF_HD), 0.1)
    diff_subln = 1.0 + nrm((L, 2 * DIFF_HD), 0.02)
    swa_qnorm = 1.0 + nrm((L, SWA_HD), 0.02)
    swa_knorm = 1.0 + nrm((L, SWA_HD), 0.02)
    swa_sinks = nrm((L, SWA_HEADS), 0.5)
    rel_bias = nrm((REL_BUCKETS, N_ATTN_HEADS), 0.5)
    w_branch = nrm((L, N_BRANCHES, BRANCH_WIDTH, D), BRANCH_WIDTH ** -0.5)
    w_out = nrm((L, D, D), D ** -0.5)
    w_router = nrm((L, D, N_EXPERTS), D ** -0.5)
    b_router = nrm((L, N_EXPERTS), 0.01)
    w1 = nrm((L, N_EXPERTS, D, 2 * D_FF), D ** -0.5)
    b1 = nrm((L, N_EXPERTS, 2 * D_FF), 0.02)
    w2 = nrm((L, N_EXPERTS, D_FF, D), D_FF ** -0.5)
    b2 = nrm((L, N_EXPERTS, D), 0.02)
    return {'x': x, 'c': c, 'w_ada': w_ada, 'b_ada': b_ada, 'norm_mix': norm_mix, 'norm_ffn': norm_ffn,
            'w_in': w_in, 'conv_w': conv_w, 'conv_b': conv_b, 'lru_wa': lru_wa, 'lru_ba': lru_ba,
            'lru_wx': lru_wx, 'lru_bx': lru_bx, 'lru_lambda': lru_lambda, 'diff_qnorm': diff_qnorm,
            'diff_knorm': diff_knorm, 'diff_lambda': diff_lambda, 'diff_subln': diff_subln,
            'swa_qnorm': swa_qnorm, 'swa_knorm': swa_knorm, 'swa_sinks': swa_sinks, 'rel_bias': rel_bias,
            'w_branch': w_branch, 'w_out': w_out, 'w_router': w_router, 'b_router': b_router,
            'w1': w1, 'b1': b1, 'w2': w2, 'b2': b2}


def reference(x, c, w_ada, b_ada, norm_mix, norm_ffn, w_in, conv_w, conv_b, lru_wa, lru_ba,
              lru_wx, lru_bx, lru_lambda, diff_qnorm, diff_knorm, diff_lambda, diff_subln,
              swa_qnorm, swa_knorm, swa_sinks, rel_bias, w_branch, w_out, w_router, b_router,
              w1, b1, w2, b2):
    cond = jax.nn.silu(c)
    for layer in range(DEPTH):
        mod = cond @ w_ada[layer] + b_ada[layer]
        sh_m, sc_m, g_m, sh_f, sc_f, g_f = [m[:, None, :] for m in jnp.split(mod, 6, axis=-1)]
        lambda_init = 0.8 - 0.6 * math.exp(-0.3 * layer)
        h = rms_norm(x, norm_mix[layer]) * (1.0 + sc_m) + sh_m
        x = x + g_m * hybrid_mixer(h, w_in[layer], conv_w[layer], conv_b[layer], lru_wa[layer],
                                   lru_ba[layer], lru_wx[layer], lru_bx[layer], lru_lambda[layer],
                                   diff_qnorm[layer], diff_knorm[layer], diff_lambda[layer],
                                   diff_subln[layer], swa_qnorm[layer], swa_knorm[layer],
                                   swa_sinks[layer], rel_bias, w_branch[layer], w_out[layer],
                                   lambda_init)
        h = rms_norm(x, norm_ffn[layer]) * (1.0 + sc_f) + sh_f
        x = x + g_f * moe_ffn(h, w_router[layer], b_router[layer], w1[layer], b1[layer],
                              w2[layer], b2[layer])
    return x
```

```python
import functools
import math

import jax
import jax.numpy as jnp
from jax import lax
from jax.experimental import pallas as pl
from jax.experimental.pallas import tpu as pltpu

F32 = jnp.float32
BF16 = jnp.bfloat16

LRU_WIDTH = 512
LRU_BLOCKS = 8
LRU_C = 8.0
CONV_W = 4
DIFF_HEADS = 4
DIFF_HD = 64
SWA_HEADS = 8
SWA_KV_HEADS = 2
SWA_GROUP = SWA_HEADS // SWA_KV_HEADS
SWA_HD = 64
WINDOW = 128
N_BRANCHES = 3
BRANCH_WIDTH = 512
REL_BUCKETS = 32
REL_MAX_DIST = 128
N_EXPERTS = 32
TOP_K = 4
SWIGLU_LIMIT = 7.0
SWIGLU_ALPHA = 1.702
NORM_EPS = 1e-6
NEG_INF = -1e30

VMEM_LIMIT_BYTES = 56 * 1024 * 1024
LANES = 128

ADALN_TN = 1536
PROJ_TM = 512
LRU_TS = 512
ATT_TB = 512
SWA_TS = 512
MERGE_TM = 512
ROUTER_TM = 512
MOE_TM = 256
COMB_TM = 128

HIGHEST = lax.Precision.HIGHEST


def _params(*sem):
    return pltpu.CompilerParams(dimension_semantics=sem, vmem_limit_bytes=VMEM_LIMIT_BYTES)


def _dot(a, b, **kw):
    return jnp.dot(a, b, preferred_element_type=F32, **kw)


def _dot_nt(a, b, **kw):
    return lax.dot_general(a, b, (((1,), (1,)), ((), ())), preferred_element_type=F32, **kw)


def _sigmoid(x):
    return 1.0 / (1.0 + jnp.exp(-x))


def _adaln_kernel(c_ref, w_ref, b_ref, o_ref):
    c = c_ref[...]
    cond = c * _sigmoid(c)
    o_ref[0] = _dot(cond, w_ref[0], precision=HIGHEST) + b_ref[0]


def _adaln(c, w_ada, b_ada):
    L, D, N = w_ada.shape
    B = c.shape[0]
    rows = 8
    cp = jnp.zeros((rows, D), F32).at[:B].set(c)
    out = pl.pallas_call(
        _adaln_kernel,
        out_shape=jax.ShapeDtypeStruct((L, rows, N), F32),
        grid=(L, N // ADALN_TN),
        in_specs=[
            pl.BlockSpec((rows, D), lambda l, j: (0, 0)),
            pl.BlockSpec((1, D, ADALN_TN), lambda l, j: (l, 0, j)),
            pl.BlockSpec((1, 1, ADALN_TN), lambda l, j: (l, 0, j)),
        ],
        out_specs=pl.BlockSpec((1, rows, ADALN_TN), lambda l, j: (l, 0, j)),
        compiler_params=_params("arbitrary", "arbitrary"),
        name="adaln",
    )(cp, w_ada, b_ada.reshape(L, 1, N))
    return out[:, :B].reshape(L, B, 6, D)


def _rms_mod(x, gain, scale, shift):
    ms = jnp.mean(x * x, axis=-1, keepdims=True)
    return (x * lax.rsqrt(ms + NORM_EPS) * gain) * (1.0 + scale) + shift


def _group_rms(x, gmat):
    sq = x * x
    hi = sq.astype(BF16)
    lo = (sq - hi.astype(F32)).astype(BF16)
    ms = _dot(hi, gmat) + _dot(lo, gmat)
    return x * lax.rsqrt(ms + NORM_EPS)


def _inproj_kernel(x_ref, mod_ref, gain_ref, w_ref, gmat_ref, qkg_ref,
                   xg_ref, q1_ref, q2_ref, dk_ref, dv_ref, sq_ref, sk_ref, sv_ref, gl_ref):
    x = x_ref[...]
    h = _rms_mod(x, gain_ref[...], mod_ref[0, 1:2, :], mod_ref[0, 0:1, :]).astype(BF16)
    gmat = gmat_ref[...]

    xg_ref[...] = _dot(h, w_ref[:, 0:1024]).astype(BF16)

    dd = _dot(h, w_ref[:, 1024:2560])
    qn = _group_rms(dd[:, 0:512], gmat) * qkg_ref[0:1, :]
    kn = _group_rms(dd[:, 512:1024], gmat) * qkg_ref[1:2, :]
    lane = lax.broadcasted_iota(jnp.int32, qn.shape, 1) % LANES
    q1_ref[...] = jnp.where(lane < DIFF_HD, qn, 0.0).astype(BF16)
    q2_ref[...] = jnp.where(lane >= DIFF_HD, qn, 0.0).astype(BF16)
    dk_ref[...] = kn.astype(BF16)
    dv_ref[...] = dd[:, 1024:1536].astype(BF16)

    ss = _dot(h, w_ref[:, 2560:3328])
    sq_ref[...] = (_group_rms(ss[:, 0:512], gmat) * qkg_ref[2:3, :]).astype(BF16)
    sk_ref[...] = (_group_rms(ss[:, 512:640], gmat[0:128, 0:128]) * qkg_ref[3:4, 0:128]).astype(BF16)
    sv_ref[...] = ss[:, 640:768].astype(BF16)

    gl_ref[...] = _dot(h, w_ref[:, 3328:6400]).astype(BF16)


def _inproj(x2, mod, gain, w_in, gmat, qkg, seq):
    T, D = x2.shape
    tm = min(PROJ_TM, seq)
    per_b = seq // tm
    widths = (1024, 512, 512, 512, 512, 512, 128, 128, 3072)
    row = lambda i: (i, 0)
    const = lambda i: (0, 0)
    return pl.pallas_call(
        _inproj_kernel,
        out_shape=[jax.ShapeDtypeStruct((T, w), BF16) for w in widths],
        grid=(T // tm,),
        in_specs=[
            pl.BlockSpec((tm, D), row),
            pl.BlockSpec((1, 6, D), lambda i: (i // per_b, 0, 0)),
            pl.BlockSpec((1, D), const),
            pl.BlockSpec(w_in.shape, const, pipeline_mode=pl.Buffered(1)),
            pl.BlockSpec(gmat.shape, const),
            pl.BlockSpec(qkg.shape, const),
        ],
        out_specs=[pl.BlockSpec((tm, w), row) for w in widths],
        compiler_params=_params("arbitrary"),
        name="inproj",
    )(x2, mod, gain, w_in, gmat, qkg)


def _rglru_kernel(xg_ref, cw_ref, cb_ref, wg_ref, bg_ref, sp_ref, o_ref, ext_ref, hc_ref):
    ts = xg_ref.shape[0]
    C = LRU_WIDTH

    @pl.when(pl.program_id(1) == 0)
    def _():
        ext_ref[0:8, :] = jnp.zeros((8, C), F32)
        hc_ref[...] = jnp.zeros_like(hc_ref)

    xr = xg_ref[:, 0:C].astype(F32)
    ext_ref[8:8 + ts, :] = xr
    xc = cb_ref[...] + xr * cw_ref[CONV_W - 1:CONV_W, :]
    for back in range(1, CONV_W):
        tap = CONV_W - 1 - back
        xc = xc + ext_ref[8 - back:8 - back + ts, :] * cw_ref[tap:tap + 1, :]
    ext_ref[0:8, :] = xr[ts - 8:ts, :]

    gates = _dot(xc.astype(BF16), wg_ref[...]) + bg_ref[...]
    r = _sigmoid(gates[:, 0:C])
    gi = _sigmoid(gates[:, C:2 * C])
    log_a = (-LRU_C) * r * sp_ref[...]
    a = jnp.exp(log_a)
    b = xc * gi * jnp.sqrt(1.0 - a * a)

    rows = lax.broadcasted_iota(jnp.int32, (ts, C), 0)
    d = 1
    while d < ts:
        keep = rows >= d
        a_sh = pltpu.roll(a, d, axis=0)
        b_sh = pltpu.roll(b, d, axis=0)
        b = jnp.where(keep, a * b_sh + b, b)
        a = jnp.where(keep, a * a_sh, a)
        d *= 2
    h = b + a * hc_ref[...]
    hc_ref[...] = h[ts - 1:ts, :]

    gr = xg_ref[:, C:2 * C].astype(F32)
    gelu = 0.5 * gr * (1.0 + jnp.tanh(math.sqrt(2.0 / math.pi) * (gr + 0.044715 * gr * gr * gr)))
    o_ref[...] = (h * gelu).astype(BF16)


def _rglru(xg, conv_w, conv_b, wg, bg, softplus_neg_lam, batch, seq):
    T = xg.shape[0]
    C = LRU_WIDTH
    ts = min(LRU_TS, seq)
    per_b = seq // ts
    const = lambda b, i: (0, 0)
    return pl.pallas_call(
        _rglru_kernel,
        out_shape=jax.ShapeDtypeStruct((T, C), BF16),
        grid=(batch, per_b),
        in_specs=[
            pl.BlockSpec((ts, 2 * C), lambda b, i: (b * per_b + i, 0)),
            pl.BlockSpec((CONV_W, C), const),
            pl.BlockSpec((1, C), const),
            pl.BlockSpec((C, 2 * C), const),
            pl.BlockSpec((1, 2 * C), const),
            pl.BlockSpec((1, C), const),
        ],
        out_specs=pl.BlockSpec((ts, C), lambda b, i: (b * per_b + i, 0)),
        scratch_shapes=[pltpu.VMEM((ts + 8, C), F32), pltpu.VMEM((1, C), F32)],
        compiler_params=_params("arbitrary", "arbitrary"),
        name="rglru",
    )(xg, conv_w, conv_b, wg, bg, softplus_neg_lam)


def _diff_attn_kernel(q1_ref, q2_ref, k_ref, v_ref, bprev_ref, bdiag_ref, lam_ref, sub_ref, o_ref,
                      m_ref, l_ref, acc_ref, *, tb, lambda_init):
    i = pl.program_id(2)
    q = (q1_ref[0], q2_ref[0])

    m_ref[...] = jnp.full(m_ref.shape, NEG_INF, F32)
    l_ref[...] = jnp.zeros(l_ref.shape, F32)
    acc_ref[...] = jnp.zeros(acc_ref.shape, F32)

    def attend(start, bias):
        kb = k_ref[0, pl.ds(start, tb), :]
        vb = v_ref[0, pl.ds(start, tb), :]
        for mp in range(2):
            s = _dot_nt(q[mp], kb)
            if bias is not None:
                s = s + bias
            m_old = m_ref[mp]
            m_new = jnp.maximum(m_old, jnp.max(s, axis=-1, keepdims=True))
            alpha = jnp.exp(m_old - m_new)
            p = jnp.exp(s - m_new)
            l_ref[mp] = alpha * l_ref[mp] + jnp.sum(p, axis=-1, keepdims=True)
            acc_ref[mp] = alpha * acc_ref[mp] + _dot(p.astype(BF16), vb)
            m_ref[mp] = m_new

    def far(j, carry):
        attend(pl.multiple_of(j * tb, tb), None)
        return carry

    lax.fori_loop(0, jnp.maximum(i - 1, 0), far, 0)

    @pl.when(i > 0)
    def _():
        attend(pl.multiple_of((i - 1) * tb, tb), bprev_ref[0])

    attend(pl.multiple_of(i * tb, tb), bdiag_ref[0])

    lv = lam_ref[...]
    lam = (jnp.exp(jnp.sum(lv[0:1] * lv[1:2], axis=-1, keepdims=True))
           - jnp.exp(jnp.sum(lv[2:3] * lv[3:4], axis=-1, keepdims=True)) + lambda_init)
    o = acc_ref[0] / l_ref[0] - lam * (acc_ref[1] / l_ref[1])
    ms = jnp.mean(o * o, axis=-1, keepdims=True)
    o = o * lax.rsqrt(ms + NORM_EPS) * sub_ref[...] * (1.0 - lambda_init)
    o_ref[0] = o.astype(BF16)


def _diff_attn(q1, q2, k, v, bprev, bdiag, lam_vecs, subln, lambda_init, batch, seq):
    W = DIFF_HEADS * 2 * DIFF_HD
    tb = min(ATT_TB, seq)
    shp = (batch, seq, W)
    qspec = pl.BlockSpec((1, tb, LANES), lambda b, h, i: (b, i, h))
    kspec = pl.BlockSpec((1, seq, LANES), lambda b, h, i: (b, 0, h))
    bspec = pl.BlockSpec((1, tb, tb), lambda b, h, i: (h, 0, 0))
    const = lambda b, h, i: (0, 0)
    out = pl.pallas_call(
        functools.partial(_diff_attn_kernel, tb=tb, lambda_init=lambda_init),
        out_shape=jax.ShapeDtypeStruct(shp, BF16),
        grid=(batch, DIFF_HEADS, seq // tb),
        in_specs=[qspec, qspec, kspec, kspec, bspec, bspec,
                  pl.BlockSpec((4, DIFF_HD), const), pl.BlockSpec((1, 2 * DIFF_HD), const)],
        out_specs=qspec,
        scratch_shapes=[pltpu.VMEM((2, tb, 1), F32), pltpu.VMEM((2, tb, 1), F32),
                        pltpu.VMEM((2, tb, LANES), F32)],
        compiler_params=_params("arbitrary", "arbitrary", "arbitrary"),
        name="diff_attn",
    )(q1.reshape(shp), q2.reshape(shp), k.reshape(shp), v.reshape(shp), bprev, bdiag, lam_vecs, subln)
    return out.reshape(batch * seq, W)


def _swa_kernel(q_ref, k_ref, v_ref, bp_ref, bc_ref, sink_ref, o_ref, *, ts):
    i = pl.program_id(1)
    blk = WINDOW
    for sub in range(ts // blk):
        start = i * ts + sub * blk
        has_prev = start > 0
        pstart = pl.multiple_of(jnp.maximum(start - blk, 0), blk)
        cstart = pl.multiple_of(start, blk)
        kp = k_ref[0, pl.ds(pstart, blk), :]
        kc = k_ref[0, pl.ds(cstart, blk), :]
        vp = v_ref[0, pl.ds(pstart, blk), :]
        vc = v_ref[0, pl.ds(cstart, blk), :]
        qs = q_ref[0, sub * blk:(sub + 1) * blk, :]
        outs = []
        for hk in range(SWA_KV_HEADS):
            c0 = hk * SWA_GROUP * SWA_HD
            qh = jnp.concatenate(
                [qs[:, c0 + g * SWA_HD:c0 + (g + 1) * SWA_HD] for g in range(SWA_GROUP)], axis=0)
            ksl = slice(hk * SWA_HD, (hk + 1) * SWA_HD)
            s_p = _dot_nt(qh, kp[:, ksl]) + bp_ref[hk]
            s_p = jnp.where(has_prev, s_p, NEG_INF)
            s_c = _dot_nt(qh, kc[:, ksl]) + bc_ref[hk]
            sink = sink_ref[hk]
            m = jnp.maximum(jnp.maximum(jnp.max(s_p, axis=-1, keepdims=True),
                                        jnp.max(s_c, axis=-1, keepdims=True)), sink)
            p_p = jnp.exp(s_p - m)
            p_c = jnp.exp(s_c - m)
            den = (jnp.sum(p_p, axis=-1, keepdims=True) + jnp.sum(p_c, axis=-1, keepdims=True)
                   + jnp.exp(sink - m))
            o = _dot(p_p.astype(BF16), vp[:, ksl]) + _dot(p_c.astype(BF16), vc[:, ksl])
            o = o / den
            outs.extend(o[g * blk:(g + 1) * blk, :] for g in range(SWA_GROUP))
        o_ref[0, sub * blk:(sub + 1) * blk, :] = jnp.concatenate(outs, axis=1).astype(BF16)


def _swa(q, k, v, bias_prev, bias_cur, sinks, batch, seq):
    ts = min(SWA_TS, seq)
    WQ = SWA_HEADS * SWA_HD
    WK = SWA_KV_HEADS * SWA_HD
    rows = SWA_GROUP * WINDOW
    const3 = lambda b, i: (0, 0, 0)
    out = pl.pallas_call(
        functools.partial(_swa_kernel, ts=ts),
        out_shape=jax.ShapeDtypeStruct((batch, seq, WQ), BF16),
        grid=(batch, seq // ts),
        in_specs=[
            pl.BlockSpec((1, ts, WQ), lambda b, i: (b, i, 0)),
            pl.BlockSpec((1, seq, WK), lambda b, i: (b, 0, 0)),
            pl.BlockSpec((1, seq, WK), lambda b, i: (b, 0, 0)),
            pl.BlockSpec((SWA_KV_HEADS, rows, WINDOW), const3),
            pl.BlockSpec((SWA_KV_HEADS, rows, WINDOW), const3),
            pl.BlockSpec((SWA_KV_HEADS, rows, 1), const3),
        ],
        out_specs=pl.BlockSpec((1, ts, WQ), lambda b, i: (b, i, 0)),
        compiler_params=_params("arbitrary", "arbitrary"),
        name="swa_attn",
    )(q.reshape(batch, seq, WQ), k.reshape(batch, seq, WK), v.reshape(batch, seq, WK),
      bias_prev, bias_cur, sinks)
    return out.reshape(batch * seq, WQ)


def _merge_kernel(x_ref, mod_ref, lru_ref, diff_ref, swa_ref, gl_ref, wb_ref, wo_ref, o_ref):
    D = x_ref.shape[1]
    merged = None
    for n, br in enumerate((lru_ref, diff_ref, swa_ref)):
        gate = _sigmoid(gl_ref[:, n * D:(n + 1) * D].astype(F32))
        term = gate * _dot(br[...], wb_ref[n])
        merged = term if merged is None else merged + term
    out = _dot(merged.astype(BF16), wo_ref[...])
    o_ref[...] = x_ref[...] + mod_ref[0, 2:3, :] * out


def _merge(x2, mod, o_lru, o_diff, o_swa, gl, w_branch, w_out, seq):
    T, D = x2.shape
    tm = min(MERGE_TM, seq)
    per_b = seq // tm
    row = lambda i: (i, 0)
    return pl.pallas_call(
        _merge_kernel,
        out_shape=jax.ShapeDtypeStruct((T, D), F32),
        grid=(T // tm,),
        in_specs=[
            pl.BlockSpec((tm, D), row),
            pl.BlockSpec((1, 6, D), lambda i: (i // per_b, 0, 0)),
            pl.BlockSpec((tm, BRANCH_WIDTH), row),
            pl.BlockSpec((tm, BRANCH_WIDTH), row),
            pl.BlockSpec((tm, BRANCH_WIDTH), row),
            pl.BlockSpec((tm, N_BRANCHES * D), row),
            pl.BlockSpec(w_branch.shape, lambda i: (0, 0, 0)),
            pl.BlockSpec(w_out.shape, lambda i: (0, 0)),
        ],
        out_specs=pl.BlockSpec((tm, D), row),
        compiler_params=_params("arbitrary"),
        name="merge_outproj",
    )(x2, mod, o_lru, o_diff, o_swa, gl, w_branch, w_out)


def _router_kernel(x_ref, mod_ref, gain_ref, wr_ref, br_ref, h_ref, e_ref, p_ref):
    x = x_ref[...]
    h = _rms_mod(x, gain_ref[...], mod_ref[0, 4:5, :], mod_ref[0, 3:4, :])
    h_ref[...] = h
    logits = _dot_nt(wr_ref[...], h, precision=HIGHEST) + br_ref[...]
    eidx = lax.broadcasted_iota(jnp.int32, logits.shape, 0)
    vals, idxs = [], []
    for _ in range(TOP_K):
        mx = jnp.max(logits, axis=0, keepdims=True)
        sel = jnp.min(jnp.where(logits == mx, eidx, N_EXPERTS), axis=0, keepdims=True)
        vals.append(mx)
        idxs.append(sel)
        logits = jnp.where(eidx == sel, -jnp.inf, logits)
    top_v = jnp.concatenate(vals, axis=0)
    ex = jnp.exp(top_v - top_v[0:1])
    p_ref[...] = ex / jnp.sum(ex, axis=0, keepdims=True)
    e_ref[...] = jnp.concatenate(idxs, axis=0)


def _router(x2, mod, gain, w_router_t, b_router, seq):
    T, D = x2.shape
    tm = min(ROUTER_TM, seq)
    per_b = seq // tm
    return pl.pallas_call(
        _router_kernel,
        out_shape=[jax.ShapeDtypeStruct((T, D), F32),
                   jax.ShapeDtypeStruct((TOP_K, T), jnp.int32),
                   jax.ShapeDtypeStruct((TOP_K, T), F32)],
        grid=(T // tm,),
        in_specs=[
            pl.BlockSpec((tm, D), lambda i: (i, 0)),
            pl.BlockSpec((1, 6, D), lambda i: (i // per_b, 0, 0)),
            pl.BlockSpec((1, D), lambda i: (0, 0)),
            pl.BlockSpec((N_EXPERTS, D), lambda i: (0, 0)),
            pl.BlockSpec((N_EXPERTS, 1), lambda i: (0, 0)),
        ],
        out_specs=[pl.BlockSpec((tm, D), lambda i: (i, 0)),
                   pl.BlockSpec((TOP_K, tm), lambda i: (0, i)),
                   pl.BlockSpec((TOP_K, tm), lambda i: (0, i))],
        compiler_params=_params("arbitrary"),
        name="router",
    )(x2, mod, gain, w_router_t, b_router)


def _expert_kernel(be_ref, nused_ref, tok_ref, h_hbm, w1g_ref, w1l_ref, b1g_ref, b1l_ref, w2_ref, b2_ref,
                   wgt_ref, y_ref, xbuf, sem, *, tm):
    i = pl.program_id(0)

    @pl.when(i < nused_ref[0])
    def _():
        def issue(r, carry):
            t = tok_ref[0, 0, r]
            pltpu.make_async_copy(h_hbm.at[pl.ds(t, 1), :], xbuf.at[pl.ds(r, 1), :], sem).start()
            return carry

        lax.fori_loop(0, tm, issue, 0)

        def drain(r, carry):
            pltpu.make_async_copy(h_hbm.at[pl.ds(0, 1), :], xbuf.at[pl.ds(r, 1), :], sem).wait()
            return carry

        lax.fori_loop(0, tm, drain, 0)

        xb = xbuf[...].astype(BF16)
        g = jnp.minimum(_dot(xb, w1g_ref[0]) + b1g_ref[0], SWIGLU_LIMIT)
        lin = jnp.clip(_dot(xb, w1l_ref[0]) + b1l_ref[0], -SWIGLU_LIMIT, SWIGLU_LIMIT)
        act = g * _sigmoid(SWIGLU_ALPHA * g) * (lin + 1.0)
        y = _dot(act.astype(BF16), w2_ref[0]) + b2_ref[0]
        y_ref[...] = y * wgt_ref[...]

    @pl.when(i >= nused_ref[0])
    def _():
        y_ref[...] = jnp.zeros(y_ref.shape, y_ref.dtype)


def _experts(block_expert, n_used, row_token, h, w1g, w1l, b1g, b1l, w2, b2, row_weight):
    T, D = h.shape
    E, _, F = w1g.shape
    tm = MOE_TM
    n_rows = row_token.shape[0]
    nb = n_rows // tm
    wmap = lambda i, be, nu: (be[i], 0, 0)
    return pl.pallas_call(
        functools.partial(_expert_kernel, tm=tm),
        out_shape=jax.ShapeDtypeStruct((n_rows, D), F32),
        grid_spec=pltpu.PrefetchScalarGridSpec(
            num_scalar_prefetch=2,
            grid=(nb,),
            in_specs=[
                pl.BlockSpec((1, 1, tm), lambda i, be, nu: (i, 0, 0), memory_space=pltpu.SMEM),
                pl.BlockSpec(memory_space=pl.ANY),
                pl.BlockSpec((1, D, F), wmap),
                pl.BlockSpec((1, D, F), wmap),
                pl.BlockSpec((1, 1, F), wmap),
                pl.BlockSpec((1, 1, F), wmap),
                pl.BlockSpec((1, F, D), wmap),
                pl.BlockSpec((1, 1, D), wmap),
                pl.BlockSpec((tm, 1), lambda i, be, nu: (i, 0)),
            ],
            out_specs=pl.BlockSpec((tm, D), lambda i, be, nu: (i, 0)),
            scratch_shapes=[pltpu.VMEM((tm, D), F32), pltpu.SemaphoreType.DMA],
        ),
        compiler_params=_params("arbitrary"),
        name="moe_experts",
    )(block_expert, n_used, row_token.reshape(nb, 1, tm), h, w1g, w1l, b1g, b1l, w2, b2, row_weight)


def _combine_kernel(dest_ref, y_hbm, x_ref, mod_ref, o_ref, ybuf, sem, *, tm):
    n = TOP_K * tm

    def issue(r, carry):
        d = dest_ref[0, 0, r]
        pltpu.make_async_copy(y_hbm.at[pl.ds(d, 1), :], ybuf.at[pl.ds(r, 1), :], sem).start()
        return carry

    lax.fori_loop(0, n, issue, 0)

    def drain(r, carry):
        pltpu.make_async_copy(y_hbm.at[pl.ds(0, 1), :], ybuf.at[pl.ds(r, 1), :], sem).wait()
        return carry

    lax.fori_loop(0, n, drain, 0)

    acc = ybuf[0:tm, :]
    for k in range(1, TOP_K):
        acc = acc + ybuf[k * tm:(k + 1) * tm, :]
    o_ref[...] = x_ref[...] + mod_ref[0, 5:6, :] * acc


def _combine(dest, y, x2, mod, seq):
    T, D = x2.shape
    tm = min(COMB_TM, seq)
    per_b = seq // tm
    nb = T // tm
    dest_t = dest.reshape(TOP_K, nb, tm).transpose(1, 0, 2).reshape(nb, 1, TOP_K * tm)
    return pl.pallas_call(
        functools.partial(_combine_kernel, tm=tm),
        out_shape=jax.ShapeDtypeStruct((T, D), F32),
        grid=(nb,),
        in_specs=[
            pl.BlockSpec((1, 1, TOP_K * tm), lambda i: (i, 0, 0), memory_space=pltpu.SMEM),
            pl.BlockSpec(memory_space=pl.ANY),
            pl.BlockSpec((tm, D), lambda i: (i, 0)),
            pl.BlockSpec((1, 6, D), lambda i: (i // per_b, 0, 0)),
        ],
        out_specs=pl.BlockSpec((tm, D), lambda i: (i, 0)),
        scratch_shapes=[pltpu.VMEM((TOP_K * tm, D), F32), pltpu.SemaphoreType.DMA],
        compiler_params=_params("arbitrary"),
        name="moe_combine",
    )(dest_t, y, x2, mod)


def _t5_bucket(dist):
    exact = REL_BUCKETS // 2
    log_ratio = jnp.log(jnp.maximum(dist, 1).astype(F32) / exact) / math.log(REL_MAX_DIST / exact)
    large = exact + (log_ratio * (REL_BUCKETS - exact)).astype(jnp.int32)
    return jnp.where(dist < exact, dist, jnp.minimum(large, REL_BUCKETS - 1))


def _diff_bias_tiles(rel_bias, tb):
    table = rel_bias[:, :DIFF_HEADS].astype(F32)
    r = jnp.arange(tb)[:, None]
    c = jnp.arange(tb)[None, :]
    far = table[REL_BUCKETS - 1]

    def tile(dist):
        b = table[_t5_bucket(jnp.maximum(dist, 0))] - far
        return jnp.moveaxis(b, -1, 0)

    diag = jnp.where((r - c >= 0)[None], tile(r - c), NEG_INF)
    prev = tile(r + tb - c)
    return prev, diag


def _swa_bias_tiles(rel_bias):
    qi = jnp.arange(WINDOW)[:, None]
    kj = jnp.arange(2 * WINDOW)[None, :]
    dist = WINDOW + qi - kj
    ok = (dist >= 0) & (dist < WINDOW)
    bias = rel_bias[_t5_bucket(jnp.clip(dist, 0, WINDOW - 1))][..., DIFF_HEADS:].astype(F32)
    bias = jnp.where(ok[..., None], bias, NEG_INF)
    bias = bias.reshape(WINDOW, 2 * WINDOW, SWA_KV_HEADS, SWA_GROUP).transpose(2, 3, 0, 1)
    bias = bias.reshape(SWA_KV_HEADS, SWA_GROUP * WINDOW, 2 * WINDOW)
    return bias[..., :WINDOW], bias[..., WINDOW:]


def _block_diag(w):
    nb, n, _ = w.shape
    eye = jnp.eye(nb, dtype=w.dtype)
    return (eye[:, None, :, None] * w[:, :, None, :]).reshape(nb * n, nb * n)


def _routing(top_e, top_p, tm):
    K, T = top_e.shape
    n_pairs = K * T
    flat_e = top_e.reshape(-1)
    flat_t = jnp.tile(jnp.arange(T, dtype=jnp.int32), K)
    order = jnp.argsort(flat_e)
    sorted_e = flat_e[order]
    counts = jnp.bincount(flat_e, length=N_EXPERTS)
    starts = jnp.cumsum(counts) - counts
    padded = (counts + tm - 1) // tm * tm
    pad_ends = jnp.cumsum(padded)
    pad_starts = pad_ends - padded
    dest_sorted = (pad_starts[sorted_e] + (jnp.arange(n_pairs) - starts[sorted_e])).astype(jnp.int32)
    n_rows = n_pairs + N_EXPERTS * tm
    nb = n_rows // tm
    row_token = jnp.zeros((n_rows,), jnp.int32).at[dest_sorted].set(flat_t[order])
    row_weight = jnp.zeros((n_rows,), F32).at[dest_sorted].set(top_p.reshape(-1)[order])
    dest = jnp.zeros((n_pairs,), jnp.int32).at[order].set(dest_sorted)
    block_expert = jnp.minimum(
        jnp.searchsorted(pad_ends, jnp.arange(nb) * tm, side="right"), N_EXPERTS - 1).astype(jnp.int32)
    n_used = (pad_ends[-1] // tm).astype(jnp.int32).reshape(1)
    return block_expert, n_used, row_token, row_weight.reshape(n_rows, 1), dest.reshape(K, T)


def kernel(x, c, w_ada, b_ada, norm_mix, norm_ffn, w_in, conv_w, conv_b, lru_wa, lru_ba, lru_wx, lru_bx,
           lru_lambda, diff_qnorm, diff_knorm, diff_lambda, diff_subln, swa_qnorm, swa_knorm, swa_sinks,
           rel_bias, w_branch, w_out, w_router, b_router, w1, b1, w2, b2):
    B, S, D = x.shape
    L = w_ada.shape[0]
    T = B * S
    tb = min(ATT_TB, S)

    mods = _adaln(c, w_ada, b_ada)
    bprev, bdiag = _diff_bias_tiles(rel_bias, tb)
    sbias_prev, sbias_cur = _swa_bias_tiles(rel_bias)
    gidx = jnp.arange(512) // DIFF_HD
    gmat = jnp.where(gidx[:, None] == gidx[None, :], 1.0 / DIFF_HD, 0.0).astype(BF16)

    x2 = x.reshape(T, D)
    for l in range(L):
        lambda_init = 0.8 - 0.6 * math.exp(-0.3 * l)
        mod = mods[l]
        qkg = jnp.stack([
            jnp.tile(diff_qnorm[l], 8) * DIFF_HD ** -0.5,
            jnp.tile(diff_knorm[l], 8),
            jnp.tile(swa_qnorm[l], 8) * SWA_HD ** -0.5,
            jnp.tile(swa_knorm[l], 8),
        ]).astype(F32)
        xg, q1, q2, dk, dv, sq, sk, sv, gl = _inproj(
            x2, mod, norm_mix[l].reshape(1, D), w_in[l].astype(BF16), gmat, qkg, S)

        wg = jnp.concatenate([_block_diag(lru_wa[l]), _block_diag(lru_wx[l])], axis=1).astype(BF16)
        bg = jnp.concatenate([lru_ba[l], lru_bx[l]]).reshape(1, 2 * LRU_WIDTH)
        sp = jax.nn.softplus(-lru_lambda[l].astype(F32)).reshape(1, LRU_WIDTH)
        o_lru = _rglru(xg, conv_w[l], conv_b[l].reshape(1, LRU_WIDTH), wg, bg, sp, B, S)

        o_diff = _diff_attn(q1, q2, dk, dv, bprev, bdiag, diff_lambda[l],
                            diff_subln[l].reshape(1, 2 * DIFF_HD), lambda_init, B, S)

        sinks = jnp.repeat(swa_sinks[l].astype(F32).reshape(SWA_KV_HEADS, SWA_GROUP), WINDOW, axis=1)
        o_swa = _swa(sq, sk, sv, sbias_prev, sbias_cur, sinks.reshape(SWA_KV_HEADS, -1, 1), B, S)

        x2 = _merge(x2, mod, o_lru, o_diff, o_swa, gl, w_branch[l].astype(BF16), w_out[l].astype(BF16), S)

        h, top_e, top_p = _router(x2, mod, norm_ffn[l].reshape(1, D), w_router[l].T,
                                  b_router[l].reshape(N_EXPERTS, 1), S)
        block_expert, n_used, row_token, row_weight, dest = _routing(top_e, top_p, MOE_TM)
        w1l_ = w1[l]
        y = _experts(block_expert, n_used, row_token, h,
                     w1l_[:, :, 0::2].astype(BF16), w1l_[:, :, 1::2].astype(BF16),
                     b1[l][:, None, 0::2], b1[l][:, None, 1::2],
                     w2[l].astype(BF16), b2[l][:, None, :], row_weight)
        x2 = _combine(dest, y, x2, mod, S)
    return x2.reshape(B, S, D)
```

```python
import functools
import math

import jax
import jax.numpy as jnp
from jax import lax
from jax.experimental import pallas as pl
from jax.experimental.pallas import tpu as pltpu

F32 = jnp.float32
BF16 = jnp.bfloat16

LRU_WIDTH = 512
LRU_BLOCKS = 8
LRU_C = 8.0
CONV_W = 4
DIFF_HEADS = 4
DIFF_HD = 64
SWA_HEADS = 8
SWA_KV_HEADS = 2
SWA_GROUP = SWA_HEADS // SWA_KV_HEADS
SWA_HD = 64
WINDOW = 128
N_BRANCHES = 3
BRANCH_WIDTH = 512
REL_BUCKETS = 32
REL_MAX_DIST = 128
N_EXPERTS = 32
TOP_K = 4
SWIGLU_LIMIT = 7.0
SWIGLU_ALPHA = 1.702
NORM_EPS = 1e-6
NEG_INF = -1e30

VMEM_LIMIT_BYTES = 56 * 1024 * 1024
LANES = 128

ADALN_TN = 1536
PROJ_TM = 512
LRU_TS = 512
ATT_TB = 512
SWA_TS = 512
MERGE_TM = 512
ROUTER_TM = 512
DISP_TM = 512
MOE_TM = 256
COMB_TM = 128

HIGHEST = lax.Precision.HIGHEST


def _params(*sem):
    return pltpu.CompilerParams(dimension_semantics=sem, vmem_limit_bytes=VMEM_LIMIT_BYTES)


def _dot(a, b, **kw):
    return jnp.dot(a, b, preferred_element_type=F32, **kw)


def _dot_nt(a, b, **kw):
    return lax.dot_general(a, b, (((1,), (1,)), ((), ())), preferred_element_type=F32, **kw)


def _sigmoid(x):
    return 1.0 / (1.0 + jnp.exp(-x))


def _adaln_kernel(c_ref, w_ref, b_ref, o_ref):
    c = c_ref[...]
    cond = c * _sigmoid(c)
    o_ref[0] = _dot(cond, w_ref[0], precision=HIGHEST) + b_ref[0]


def _adaln(c, w_ada, b_ada):
    L, D, N = w_ada.shape
    B = c.shape[0]
    rows = 8
    cp = jnp.zeros((rows, D), F32).at[:B].set(c)
    out = pl.pallas_call(
        _adaln_kernel,
        out_shape=jax.ShapeDtypeStruct((L, rows, N), F32),
        grid=(L, N // ADALN_TN),
        in_specs=[
            pl.BlockSpec((rows, D), lambda l, j: (0, 0)),
            pl.BlockSpec((1, D, ADALN_TN), lambda l, j: (l, 0, j)),
            pl.BlockSpec((1, 1, ADALN_TN), lambda l, j: (l, 0, j)),
        ],
        out_specs=pl.BlockSpec((1, rows, ADALN_TN), lambda l, j: (l, 0, j)),
        compiler_params=_params("arbitrary", "arbitrary"),
        name="adaln",
    )(cp, w_ada, b_ada.reshape(L, 1, N))
    return out[:, :B].reshape(L, B, 6, D)


def _rms_mod(x, gain, scale, shift):
    ms = jnp.mean(x * x, axis=-1, keepdims=True)
    return (x * lax.rsqrt(ms + NORM_EPS) * gain) * (1.0 + scale) + shift


def _group_rms(x, gmat):
    sq = x * x
    hi = sq.astype(BF16)
    lo = (sq - hi.astype(F32)).astype(BF16)
    ms = _dot(hi, gmat) + _dot(lo, gmat)
    return x * lax.rsqrt(ms + NORM_EPS)


def _inproj_kernel(x_ref, mod_ref, gain_ref, w_ref, gmat_ref, qkg_ref,
                   xg_ref, q1_ref, q2_ref, dk_ref, dv_ref, sq_ref, sk_ref, sv_ref, gl_ref):
    x = x_ref[...]
    h = _rms_mod(x, gain_ref[...], mod_ref[0, 1:2, :], mod_ref[0, 0:1, :]).astype(BF16)
    gmat = gmat_ref[...]

    xg_ref[...] = _dot(h, w_ref[:, 0:1024]).astype(BF16)

    dd = _dot(h, w_ref[:, 1024:2560])
    qn = _group_rms(dd[:, 0:512], gmat) * qkg_ref[0:1, :]
    kn = _group_rms(dd[:, 512:1024], gmat) * qkg_ref[1:2, :]
    lane = lax.broadcasted_iota(jnp.int32, qn.shape, 1) % LANES
    q1_ref[...] = jnp.where(lane < DIFF_HD, qn, 0.0).astype(BF16)
    q2_ref[...] = jnp.where(lane >= DIFF_HD, qn, 0.0).astype(BF16)
    dk_ref[...] = kn.astype(BF16)
    dv_ref[...] = dd[:, 1024:1536].astype(BF16)

    ss = _dot(h, w_ref[:, 2560:3328])
    sq_ref[...] = (_group_rms(ss[:, 0:512], gmat) * qkg_ref[2:3, :]).astype(BF16)
    sk_ref[...] = (_group_rms(ss[:, 512:640], gmat[0:128, 0:128]) * qkg_ref[3:4, 0:128]).astype(BF16)
    sv_ref[...] = ss[:, 640:768].astype(BF16)

    gl_ref[...] = _dot(h, w_ref[:, 3328:6400]).astype(BF16)


def _inproj(x2, mod, gain, w_in, gmat, qkg, seq):
    T, D = x2.shape
    tm = min(PROJ_TM, seq)
    per_b = seq // tm
    widths = (1024, 512, 512, 512, 512, 512, 128, 128, 3072)
    row = lambda i: (i, 0)
    const = lambda i: (0, 0)
    return pl.pallas_call(
        _inproj_kernel,
        out_shape=[jax.ShapeDtypeStruct((T, w), BF16) for w in widths],
        grid=(T // tm,),
        in_specs=[
            pl.BlockSpec((tm, D), row),
            pl.BlockSpec((1, 6, D), lambda i: (i // per_b, 0, 0)),
            pl.BlockSpec((1, D), const),
            pl.BlockSpec(w_in.shape, const, pipeline_mode=pl.Buffered(1)),
            pl.BlockSpec(gmat.shape, const),
            pl.BlockSpec(qkg.shape, const),
        ],
        out_specs=[pl.BlockSpec((tm, w), row) for w in widths],
        compiler_params=_params("arbitrary"),
        name="inproj",
    )(x2, mod, gain, w_in, gmat, qkg)


def _rglru_kernel(xg_ref, cw_ref, cb_ref, wg_ref, bg_ref, sp_ref, o_ref, ext_ref, hc_ref):
    ts = xg_ref.shape[0]
    C = LRU_WIDTH

    @pl.when(pl.program_id(1) == 0)
    def _():
        ext_ref[0:8, :] = jnp.zeros((8, C), F32)
        hc_ref[...] = jnp.zeros_like(hc_ref)

    xr = xg_ref[:, 0:C].astype(F32)
    ext_ref[8:8 + ts, :] = xr
    xc = cb_ref[...] + xr * cw_ref[CONV_W - 1:CONV_W, :]
    for back in range(1, CONV_W):
        tap = CONV_W - 1 - back
        xc = xc + ext_ref[8 - back:8 - back + ts, :] * cw_ref[tap:tap + 1, :]
    ext_ref[0:8, :] = xr[ts - 8:ts, :]

    gates = _dot(xc.astype(BF16), wg_ref[...]) + bg_ref[...]
    r = _sigmoid(gates[:, 0:C])
    gi = _sigmoid(gates[:, C:2 * C])
    log_a = (-LRU_C) * r * sp_ref[...]
    a = jnp.exp(log_a)
    b = xc * gi * jnp.sqrt(1.0 - a * a)

    rows = lax.broadcasted_iota(jnp.int32, (ts, C), 0)
    d = 1
    while d < ts:
        keep = rows >= d
        a_sh = pltpu.roll(a, d, axis=0)
        b_sh = pltpu.roll(b, d, axis=0)
        b = jnp.where(keep, a * b_sh + b, b)
        a = jnp.where(keep, a * a_sh, a)
        d *= 2
    h = b + a * hc_ref[...]
    hc_ref[...] = h[ts - 1:ts, :]

    gr = xg_ref[:, C:2 * C].astype(F32)
    gelu = 0.5 * gr * (1.0 + jnp.tanh(math.sqrt(2.0 / math.pi) * (gr + 0.044715 * gr * gr * gr)))
    o_ref[...] = (h * gelu).astype(BF16)


def _rglru(xg, conv_w, conv_b, wg, bg, softplus_neg_lam, batch, seq):
    T = xg.shape[0]
    C = LRU_WIDTH
    ts = min(LRU_TS, seq)
    per_b = seq // ts
    const = lambda b, i: (0, 0)
    return pl.pallas_call(
        _rglru_kernel,
        out_shape=jax.ShapeDtypeStruct((T, C), BF16),
        grid=(batch, per_b),
        in_specs=[
            pl.BlockSpec((ts, 2 * C), lambda b, i: (b * per_b + i, 0)),
            pl.BlockSpec((CONV_W, C), const),
            pl.BlockSpec((1, C), const),
            pl.BlockSpec((C, 2 * C), const),
            pl.BlockSpec((1, 2 * C), const),
            pl.BlockSpec((1, C), const),
        ],
        out_specs=pl.BlockSpec((ts, C), lambda b, i: (b * per_b + i, 0)),
        scratch_shapes=[pltpu.VMEM((ts + 8, C), F32), pltpu.VMEM((1, C), F32)],
        compiler_params=_params("arbitrary", "arbitrary"),
        name="rglru",
    )(xg, conv_w, conv_b, wg, bg, softplus_neg_lam)


def _diff_attn_kernel(q1_ref, q2_ref, k_ref, v_ref, bprev_ref, bdiag_ref, lam_ref, sub_ref, o_ref,
                      m_ref, l_ref, acc_ref, *, tb, lambda_init):
    i = pl.program_id(2)
    q = (q1_ref[0], q2_ref[0])

    m_ref[...] = jnp.full(m_ref.shape, NEG_INF, F32)
    l_ref[...] = jnp.zeros(l_ref.shape, F32)
    acc_ref[...] = jnp.zeros(acc_ref.shape, F32)

    def attend(start, bias):
        kb = k_ref[0, pl.ds(start, tb), :]
        vb = v_ref[0, pl.ds(start, tb), :]
        for mp in range(2):
            s = _dot_nt(q[mp], kb)
            if bias is not None:
                s = s + bias
            m_old = m_ref[mp]
            m_new = jnp.maximum(m_old, jnp.max(s, axis=-1, keepdims=True))
            alpha = jnp.exp(m_old - m_new)
            p = jnp.exp(s - m_new)
            l_ref[mp] = alpha * l_ref[mp] + jnp.sum(p, axis=-1, keepdims=True)
            acc_ref[mp] = alpha * acc_ref[mp] + _dot(p.astype(BF16), vb)
            m_ref[mp] = m_new

    def far(j, carry):
        attend(pl.multiple_of(j * tb, tb), None)
        return carry

    lax.fori_loop(0, jnp.maximum(i - 1, 0), far, 0)

    @pl.when(i > 0)
    def _():
        attend(pl.multiple_of((i - 1) * tb, tb), bprev_ref[0])

    attend(pl.multiple_of(i * tb, tb), bdiag_ref[0])

    lv = lam_ref[...]
    lam = (jnp.exp(jnp.sum(lv[0:1] * lv[1:2], axis=-1, keepdims=True))
           - jnp.exp(jnp.sum(lv[2:3] * lv[3:4], axis=-1, keepdims=True)) + lambda_init)
    o = acc_ref[0] / l_ref[0] - lam * (acc_ref[1] / l_ref[1])
    ms = jnp.mean(o * o, axis=-1, keepdims=True)
    o = o * lax.rsqrt(ms + NORM_EPS) * sub_ref[...] * (1.0 - lambda_init)
    o_ref[0] = o.astype(BF16)


def _diff_attn(q1, q2, k, v, bprev, bdiag, lam_vecs, subln, lambda_init, batch, seq):
    W = DIFF_HEADS * 2 * DIFF_HD
    tb = min(ATT_TB, seq)
    shp = (batch, seq, W)
    qspec = pl.BlockSpec((1, tb, LANES), lambda b, h, i: (b, i, h))
    kspec = pl.BlockSpec((1, seq, LANES), lambda b, h, i: (b, 0, h))
    bspec = pl.BlockSpec((1, tb, tb), lambda b, h, i: (h, 0, 0))
    const = lambda b, h, i: (0, 0)
    out = pl.pallas_call(
        functools.partial(_diff_attn_kernel, tb=tb, lambda_init=lambda_init),
        out_shape=jax.ShapeDtypeStruct(shp, BF16),
        grid=(batch, DIFF_HEADS, seq // tb),
        in_specs=[qspec, qspec, kspec, kspec, bspec, bspec,
                  pl.BlockSpec((4, DIFF_HD), const), pl.BlockSpec((1, 2 * DIFF_HD), const)],
        out_specs=qspec,
        scratch_shapes=[pltpu.VMEM((2, tb, 1), F32), pltpu.VMEM((2, tb, 1), F32),
                        pltpu.VMEM((2, tb, LANES), F32)],
        compiler_params=_params("arbitrary", "arbitrary", "arbitrary"),
        name="diff_attn",
    )(q1.reshape(shp), q2.reshape(shp), k.reshape(shp), v.reshape(shp), bprev, bdiag, lam_vecs, subln)
    return out.reshape(batch * seq, W)


def _swa_kernel(q_ref, k_ref, v_ref, bp_ref, bc_ref, sink_ref, o_ref, *, ts):
    i = pl.program_id(1)
    blk = WINDOW
    for sub in range(ts // blk):
        start = i * ts + sub * blk
        has_prev = start > 0
        pstart = pl.multiple_of(jnp.maximum(start - blk, 0), blk)
        cstart = pl.multiple_of(start, blk)
        kp = k_ref[0, pl.ds(pstart, blk), :]
        kc = k_ref[0, pl.ds(cstart, blk), :]
        vp = v_ref[0, pl.ds(pstart, blk), :]
        vc = v_ref[0, pl.ds(cstart, blk), :]
        qs = q_ref[0, sub * blk:(sub + 1) * blk, :]
        outs = []
        for hk in range(SWA_KV_HEADS):
            c0 = hk * SWA_GROUP * SWA_HD
            qh = jnp.concatenate(
                [qs[:, c0 + g * SWA_HD:c0 + (g + 1) * SWA_HD] for g in range(SWA_GROUP)], axis=0)
            ksl = slice(hk * SWA_HD, (hk + 1) * SWA_HD)
            s_p = _dot_nt(qh, kp[:, ksl]) + bp_ref[hk]
            s_p = jnp.where(has_prev, s_p, NEG_INF)
            s_c = _dot_nt(qh, kc[:, ksl]) + bc_ref[hk]
            sink = sink_ref[hk]
            m = jnp.maximum(jnp.maximum(jnp.max(s_p, axis=-1, keepdims=True),
                                        jnp.max(s_c, axis=-1, keepdims=True)), sink)
            p_p = jnp.exp(s_p - m)
            p_c = jnp.exp(s_c - m)
            den = (jnp.sum(p_p, axis=-1, keepdims=True) + jnp.sum(p_c, axis=-1, keepdims=True)
                   + jnp.exp(sink - m))
            o = _dot(p_p.astype(BF16), vp[:, ksl]) + _dot(p_c.astype(BF16), vc[:, ksl])
            o = o / den
            outs.extend(o[g * blk:(g + 1) * blk, :] for g in range(SWA_GROUP))
        o_ref[0, sub * blk:(sub + 1) * blk, :] = jnp.concatenate(outs, axis=1).astype(BF16)


def _swa(q, k, v, bias_prev, bias_cur, sinks, batch, seq):
    ts = min(SWA_TS, seq)
    WQ = SWA_HEADS * SWA_HD
    WK = SWA_KV_HEADS * SWA_HD
    rows = SWA_GROUP * WINDOW
    const3 = lambda b, i: (0, 0, 0)
    out = pl.pallas_call(
        functools.partial(_swa_kernel, ts=ts),
        out_shape=jax.ShapeDtypeStruct((batch, seq, WQ), BF16),
        grid=(batch, seq // ts),
        in_specs=[
            pl.BlockSpec((1, ts, WQ), lambda b, i: (b, i, 0)),
            pl.BlockSpec((1, seq, WK), lambda b, i: (b, 0, 0)),
            pl.BlockSpec((1, seq, WK), lambda b, i: (b, 0, 0)),
            pl.BlockSpec((SWA_KV_HEADS, rows, WINDOW), const3),
            pl.BlockSpec((SWA_KV_HEADS, rows, WINDOW), const3),
            pl.BlockSpec((SWA_KV_HEADS, rows, 1), const3),
        ],
        out_specs=pl.BlockSpec((1, ts, WQ), lambda b, i: (b, i, 0)),
        compiler_params=_params("arbitrary", "arbitrary"),
        name="swa_attn",
    )(q.reshape(batch, seq, WQ), k.reshape(batch, seq, WK), v.reshape(batch, seq, WK),
      bias_prev, bias_cur, sinks)
    return out.reshape(batch * seq, WQ)


def _merge_kernel(x_ref, mod_ref, lru_ref, diff_ref, swa_ref, gl_ref, wb_ref, wo_ref, o_ref):
    D = x_ref.shape[1]
    merged = None
    for n, br in enumerate((lru_ref, diff_ref, swa_ref)):
        gate = _sigmoid(gl_ref[:, n * D:(n + 1) * D].astype(F32))
        term = gate * _dot(br[...], wb_ref[n])
        merged = term if merged is None else merged + term
    out = _dot(merged.astype(BF16), wo_ref[...])
    o_ref[...] = x_ref[...] + mod_ref[0, 2:3, :] * out


def _merge(x2, mod, o_lru, o_diff, o_swa, gl, w_branch, w_out, seq):
    T, D = x2.shape
    tm = min(MERGE_TM, seq)
    per_b = seq // tm
    row = lambda i: (i, 0)
    return pl.pallas_call(
        _merge_kernel,
        out_shape=jax.ShapeDtypeStruct((T, D), F32),
        grid=(T // tm,),
        in_specs=[
            pl.BlockSpec((tm, D), row),
            pl.BlockSpec((1, 6, D), lambda i: (i // per_b, 0, 0)),
            pl.BlockSpec((tm, BRANCH_WIDTH), row),
            pl.BlockSpec((tm, BRANCH_WIDTH), row),
            pl.BlockSpec((tm, BRANCH_WIDTH), row),
            pl.BlockSpec((tm, N_BRANCHES * D), row),
            pl.BlockSpec(w_branch.shape, lambda i: (0, 0, 0)),
            pl.BlockSpec(w_out.shape, lambda i: (0, 0)),
        ],
        out_specs=pl.BlockSpec((tm, D), row),
        compiler_params=_params("arbitrary"),
        name="merge_outproj",
    )(x2, mod, o_lru, o_diff, o_swa, gl, w_branch, w_out)


def _router_kernel(x_ref, mod_ref, gain_ref, wr_ref, br_ref, tri_ref,
                   hp_ref, e_ref, pt_ref, rank_ref, cnt_ref, run_ref):
    @pl.when(pl.program_id(0) == 0)
    def _():
        run_ref[...] = jnp.zeros_like(run_ref)

    x = x_ref[...]
    tm, D = x.shape
    h = _rms_mod(x, gain_ref[...], mod_ref[0, 4:5, :], mod_ref[0, 3:4, :])
    bits = lax.bitcast_convert_type(h.astype(BF16).astype(F32), jnp.uint32)
    hp_ref[...] = (bits[:, :D // 2] >> 16) | bits[:, D // 2:]

    logits = _dot_nt(wr_ref[...], h, precision=HIGHEST) + br_ref[...]
    eidx = lax.broadcasted_iota(jnp.int32, logits.shape, 0)
    vals, idxs, hots = [], [], []
    for _ in range(TOP_K):
        mx = jnp.max(logits, axis=0, keepdims=True)
        sel = jnp.min(jnp.where(logits == mx, eidx, N_EXPERTS), axis=0, keepdims=True)
        hot = eidx == sel
        vals.append(mx)
        idxs.append(sel)
        hots.append(hot)
        logits = jnp.where(hot, -jnp.inf, logits)
    e_ref[...] = jnp.concatenate(idxs, axis=0)
    top_v = jnp.concatenate(vals, axis=0)
    ex = jnp.exp(top_v - top_v[0:1])
    p = ex / jnp.sum(ex, axis=0, keepdims=True)
    pt_ref[...] = jnp.concatenate([p, jnp.zeros((LANES - TOP_K, tm), F32)], axis=0).T

    member = hots[0]
    for hot in hots[1:]:
        member = member | hot
    member = jnp.where(member, 1.0, 0.0)
    before = _dot(member.astype(BF16), tri_ref[...]) + run_ref[:, 0:1]
    ranks = [jnp.sum(jnp.where(hot, before, 0.0), axis=0, keepdims=True) for hot in hots]
    rank_ref[...] = jnp.concatenate(ranks, axis=0).astype(jnp.int32)
    run_ref[...] = run_ref[...] + jnp.sum(member, axis=1, keepdims=True)
    cnt_ref[...] = run_ref[...]


def _router(x2, mod, gain, w_router_t, b_router, seq):
    T, D = x2.shape
    tm = min(ROUTER_TM, seq)
    per_b = seq // tm
    tri = (jnp.arange(tm)[:, None] < jnp.arange(tm)[None, :]).astype(BF16)
    const = lambda i: (0, 0)
    return pl.pallas_call(
        _router_kernel,
        out_shape=[jax.ShapeDtypeStruct((T, D // 2), jnp.uint32),
                   jax.ShapeDtypeStruct((TOP_K, T), jnp.int32),
                   jax.ShapeDtypeStruct((T, LANES), F32),
                   jax.ShapeDtypeStruct((TOP_K, T), jnp.int32),
                   jax.ShapeDtypeStruct((N_EXPERTS, LANES), F32)],
        grid=(T // tm,),
        in_specs=[
            pl.BlockSpec((tm, D), lambda i: (i, 0)),
            pl.BlockSpec((1, 6, D), lambda i: (i // per_b, 0, 0)),
            pl.BlockSpec((1, D), const),
            pl.BlockSpec((N_EXPERTS, D), const),
            pl.BlockSpec((N_EXPERTS, 1), const),
            pl.BlockSpec((tm, tm), const),
        ],
        out_specs=[pl.BlockSpec((tm, D // 2), lambda i: (i, 0)),
                   pl.BlockSpec((TOP_K, tm), lambda i: (0, i)),
                   pl.BlockSpec((tm, LANES), lambda i: (i, 0)),
                   pl.BlockSpec((TOP_K, tm), lambda i: (0, i)),
                   pl.BlockSpec((N_EXPERTS, LANES), const)],
        scratch_shapes=[pltpu.VMEM((N_EXPERTS, LANES), F32)],
        compiler_params=_params("arbitrary"),
        name="router",
    )(x2, mod, gain, w_router_t, b_router, tri)


def _tile_rows(idx, tm):
    K, T = idx.shape
    nb = T // tm
    return idx.reshape(K, nb, tm).transpose(1, 0, 2).reshape(nb, 1, K * tm)


def _dispatch_kernel(dest_ref, hp_ref, xs_in_ref, xs_ref, sem, *, tm):
    del xs_in_ref
    for k in range(TOP_K):
        def issue(j, carry, k=k):
            d = dest_ref[0, 0, k * tm + j]
            pltpu.make_async_copy(hp_ref.at[pl.ds(j, 1), :], xs_ref.at[pl.ds(d, 1), :], sem).start()
            return carry

        lax.fori_loop(0, tm, issue, 0, unroll=8)

    def drain(r, carry):
        pltpu.make_async_copy(hp_ref.at[pl.ds(0, 1), :], xs_ref.at[pl.ds(0, 1), :], sem).wait()
        return carry

    lax.fori_loop(0, TOP_K * tm, drain, 0, unroll=8)


def _dispatch(dest, hp, n_rows, seq):
    T, W = hp.shape
    tm = min(DISP_TM, seq)
    nb = T // tm
    return pl.pallas_call(
        functools.partial(_dispatch_kernel, tm=tm),
        out_shape=jax.ShapeDtypeStruct((n_rows, W), hp.dtype),
        grid=(nb,),
        in_specs=[
            pl.BlockSpec((1, 1, TOP_K * tm), lambda i: (i, 0, 0), memory_space=pltpu.SMEM),
            pl.BlockSpec((tm, W), lambda i: (i, 0)),
            pl.BlockSpec(memory_space=pl.ANY),
        ],
        out_specs=pl.BlockSpec(memory_space=pl.ANY),
        scratch_shapes=[pltpu.SemaphoreType.DMA],
        input_output_aliases={2: 0},
        compiler_params=_params("arbitrary"),
        name="moe_dispatch",
    )(_tile_rows(dest, tm), hp, jnp.zeros((n_rows, W), hp.dtype))


def _expert_kernel(be_ref, nused_ref, xs_ref, w1_ref, b1_ref, w2_ref, b2_ref, y_ref, act_ref):
    i = pl.program_id(0)

    @pl.when(i < nused_ref[0])
    def _():
        tm, half = xs_ref.shape
        words = xs_ref[...]
        xa = lax.bitcast_convert_type(words << 16, F32).astype(BF16)
        xb = lax.bitcast_convert_type(words & jnp.uint32(0xFFFF0000), F32).astype(BF16)
        even = lax.broadcasted_iota(jnp.int32, (tm, LANES), 1) % 2 == 0
        for q in range(w1_ref.shape[2] // (2 * LANES)):
            cs = slice(2 * LANES * q, 2 * LANES * (q + 1))
            hq = (_dot(xa, w1_ref[0, 0:half, cs]) + _dot(xb, w1_ref[0, half:2 * half, cs])
                  + b1_ref[0, :, cs])
            lo, hi = hq[:, :LANES], hq[:, LANES:]
            glu = jnp.where(even, lo, pltpu.roll(hi, 1, axis=1))
            lin = jnp.where(even, pltpu.roll(lo, LANES - 1, axis=1), hi)
            glu = jnp.minimum(glu, SWIGLU_LIMIT)
            lin = jnp.clip(lin, -SWIGLU_LIMIT, SWIGLU_LIMIT)
            act = glu * _sigmoid(SWIGLU_ALPHA * glu) * (lin + 1.0)
            act_ref[:, LANES * q:LANES * (q + 1)] = act.astype(BF16)
        y_ref[...] = _dot(act_ref[...], w2_ref[0]) + b2_ref[0]

    @pl.when(i >= nused_ref[0])
    def _():
        y_ref[...] = jnp.zeros(y_ref.shape, y_ref.dtype)


def _experts(block_expert, n_used, xs, w1, b1, w2p, b2):
    n_rows, half = xs.shape
    E, D, F2 = w1.shape
    F = F2 // 2
    tm = MOE_TM
    nb = n_rows // tm
    wmap = lambda i, be, nu: (be[i], 0, 0)
    rmap = lambda i, be, nu: (jnp.minimum(i, nu[0] - 1), 0)
    return pl.pallas_call(
        _expert_kernel,
        out_shape=jax.ShapeDtypeStruct((n_rows, D), F32),
        grid_spec=pltpu.PrefetchScalarGridSpec(
            num_scalar_prefetch=2,
            grid=(nb,),
            in_specs=[
                pl.BlockSpec((tm, half), rmap),
                pl.BlockSpec((1, D, F2), wmap),
                pl.BlockSpec((1, 1, F2), wmap),
                pl.BlockSpec((1, F, D), wmap),
                pl.BlockSpec((1, 1, D), wmap),
            ],
            out_specs=pl.BlockSpec((tm, D), lambda i, be, nu: (i, 0)),
            scratch_shapes=[pltpu.VMEM((tm, F), BF16)],
        ),
        compiler_params=_params("arbitrary"),
        name="moe_experts",
    )(block_expert, n_used, xs, w1, b1, w2p, b2)


def _combine_kernel(dest_ref, y_hbm, pt_ref, x_ref, mod_ref, o_ref, ybuf, sem, *, tm):
    n = TOP_K * tm

    def issue(r, carry):
        d = dest_ref[0, 0, r]
        pltpu.make_async_copy(y_hbm.at[pl.ds(d, 1), :], ybuf.at[pl.ds(r, 1), :], sem).start()
        return carry

    lax.fori_loop(0, n, issue, 0, unroll=8)

    def drain(r, carry):
        pltpu.make_async_copy(y_hbm.at[pl.ds(0, 1), :], ybuf.at[pl.ds(0, 1), :], sem).wait()
        return carry

    lax.fori_loop(0, n, drain, 0, unroll=8)

    acc = ybuf[0:tm, :] * pt_ref[:, 0:1]
    for k in range(1, TOP_K):
        acc = acc + ybuf[k * tm:(k + 1) * tm, :] * pt_ref[:, k:k + 1]
    o_ref[...] = x_ref[...] + mod_ref[0, 5:6, :] * acc


def _combine(dest, y, pt, x2, mod, seq):
    T, D = x2.shape
    tm = min(COMB_TM, seq)
    per_b = seq // tm
    nb = T // tm
    return pl.pallas_call(
        functools.partial(_combine_kernel, tm=tm),
        out_shape=jax.ShapeDtypeStruct((T, D), F32),
        grid=(nb,),
        in_specs=[
            pl.BlockSpec((1, 1, TOP_K * tm), lambda i: (i, 0, 0), memory_space=pltpu.SMEM),
            pl.BlockSpec(memory_space=pl.ANY),
            pl.BlockSpec((tm, LANES), lambda i: (i, 0)),
            pl.BlockSpec((tm, D), lambda i: (i, 0)),
            pl.BlockSpec((1, 6, D), lambda i: (i // per_b, 0, 0)),
        ],
        out_specs=pl.BlockSpec((tm, D), lambda i: (i, 0)),
        scratch_shapes=[pltpu.VMEM((TOP_K * tm, D), F32), pltpu.SemaphoreType.DMA],
        compiler_params=_params("arbitrary"),
        name="moe_combine",
    )(_tile_rows(dest, tm), y, pt, x2, mod)


def _t5_bucket(dist):
    exact = REL_BUCKETS // 2
    log_ratio = jnp.log(jnp.maximum(dist, 1).astype(F32) / exact) / math.log(REL_MAX_DIST / exact)
    large = exact + (log_ratio * (REL_BUCKETS - exact)).astype(jnp.int32)
    return jnp.where(dist < exact, dist, jnp.minimum(large, REL_BUCKETS - 1))


def _diff_bias_tiles(rel_bias, tb):
    table = rel_bias[:, :DIFF_HEADS].astype(F32)
    r = jnp.arange(tb)[:, None]
    c = jnp.arange(tb)[None, :]
    far = table[REL_BUCKETS - 1]

    def tile(dist):
        b = table[_t5_bucket(jnp.maximum(dist, 0))] - far
        return jnp.moveaxis(b, -1, 0)

    diag = jnp.where((r - c >= 0)[None], tile(r - c), NEG_INF)
    prev = tile(r + tb - c)
    return prev, diag


def _swa_bias_tiles(rel_bias):
    qi = jnp.arange(WINDOW)[:, None]
    kj = jnp.arange(2 * WINDOW)[None, :]
    dist = WINDOW + qi - kj
    ok = (dist >= 0) & (dist < WINDOW)
    bias = rel_bias[_t5_bucket(jnp.clip(dist, 0, WINDOW - 1))][..., DIFF_HEADS:].astype(F32)
    bias = jnp.where(ok[..., None], bias, NEG_INF)
    bias = bias.reshape(WINDOW, 2 * WINDOW, SWA_KV_HEADS, SWA_GROUP).transpose(2, 3, 0, 1)
    bias = bias.reshape(SWA_KV_HEADS, SWA_GROUP * WINDOW, 2 * WINDOW)
    return bias[..., :WINDOW], bias[..., WINDOW:]


def _block_diag(w):
    nb, n, _ = w.shape
    eye = jnp.eye(nb, dtype=w.dtype)
    return (eye[:, None, :, None] * w[:, :, None, :]).reshape(nb * n, nb * n)


def _routing(counts, top_e, rank, tm):
    K, T = top_e.shape
    counts = counts.astype(jnp.int32)
    padded = (counts + tm - 1) // tm * tm
    pad_ends = jnp.cumsum(padded)
    pad_starts = pad_ends - padded
    onehot = top_e[..., None] == jnp.arange(N_EXPERTS, dtype=jnp.int32)
    dest = rank + jnp.sum(jnp.where(onehot, pad_starts, 0), axis=-1)
    n_rows = K * T + N_EXPERTS * tm
    nb = n_rows // tm
    n_used = pad_ends[-1] // tm
    blk = jnp.minimum(jnp.arange(nb), n_used - 1) * tm
    block_expert = jnp.sum(blk[:, None] >= pad_ends[None, :], axis=-1)
    return block_expert.astype(jnp.int32), n_used.astype(jnp.int32).reshape(1), dest.astype(jnp.int32), n_rows


def kernel(x, c, w_ada, b_ada, norm_mix, norm_ffn, w_in, conv_w, conv_b, lru_wa, lru_ba, lru_wx, lru_bx,
           lru_lambda, diff_qnorm, diff_knorm, diff_lambda, diff_subln, swa_qnorm, swa_knorm, swa_sinks,
           rel_bias, w_branch, w_out, w_router, b_router, w1, b1, w2, b2):
    B, S, D = x.shape
    L = w_ada.shape[0]
    T = B * S
    tb = min(ATT_TB, S)

    mods = _adaln(c, w_ada, b_ada)
    bprev, bdiag = _diff_bias_tiles(rel_bias, tb)
    sbias_prev, sbias_cur = _swa_bias_tiles(rel_bias)
    gidx = jnp.arange(512) // DIFF_HD
    gmat = jnp.where(gidx[:, None] == gidx[None, :], 1.0 / DIFF_HD, 0.0).astype(BF16)

    x2 = x.reshape(T, D)
    for l in range(L):
        lambda_init = 0.8 - 0.6 * math.exp(-0.3 * l)
        mod = mods[l]
        qkg = jnp.stack([
            jnp.tile(diff_qnorm[l], 8) * DIFF_HD ** -0.5,
            jnp.tile(diff_knorm[l], 8),
            jnp.tile(swa_qnorm[l], 8) * SWA_HD ** -0.5,
            jnp.tile(swa_knorm[l], 8),
        ]).astype(F32)
        xg, q1, q2, dk, dv, sq, sk, sv, gl = _inproj(
            x2, mod, norm_mix[l].reshape(1, D), w_in[l].astype(BF16), gmat, qkg, S)

        wg = jnp.concatenate([_block_diag(lru_wa[l]), _block_diag(lru_wx[l])], axis=1).astype(BF16)
        bg = jnp.concatenate([lru_ba[l], lru_bx[l]]).reshape(1, 2 * LRU_WIDTH)
        sp = jax.nn.softplus(-lru_lambda[l].astype(F32)).reshape(1, LRU_WIDTH)
        o_lru = _rglru(xg, conv_w[l], conv_b[l].reshape(1, LRU_WIDTH), wg, bg, sp, B, S)

        o_diff = _diff_attn(q1, q2, dk, dv, bprev, bdiag, diff_lambda[l],
                            diff_subln[l].reshape(1, 2 * DIFF_HD), lambda_init, B, S)

        sinks = jnp.repeat(swa_sinks[l].astype(F32).reshape(SWA_KV_HEADS, SWA_GROUP), WINDOW, axis=1)
        o_swa = _swa(sq, sk, sv, sbias_prev, sbias_cur, sinks.reshape(SWA_KV_HEADS, -1, 1), B, S)

        x2 = _merge(x2, mod, o_lru, o_diff, o_swa, gl, w_branch[l].astype(BF16), w_out[l].astype(BF16), S)

        hp, top_e, pt, rank, counts = _router(x2, mod, norm_ffn[l].reshape(1, D), w_router[l].T,
                                              b_router[l].reshape(N_EXPERTS, 1), S)
        block_expert, n_used, dest, n_rows = _routing(counts[:, 0], top_e, rank, MOE_TM)
        xs = _dispatch(dest, hp, n_rows, S)
        E, F, _ = w2[l].shape
        half = LANES // 2
        w2p = (w2[l].astype(BF16).reshape(E, F // (2 * half), 2, half, D)
               .transpose(0, 1, 3, 2, 4).reshape(E, F, D))
        y = _experts(block_expert, n_used, xs, w1[l].astype(BF16), b1[l][:, None, :], w2p, b2[l][:, None, :])
        x2 = _combine(dest, y, pt, x2, mod, S)
    return x2.reshape(B, S, D)
```

```python
import functools
import math

import jax
import jax.numpy as jnp
from jax import lax
from jax.experimental import pallas as pl
from jax.experimental.pallas import tpu as pltpu

F32 = jnp.float32
BF16 = jnp.bfloat16

LRU_WIDTH = 512
LRU_BLOCKS = 8
LRU_C = 8.0
CONV_W = 4
DIFF_HEADS = 4
DIFF_HD = 64
SWA_HEADS = 8
SWA_KV_HEADS = 2
SWA_GROUP = SWA_HEADS // SWA_KV_HEADS
SWA_HD = 64
WINDOW = 128
N_BRANCHES = 3
BRANCH_WIDTH = 512
REL_BUCKETS = 32
REL_MAX_DIST = 128
N_EXPERTS = 32
TOP_K = 4
SWIGLU_LIMIT = 7.0
SWIGLU_ALPHA = 1.702
NORM_EPS = 1e-6
NEG_INF = -1e30

VMEM_LIMIT_BYTES = 56 * 1024 * 1024
LANES = 128

ADALN_TN = 1536
PROJ_TM = 512
LRU_TS = 512
ATT_TB = 512
SWA_TS = 512
MERGE_TM = 512
ROUTER_TM = 512
DISP_TM = 512
MOE_TM = 256
COMB_TM = 128

HIGHEST = lax.Precision.HIGHEST


def _params(*sem):
    return pltpu.CompilerParams(dimension_semantics=sem, vmem_limit_bytes=VMEM_LIMIT_BYTES)


def _dot(a, b, **kw):
    return jnp.dot(a, b, preferred_element_type=F32, **kw)


def _dot_nt(a, b, **kw):
    return lax.dot_general(a, b, (((1,), (1,)), ((), ())), preferred_element_type=F32, **kw)


def _sigmoid(x):
    return 1.0 / (1.0 + jnp.exp(-x))


def _adaln_kernel(c_ref, w_ref, b_ref, o_ref):
    c = c_ref[...]
    cond = c * _sigmoid(c)
    o_ref[0] = _dot(cond, w_ref[0], precision=HIGHEST) + b_ref[0]


def _adaln(c, w_ada, b_ada):
    L, D, N = w_ada.shape
    B = c.shape[0]
    rows = 8
    cp = jnp.zeros((rows, D), F32).at[:B].set(c)
    out = pl.pallas_call(
        _adaln_kernel,
        out_shape=jax.ShapeDtypeStruct((L, rows, N), F32),
        grid=(L, N // ADALN_TN),
        in_specs=[
            pl.BlockSpec((rows, D), lambda l, j: (0, 0)),
            pl.BlockSpec((1, D, ADALN_TN), lambda l, j: (l, 0, j)),
            pl.BlockSpec((1, 1, ADALN_TN), lambda l, j: (l, 0, j)),
        ],
        out_specs=pl.BlockSpec((1, rows, ADALN_TN), lambda l, j: (l, 0, j)),
        compiler_params=_params("arbitrary", "arbitrary"),
        name="adaln",
    )(cp, w_ada, b_ada.reshape(L, 1, N))
    return out[:, :B].reshape(L, B, 6, D)


def _rms_mod(x, gain, scale, shift):
    ms = jnp.mean(x * x, axis=-1, keepdims=True)
    return (x * lax.rsqrt(ms + NORM_EPS) * gain) * (1.0 + scale) + shift


def _group_rms(x, gmat):
    sq = x * x
    hi = sq.astype(BF16)
    lo = (sq - hi.astype(F32)).astype(BF16)
    ms = _dot(hi, gmat) + _dot(lo, gmat)
    return x * lax.rsqrt(ms + NORM_EPS)


def _inproj_kernel(x_ref, mod_ref, gain_ref, w_ref, wvt_ref, gmat_ref, qkg_ref,
                   xg_ref, q1_ref, q2_ref, dk_ref, dvt_ref, sq_ref, sk_ref, sv_ref, gl_ref):
    x = x_ref[...]
    h = _rms_mod(x, gain_ref[...], mod_ref[0, 1:2, :], mod_ref[0, 0:1, :]).astype(BF16)
    gmat = gmat_ref[...]

    xg_ref[...] = _dot(h, w_ref[:, 0:1024]).astype(BF16)

    dd = _dot(h, w_ref[:, 1024:2048])
    qn = _group_rms(dd[:, 0:512], gmat) * qkg_ref[0:1, :]
    kn = _group_rms(dd[:, 512:1024], gmat) * qkg_ref[1:2, :]
    lane = lax.broadcasted_iota(jnp.int32, qn.shape, 1) % LANES
    q1_ref[...] = jnp.where(lane < DIFF_HD, qn, 0.0).astype(BF16)
    q2_ref[...] = jnp.where(lane >= DIFF_HD, qn, 0.0).astype(BF16)
    dk_ref[...] = kn.astype(BF16)
    dvt_ref[0] = _dot_nt(wvt_ref[...], h).astype(BF16)

    ss = _dot(h, w_ref[:, 2560:3328])
    sq_ref[...] = (_group_rms(ss[:, 0:512], gmat) * qkg_ref[2:3, :]).astype(BF16)
    sk_ref[...] = (_group_rms(ss[:, 512:640], gmat[0:128, 0:128]) * qkg_ref[3:4, 0:128]).astype(BF16)
    sv_ref[...] = ss[:, 640:768].astype(BF16)

    gl_ref[...] = _dot(h, w_ref[:, 3328:6400]).astype(BF16)


def _inproj(x2, mod, gain, w_in, wvt, gmat, qkg, seq):
    T, D = x2.shape
    tm = min(PROJ_TM, seq)
    per_b = seq // tm
    VW = wvt.shape[0]
    widths = (1024, 512, 512, 512, None, 512, 128, 128, 3072)
    row = lambda i: (i, 0)
    const = lambda i: (0, 0)
    vt_shape = jax.ShapeDtypeStruct((T // seq, VW, seq), BF16)
    vt_spec = pl.BlockSpec((1, VW, tm), lambda i: (i // per_b, 0, i % per_b))
    return pl.pallas_call(
        _inproj_kernel,
        out_shape=[vt_shape if w is None else jax.ShapeDtypeStruct((T, w), BF16) for w in widths],
        grid=(T // tm,),
        in_specs=[
            pl.BlockSpec((tm, D), row),
            pl.BlockSpec((1, 6, D), lambda i: (i // per_b, 0, 0)),
            pl.BlockSpec((1, D), const),
            pl.BlockSpec(w_in.shape, const, pipeline_mode=pl.Buffered(1)),
            pl.BlockSpec(wvt.shape, const),
            pl.BlockSpec(gmat.shape, const),
            pl.BlockSpec(qkg.shape, const),
        ],
        out_specs=[vt_spec if w is None else pl.BlockSpec((tm, w), row) for w in widths],
        compiler_params=_params("arbitrary"),
        name="inproj",
    )(x2, mod, gain, w_in, wvt, gmat, qkg)


def _rglru_kernel(xg_ref, cw_ref, cb_ref, wg_ref, bg_ref, sp_ref, o_ref, ext_ref, hc_ref):
    ts = xg_ref.shape[0]
    C = LRU_WIDTH

    @pl.when(pl.program_id(1) == 0)
    def _():
        ext_ref[0:8, :] = jnp.zeros((8, C), F32)
        hc_ref[...] = jnp.zeros_like(hc_ref)

    xr = xg_ref[:, 0:C].astype(F32)
    ext_ref[8:8 + ts, :] = xr
    xc = cb_ref[...] + xr * cw_ref[CONV_W - 1:CONV_W, :]
    for back in range(1, CONV_W):
        tap = CONV_W - 1 - back
        xc = xc + ext_ref[8 - back:8 - back + ts, :] * cw_ref[tap:tap + 1, :]
    ext_ref[0:8, :] = xr[ts - 8:ts, :]

    gates = _dot(xc.astype(BF16), wg_ref[...]) + bg_ref[...]
    r = _sigmoid(gates[:, 0:C])
    gi = _sigmoid(gates[:, C:2 * C])
    log_a = (-LRU_C) * r * sp_ref[...]
    a = jnp.exp(log_a)
    b = xc * gi * jnp.sqrt(1.0 - a * a)

    rows = lax.broadcasted_iota(jnp.int32, (ts, C), 0)
    d = 1
    while d < ts:
        keep = rows >= d
        a_sh = pltpu.roll(a, d, axis=0)
        b_sh = pltpu.roll(b, d, axis=0)
        b = jnp.where(keep, a * b_sh + b, b)
        a = jnp.where(keep, a * a_sh, a)
        d *= 2
    h = b + a * hc_ref[...]
    hc_ref[...] = h[ts - 1:ts, :]

    gr = xg_ref[:, C:2 * C].astype(F32)
    gelu = 0.5 * gr * (1.0 + jnp.tanh(math.sqrt(2.0 / math.pi) * (gr + 0.044715 * gr * gr * gr)))
    o_ref[...] = (h * gelu).astype(BF16)


def _rglru(xg, conv_w, conv_b, wg, bg, softplus_neg_lam, batch, seq):
    T = xg.shape[0]
    C = LRU_WIDTH
    ts = min(LRU_TS, seq)
    per_b = seq // ts
    const = lambda b, i: (0, 0)
    return pl.pallas_call(
        _rglru_kernel,
        out_shape=jax.ShapeDtypeStruct((T, C), BF16),
        grid=(batch, per_b),
        in_specs=[
            pl.BlockSpec((ts, 2 * C), lambda b, i: (b * per_b + i, 0)),
            pl.BlockSpec((CONV_W, C), const),
            pl.BlockSpec((1, C), const),
            pl.BlockSpec((C, 2 * C), const),
            pl.BlockSpec((1, 2 * C), const),
            pl.BlockSpec((1, C), const),
        ],
        out_specs=pl.BlockSpec((ts, C), lambda b, i: (b * per_b + i, 0)),
        scratch_shapes=[pltpu.VMEM((ts + 8, C), F32), pltpu.VMEM((1, C), F32)],
        compiler_params=_params("arbitrary", "arbitrary"),
        name="rglru",
    )(xg, conv_w, conv_b, wg, bg, softplus_neg_lam)


def _diff_attn_kernel(q1_ref, q2_ref, k_ref, vt_ref, bprev_ref, bdiag_ref, lam_ref, sub_ref, o_ref,
                      m_ref, l_ref, acc_ref, *, tb, lambda_init):
    i = pl.program_id(2)
    q = (q1_ref[0], q2_ref[0])

    m_ref[...] = jnp.full(m_ref.shape, NEG_INF, F32)
    l_ref[...] = jnp.zeros(l_ref.shape, F32)
    acc_ref[...] = jnp.zeros(acc_ref.shape, F32)

    def attend(start, bias):
        kb = k_ref[0, pl.ds(start, tb), :]
        vb = vt_ref[0, :, pl.ds(start, tb)]
        for mp in range(2):
            s = _dot_nt(kb, q[mp])
            if bias is not None:
                s = s + bias
            m_old = m_ref[mp]
            m_new = jnp.maximum(m_old, jnp.max(s, axis=0, keepdims=True))
            alpha = jnp.exp(m_old - m_new)
            p = jnp.exp(s - m_new)
            l_ref[mp] = alpha * l_ref[mp] + jnp.sum(p, axis=0, keepdims=True)
            acc_ref[mp] = alpha * acc_ref[mp] + _dot(vb, p.astype(BF16))
            m_ref[mp] = m_new

    def far(j, carry):
        attend(pl.multiple_of(j * tb, tb), None)
        return carry

    lax.fori_loop(0, jnp.maximum(i - 1, 0), far, 0)

    @pl.when(i > 0)
    def _():
        attend(pl.multiple_of((i - 1) * tb, tb), bprev_ref[0])

    attend(pl.multiple_of(i * tb, tb), bdiag_ref[0])

    lv = lam_ref[...]
    lam = (jnp.exp(jnp.sum(lv[0:1] * lv[1:2], axis=-1, keepdims=True))
           - jnp.exp(jnp.sum(lv[2:3] * lv[3:4], axis=-1, keepdims=True)) + lambda_init)
    o = acc_ref[0] / l_ref[0] - lam * (acc_ref[1] / l_ref[1])
    ms = jnp.mean(o * o, axis=0, keepdims=True)
    o = o * lax.rsqrt(ms + NORM_EPS) * sub_ref[...] * (1.0 - lambda_init)
    o_ref[0] = o.T.astype(BF16)


def _diff_attn(q1, q2, k, vt, bprev, bdiag, lam_vecs, subln, lambda_init, batch, seq):
    W = DIFF_HEADS * 2 * DIFF_HD
    tb = min(ATT_TB, seq)
    shp = (batch, seq, W)
    qspec = pl.BlockSpec((1, tb, LANES), lambda b, h, i: (b, i, h))
    kspec = pl.BlockSpec((1, seq, LANES), lambda b, h, i: (b, 0, h))
    vspec = pl.BlockSpec((1, LANES, seq), lambda b, h, i: (b, h, 0))
    bspec = pl.BlockSpec((1, tb, tb), lambda b, h, i: (h, 0, 0))
    const = lambda b, h, i: (0, 0)
    out = pl.pallas_call(
        functools.partial(_diff_attn_kernel, tb=tb, lambda_init=lambda_init),
        out_shape=jax.ShapeDtypeStruct(shp, BF16),
        grid=(batch, DIFF_HEADS, seq // tb),
        in_specs=[qspec, qspec, kspec, vspec, bspec, bspec,
                  pl.BlockSpec((4, DIFF_HD), const), pl.BlockSpec((2 * DIFF_HD, 1), const)],
        out_specs=qspec,
        scratch_shapes=[pltpu.VMEM((2, 1, tb), F32), pltpu.VMEM((2, 1, tb), F32),
                        pltpu.VMEM((2, LANES, tb), F32)],
        compiler_params=_params("arbitrary", "arbitrary", "arbitrary"),
        name="diff_attn",
    )(q1.reshape(shp), q2.reshape(shp), k.reshape(shp), vt, bprev, bdiag, lam_vecs, subln)
    return out.reshape(batch * seq, W)


def _swa_kernel(q_ref, k_ref, v_ref, bp_ref, bc_ref, sink_ref, o_ref, *, ts):
    i = pl.program_id(1)
    blk = WINDOW
    for sub in range(ts // blk):
        start = i * ts + sub * blk
        has_prev = start > 0
        pstart = pl.multiple_of(jnp.maximum(start - blk, 0), blk)
        cstart = pl.multiple_of(start, blk)
        kp = k_ref[0, pl.ds(pstart, blk), :]
        kc = k_ref[0, pl.ds(cstart, blk), :]
        vp = v_ref[0, pl.ds(pstart, blk), :]
        vc = v_ref[0, pl.ds(cstart, blk), :]
        qs = q_ref[0, sub * blk:(sub + 1) * blk, :]
        outs = []
        for hk in range(SWA_KV_HEADS):
            c0 = hk * SWA_GROUP * SWA_HD
            qh = jnp.concatenate(
                [qs[:, c0 + g * SWA_HD:c0 + (g + 1) * SWA_HD] for g in range(SWA_GROUP)], axis=0)
            ksl = slice(hk * SWA_HD, (hk + 1) * SWA_HD)
            s_p = _dot_nt(qh, kp[:, ksl]) + bp_ref[hk]
            s_p = jnp.where(has_prev, s_p, NEG_INF)
            s_c = _dot_nt(qh, kc[:, ksl]) + bc_ref[hk]
            sink = sink_ref[hk]
            m = jnp.maximum(jnp.maximum(jnp.max(s_p, axis=-1, keepdims=True),
                                        jnp.max(s_c, axis=-1, keepdims=True)), sink)
            p_p = jnp.exp(s_p - m)
            p_c = jnp.exp(s_c - m)
            den = (jnp.sum(p_p, axis=-1, keepdims=True) + jnp.sum(p_c, axis=-1, keepdims=True)
                   + jnp.exp(sink - m))
            o = _dot(p_p.astype(BF16), vp[:, ksl]) + _dot(p_c.astype(BF16), vc[:, ksl])
            o = o / den
            outs.extend(o[g * blk:(g + 1) * blk, :] for g in range(SWA_GROUP))
        o_ref[0, sub * blk:(sub + 1) * blk, :] = jnp.concatenate(outs, axis=1).astype(BF16)


def _swa(q, k, v, bias_prev, bias_cur, sinks, batch, seq):
    ts = min(SWA_TS, seq)
    WQ = SWA_HEADS * SWA_HD
    WK = SWA_KV_HEADS * SWA_HD
    rows = SWA_GROUP * WINDOW
    const3 = lambda b, i: (0, 0, 0)
    out = pl.pallas_call(
        functools.partial(_swa_kernel, ts=ts),
        out_shape=jax.ShapeDtypeStruct((batch, seq, WQ), BF16),
        grid=(batch, seq // ts),
        in_specs=[
            pl.BlockSpec((1, ts, WQ), lambda b, i: (b, i, 0)),
            pl.BlockSpec((1, seq, WK), lambda b, i: (b, 0, 0)),
            pl.BlockSpec((1, seq, WK), lambda b, i: (b, 0, 0)),
            pl.BlockSpec((SWA_KV_HEADS, rows, WINDOW), const3),
            pl.BlockSpec((SWA_KV_HEADS, rows, WINDOW), const3),
            pl.BlockSpec((SWA_KV_HEADS, rows, 1), const3),
        ],
        out_specs=pl.BlockSpec((1, ts, WQ), lambda b, i: (b, i, 0)),
        compiler_params=_params("arbitrary", "arbitrary"),
        name="swa_attn",
    )(q.reshape(batch, seq, WQ), k.reshape(batch, seq, WK), v.reshape(batch, seq, WK),
      bias_prev, bias_cur, sinks)
    return out.reshape(batch * seq, WQ)


def _merge_kernel(x_ref, mod_ref, lru_ref, diff_ref, swa_ref, gl_ref, wb_ref, wo_ref, o_ref):
    D = x_ref.shape[1]
    merged = None
    for n, br in enumerate((lru_ref, diff_ref, swa_ref)):
        gate = _sigmoid(gl_ref[:, n * D:(n + 1) * D].astype(F32))
        term = gate * _dot(br[...], wb_ref[n])
        merged = term if merged is None else merged + term
    out = _dot(merged.astype(BF16), wo_ref[...])
    o_ref[...] = x_ref[...] + mod_ref[0, 2:3, :] * out


def _merge(x2, mod, o_lru, o_diff, o_swa, gl, w_branch, w_out, seq):
    T, D = x2.shape
    tm = min(MERGE_TM, seq)
    per_b = seq // tm
    row = lambda i: (i, 0)
    return pl.pallas_call(
        _merge_kernel,
        out_shape=jax.ShapeDtypeStruct((T, D), F32),
        grid=(T // tm,),
        in_specs=[
            pl.BlockSpec((tm, D), row),
            pl.BlockSpec((1, 6, D), lambda i: (i // per_b, 0, 0)),
            pl.BlockSpec((tm, BRANCH_WIDTH), row),
            pl.BlockSpec((tm, BRANCH_WIDTH), row),
            pl.BlockSpec((tm, BRANCH_WIDTH), row),
            pl.BlockSpec((tm, N_BRANCHES * D), row),
            pl.BlockSpec(w_branch.shape, lambda i: (0, 0, 0)),
            pl.BlockSpec(w_out.shape, lambda i: (0, 0)),
        ],
        out_specs=pl.BlockSpec((tm, D), row),
        compiler_params=_params("arbitrary"),
        name="merge_outproj",
    )(x2, mod, o_lru, o_diff, o_swa, gl, w_branch, w_out)


def _router_kernel(x_ref, mod_ref, gain_ref, wr_ref, br_ref, tri_ref,
                   hp_ref, e_ref, pt_ref, rank_ref, cnt_ref, run_ref):
    @pl.when(pl.program_id(0) == 0)
    def _():
        run_ref[...] = jnp.zeros_like(run_ref)

    x = x_ref[...]
    tm, D = x.shape
    h = _rms_mod(x, gain_ref[...], mod_ref[0, 4:5, :], mod_ref[0, 3:4, :])
    bits = lax.bitcast_convert_type(h.astype(BF16).astype(F32), jnp.uint32)
    hp_ref[...] = (bits[:, :D // 2] >> 16) | bits[:, D // 2:]

    logits = _dot_nt(wr_ref[...], h, precision=HIGHEST) + br_ref[...]
    eidx = lax.broadcasted_iota(jnp.int32, logits.shape, 0)
    vals, idxs, hots = [], [], []
    for _ in range(TOP_K):
        mx = jnp.max(logits, axis=0, keepdims=True)
        sel = jnp.min(jnp.where(logits == mx, eidx, N_EXPERTS), axis=0, keepdims=True)
        hot = eidx == sel
        vals.append(mx)
        idxs.append(sel)
        hots.append(hot)
        logits = jnp.where(hot, -jnp.inf, logits)
    e_ref[...] = jnp.concatenate(idxs, axis=0)
    top_v = jnp.concatenate(vals, axis=0)
    ex = jnp.exp(top_v - top_v[0:1])
    p = ex / jnp.sum(ex, axis=0, keepdims=True)
    pt_ref[...] = jnp.concatenate([p, jnp.zeros((LANES - TOP_K, tm), F32)], axis=0).T

    member = hots[0]
    for hot in hots[1:]:
        member = member | hot
    member = jnp.where(member, 1.0, 0.0)
    before = _dot(member.astype(BF16), tri_ref[...]) + run_ref[:, 0:1]
    ranks = [jnp.sum(jnp.where(hot, before, 0.0), axis=0, keepdims=True) for hot in hots]
    rank_ref[...] = jnp.concatenate(ranks, axis=0).astype(jnp.int32)
    run_ref[...] = run_ref[...] + jnp.sum(member, axis=1, keepdims=True)
    cnt_ref[...] = run_ref[...]


def _router(x2, mod, gain, w_router_t, b_router, seq):
    T, D = x2.shape
    tm = min(ROUTER_TM, seq)
    per_b = seq // tm
    tri = (jnp.arange(tm)[:, None] < jnp.arange(tm)[None, :]).astype(BF16)
    const = lambda i: (0, 0)
    return pl.pallas_call(
        _router_kernel,
        out_shape=[jax.ShapeDtypeStruct((T, D // 2), jnp.uint32),
                   jax.ShapeDtypeStruct((TOP_K, T), jnp.int32),
                   jax.ShapeDtypeStruct((T, LANES), F32),
                   jax.ShapeDtypeStruct((TOP_K, T), jnp.int32),
                   jax.ShapeDtypeStruct((N_EXPERTS, LANES), F32)],
        grid=(T // tm,),
        in_specs=[
            pl.BlockSpec((tm, D), lambda i: (i, 0)),
            pl.BlockSpec((1, 6, D), lambda i: (i // per_b, 0, 0)),
            pl.BlockSpec((1, D), const),
            pl.BlockSpec((N_EXPERTS, D), const),
            pl.BlockSpec((N_EXPERTS, 1), const),
            pl.BlockSpec((tm, tm), const),
        ],
        out_specs=[pl.BlockSpec((tm, D // 2), lambda i: (i, 0)),
                   pl.BlockSpec((TOP_K, tm), lambda i: (0, i)),
                   pl.BlockSpec((tm, LANES), lambda i: (i, 0)),
                   pl.BlockSpec((TOP_K, tm), lambda i: (0, i)),
                   pl.BlockSpec((N_EXPERTS, LANES), const)],
        scratch_shapes=[pltpu.VMEM((N_EXPERTS, LANES), F32)],
        compiler_params=_params("arbitrary"),
        name="router",
    )(x2, mod, gain, w_router_t, b_router, tri)


def _tile_rows(idx, tm):
    K, T = idx.shape
    nb = T // tm
    return idx.reshape(K, nb, tm).transpose(1, 0, 2).reshape(nb, 1, K * tm)


def _dispatch_kernel(nused_ref, tok_ref, hp_hbm, xs_ref, sem, *, tm, per_moe):
    i = pl.program_id(0)

    @pl.when(i * per_moe < nused_ref[0])
    def _():
        def issue(r, carry):
            t = tok_ref[0, 0, r]
            pltpu.make_async_copy(hp_hbm.at[pl.ds(t, 1), :], xs_ref.at[pl.ds(r, 1), :], sem).start()
            return carry

        lax.fori_loop(0, tm, issue, 0, unroll=8)

        def drain(r, carry):
            pltpu.make_async_copy(hp_hbm.at[pl.ds(0, 1), :], xs_ref.at[pl.ds(0, 1), :], sem).wait()
            return carry

        lax.fori_loop(0, tm, drain, 0, unroll=8)

    @pl.when(i * per_moe >= nused_ref[0])
    def _():
        xs_ref[...] = jnp.zeros(xs_ref.shape, xs_ref.dtype)


def _dispatch(n_used, row_token, hp):
    T, W = hp.shape
    n_rows = row_token.shape[0]
    tm = DISP_TM
    nb = n_rows // tm
    return pl.pallas_call(
        functools.partial(_dispatch_kernel, tm=tm, per_moe=tm // MOE_TM),
        out_shape=jax.ShapeDtypeStruct((n_rows, W), hp.dtype),
        grid_spec=pltpu.PrefetchScalarGridSpec(
            num_scalar_prefetch=1,
            grid=(nb,),
            in_specs=[
                pl.BlockSpec((1, 1, tm), lambda i, nu: (i, 0, 0), memory_space=pltpu.SMEM),
                pl.BlockSpec(memory_space=pl.ANY),
            ],
            out_specs=pl.BlockSpec((tm, W), lambda i, nu: (i, 0)),
            scratch_shapes=[pltpu.SemaphoreType.DMA],
        ),
        compiler_params=_params("arbitrary"),
        name="moe_dispatch",
    )(n_used, row_token.reshape(nb, 1, tm), hp)


def _expert_kernel(be_ref, nused_ref, xs_ref, w1_ref, b1_ref, w2_ref, b2_ref, y_ref, act_ref):
    i = pl.program_id(0)

    @pl.when(i < nused_ref[0])
    def _():
        tm, half = xs_ref.shape
        words = xs_ref[...]
        xa = lax.bitcast_convert_type(words << 16, F32).astype(BF16)
        xb = lax.bitcast_convert_type(words & jnp.uint32(0xFFFF0000), F32).astype(BF16)
        even = lax.broadcasted_iota(jnp.int32, (tm, LANES), 1) % 2 == 0
        for q in range(w1_ref.shape[2] // (2 * LANES)):
            cs = slice(2 * LANES * q, 2 * LANES * (q + 1))
            hq = (_dot(xa, w1_ref[0, 0:half, cs]) + _dot(xb, w1_ref[0, half:2 * half, cs])
                  + b1_ref[0, :, cs])
            lo, hi = hq[:, :LANES], hq[:, LANES:]
            glu = jnp.where(even, lo, pltpu.roll(hi, 1, axis=1))
            lin = jnp.where(even, pltpu.roll(lo, LANES - 1, axis=1), hi)
            glu = jnp.minimum(glu, SWIGLU_LIMIT)
            lin = jnp.clip(lin, -SWIGLU_LIMIT, SWIGLU_LIMIT)
            act = glu * _sigmoid(SWIGLU_ALPHA * glu) * (lin + 1.0)
            act_ref[:, LANES * q:LANES * (q + 1)] = act.astype(BF16)
        y_ref[...] = _dot(act_ref[...], w2_ref[0]) + b2_ref[0]

    @pl.when(i >= nused_ref[0])
    def _():
        y_ref[...] = jnp.zeros(y_ref.shape, y_ref.dtype)


def _experts(block_expert, n_used, xs, w1, b1, w2p, b2):
    n_rows, half = xs.shape
    E, D, F2 = w1.shape
    F = F2 // 2
    tm = MOE_TM
    nb = n_rows // tm
    wmap = lambda i, be, nu: (be[i], 0, 0)
    rmap = lambda i, be, nu: (jnp.maximum(jnp.minimum(i, nu[0] - 1), 0), 0)
    return pl.pallas_call(
        _expert_kernel,
        out_shape=jax.ShapeDtypeStruct((n_rows, D), F32),
        grid_spec=pltpu.PrefetchScalarGridSpec(
            num_scalar_prefetch=2,
            grid=(nb,),
            in_specs=[
                pl.BlockSpec((tm, half), rmap),
                pl.BlockSpec((1, D, F2), wmap),
                pl.BlockSpec((1, 1, F2), wmap),
                pl.BlockSpec((1, F, D), wmap),
                pl.BlockSpec((1, 1, D), wmap),
            ],
            out_specs=pl.BlockSpec((tm, D), lambda i, be, nu: (i, 0)),
            scratch_shapes=[pltpu.VMEM((tm, F), BF16)],
        ),
        compiler_params=_params("arbitrary"),
        name="moe_experts",
    )(block_expert, n_used, xs, w1, b1, w2p, b2)


def _combine_kernel(dest_ref, y_hbm, pt_ref, x_ref, mod_ref, o_ref, ybuf, sem, *, tm):
    n = TOP_K * tm

    def issue(r, carry):
        d = dest_ref[0, 0, r]
        pltpu.make_async_copy(y_hbm.at[pl.ds(d, 1), :], ybuf.at[pl.ds(r, 1), :], sem).start()
        return carry

    lax.fori_loop(0, n, issue, 0, unroll=8)

    def drain(r, carry):
        pltpu.make_async_copy(y_hbm.at[pl.ds(0, 1), :], ybuf.at[pl.ds(0, 1), :], sem).wait()
        return carry

    lax.fori_loop(0, n, drain, 0, unroll=8)

    acc = ybuf[0:tm, :] * pt_ref[:, 0:1]
    for k in range(1, TOP_K):
        acc = acc + ybuf[k * tm:(k + 1) * tm, :] * pt_ref[:, k:k + 1]
    o_ref[...] = x_ref[...] + mod_ref[0, 5:6, :] * acc


def _combine(dest, y, pt, x2, mod, seq):
    T, D = x2.shape
    tm = min(COMB_TM, seq)
    per_b = seq // tm
    nb = T // tm
    return pl.pallas_call(
        functools.partial(_combine_kernel, tm=tm),
        out_shape=jax.ShapeDtypeStruct((T, D), F32),
        grid=(nb,),
        in_specs=[
            pl.BlockSpec((1, 1, TOP_K * tm), lambda i: (i, 0, 0), memory_space=pltpu.SMEM),
            pl.BlockSpec(memory_space=pl.ANY),
            pl.BlockSpec((tm, LANES), lambda i: (i, 0)),
            pl.BlockSpec((tm, D), lambda i: (i, 0)),
            pl.BlockSpec((1, 6, D), lambda i: (i // per_b, 0, 0)),
        ],
        out_specs=pl.BlockSpec((tm, D), lambda i: (i, 0)),
        scratch_shapes=[pltpu.VMEM((TOP_K * tm, D), F32), pltpu.SemaphoreType.DMA],
        compiler_params=_params("arbitrary"),
        name="moe_combine",
    )(_tile_rows(dest, tm), y, pt, x2, mod)


def _t5_bucket(dist):
    exact = REL_BUCKETS // 2
    log_ratio = jnp.log(jnp.maximum(dist, 1).astype(F32) / exact) / math.log(REL_MAX_DIST / exact)
    large = exact + (log_ratio * (REL_BUCKETS - exact)).astype(jnp.int32)
    return jnp.where(dist < exact, dist, jnp.minimum(large, REL_BUCKETS - 1))


def _diff_bias_tiles(rel_bias, tb):
    table = rel_bias[:, :DIFF_HEADS].astype(F32)
    shifted = table - table[REL_BUCKETS - 1]
    kk = jnp.arange(tb)[:, None]
    qq = jnp.arange(tb)[None, :]

    def tile(dist):
        return _bucket_lookup(shifted, _t5_bucket(jnp.maximum(dist, 0)))

    diag = jnp.where((qq - kk >= 0)[None], tile(qq - kk), NEG_INF)
    prev = tile(qq + tb - kk)
    return prev, diag


def _bucket_lookup(table, bucket):
    out = jnp.zeros((table.shape[1],) + bucket.shape, F32)
    for b in range(table.shape[0]):
        out = out + jnp.where(bucket[None] == b, table[b].reshape((-1,) + (1,) * bucket.ndim), 0.0)
    return out


def _swa_bias_tiles(rel_bias):
    qi = jnp.arange(WINDOW)[:, None]
    kj = jnp.arange(2 * WINDOW)[None, :]
    dist = WINDOW + qi - kj
    ok = (dist >= 0) & (dist < WINDOW)
    bias = _bucket_lookup(rel_bias[:, DIFF_HEADS:].astype(F32),
                          _t5_bucket(jnp.clip(dist, 0, WINDOW - 1)))
    bias = jnp.where(ok[None], bias, NEG_INF)
    bias = bias.reshape(SWA_KV_HEADS, SWA_GROUP * WINDOW, 2 * WINDOW)
    return bias[..., :WINDOW], bias[..., WINDOW:]


def _block_diag(w):
    nb, n, _ = w.shape
    eye = jnp.eye(nb, dtype=w.dtype)
    return (eye[:, None, :, None] * w[:, :, None, :]).reshape(nb * n, nb * n)


def _routing(counts, top_e, rank, tm):
    K, T = top_e.shape
    counts = counts.astype(jnp.int32)
    padded = (counts + tm - 1) // tm * tm
    pad_ends = jnp.cumsum(padded)
    pad_starts = pad_ends - padded
    onehot = top_e[..., None] == jnp.arange(N_EXPERTS, dtype=jnp.int32)
    dest = rank + jnp.sum(jnp.where(onehot, pad_starts, 0), axis=-1)
    n_rows = K * T + N_EXPERTS * tm
    nb = n_rows // tm
    n_used = pad_ends[-1] // tm
    blk = jnp.minimum(jnp.arange(nb), n_used - 1) * tm
    block_expert = jnp.sum(blk[:, None] >= pad_ends[None, :], axis=-1)
    fill_ends = jnp.cumsum(padded - counts)
    fill_expert = jnp.sum(jnp.arange(N_EXPERTS * tm)[:, None] >= fill_ends[None, :], axis=-1)
    keys = jnp.concatenate([(top_e * T + jnp.arange(T, dtype=jnp.int32)[None, :]).reshape(-1),
                            (fill_expert * T + (T - 1)).astype(jnp.int32)])
    row_token = jnp.sort(keys) % T
    return (block_expert.astype(jnp.int32), n_used.astype(jnp.int32).reshape(1), dest.astype(jnp.int32),
            row_token.astype(jnp.int32))


def kernel(x, c, w_ada, b_ada, norm_mix, norm_ffn, w_in, conv_w, conv_b, lru_wa, lru_ba, lru_wx, lru_bx,
           lru_lambda, diff_qnorm, diff_knorm, diff_lambda, diff_subln, swa_qnorm, swa_knorm, swa_sinks,
           rel_bias, w_branch, w_out, w_router, b_router, w1, b1, w2, b2):
    B, S, D = x.shape
    L = w_ada.shape[0]
    T = B * S
    tb = min(ATT_TB, S)

    mods = _adaln(c, w_ada, b_ada)
    bprev, bdiag = _diff_bias_tiles(rel_bias, tb)
    sbias_prev, sbias_cur = _swa_bias_tiles(rel_bias)
    gidx = jnp.arange(512) // DIFF_HD
    gmat = jnp.where(gidx[:, None] == gidx[None, :], 1.0 / DIFF_HD, 0.0).astype(BF16)

    x2 = x.reshape(T, D)
    for l in range(L):
        lambda_init = 0.8 - 0.6 * math.exp(-0.3 * l)
        mod = mods[l]
        qkg = jnp.stack([
            jnp.tile(diff_qnorm[l], 8) * DIFF_HD ** -0.5,
            jnp.tile(diff_knorm[l], 8),
            jnp.tile(swa_qnorm[l], 8) * SWA_HD ** -0.5,
            jnp.tile(swa_knorm[l], 8),
        ]).astype(F32)
        w_in_l = w_in[l].astype(BF16)
        xg, q1, q2, dk, dvt, sq, sk, sv, gl = _inproj(
            x2, mod, norm_mix[l].reshape(1, D), w_in_l, w_in_l[:, 2048:2560].T, gmat, qkg, S)

        wg = jnp.concatenate([_block_diag(lru_wa[l]), _block_diag(lru_wx[l])], axis=1).astype(BF16)
        bg = jnp.concatenate([lru_ba[l], lru_bx[l]]).reshape(1, 2 * LRU_WIDTH)
        sp = jax.nn.softplus(-lru_lambda[l].astype(F32)).reshape(1, LRU_WIDTH)
        o_lru = _rglru(xg, conv_w[l], conv_b[l].reshape(1, LRU_WIDTH), wg, bg, sp, B, S)

        o_diff = _diff_attn(q1, q2, dk, dvt, bprev, bdiag, diff_lambda[l],
                            diff_subln[l].reshape(2 * DIFF_HD, 1), lambda_init, B, S)

        sinks = jnp.repeat(swa_sinks[l].astype(F32).reshape(SWA_KV_HEADS, SWA_GROUP), WINDOW, axis=1)
        o_swa = _swa(sq, sk, sv, sbias_prev, sbias_cur, sinks.reshape(SWA_KV_HEADS, -1, 1), B, S)

        x2 = _merge(x2, mod, o_lru, o_diff, o_swa, gl, w_branch[l].astype(BF16), w_out[l].astype(BF16), S)

        hp, top_e, pt, rank, counts = _router(x2, mod, norm_ffn[l].reshape(1, D), w_router[l].T,
                                              b_router[l].reshape(N_EXPERTS, 1), S)
        block_expert, n_used, dest, row_token = _routing(counts[:, 0], top_e, rank, MOE_TM)
        xs = _dispatch(n_used, row_token, hp)
        E, F, _ = w2[l].shape
        half = LANES // 2
        w2p = (w2[l].astype(BF16).reshape(E, F // (2 * half), 2, half, D)
               .transpose(0, 1, 3, 2, 4).reshape(E, F, D))
        y = _experts(block_expert, n_used, xs, w1[l].astype(BF16), b1[l][:, None, :], w2p, b2[l][:, None, :])
        x2 = _combine(dest, y, pt, x2, mod, S)
    return x2.reshape(B, S, D)
```

```python
import functools
import math

import jax
import jax.numpy as jnp
from jax import lax
from jax.experimental import pallas as pl
from jax.experimental.pallas import tpu as pltpu

F32 = jnp.float32
BF16 = jnp.bfloat16

LRU_WIDTH = 512
LRU_BLOCKS = 8
LRU_C = 8.0
CONV_W = 4
DIFF_HEADS = 4
DIFF_HD = 64
SWA_HEADS = 8
SWA_KV_HEADS = 2
SWA_GROUP = SWA_HEADS // SWA_KV_HEADS
SWA_HD = 64
WINDOW = 128
N_BRANCHES = 3
BRANCH_WIDTH = 512
REL_BUCKETS = 32
REL_MAX_DIST = 128
N_EXPERTS = 32
TOP_K = 4
SWIGLU_LIMIT = 7.0
SWIGLU_ALPHA = 1.702
NORM_EPS = 1e-6
NEG_INF = -1e30
LOG2E = math.log2(math.e)

VMEM_LIMIT_BYTES = 56 * 1024 * 1024
LANES = 128

ADALN_TN = 1536
PROJ_TM = 512
LRU_TS = 512
ATT_TB = 512
SWA_TS = 512
MERGE_TM = 512
ROUTER_TM = 512
DISP_TM = 512
MOE_TM = 512
COMB_TM = 128

HIGHEST = lax.Precision.HIGHEST


def _params(*sem):
    return pltpu.CompilerParams(dimension_semantics=sem, vmem_limit_bytes=VMEM_LIMIT_BYTES)


def _dot(a, b, **kw):
    return jnp.dot(a, b, preferred_element_type=F32, **kw)


def _dot_nt(a, b, **kw):
    return lax.dot_general(a, b, (((1,), (1,)), ((), ())), preferred_element_type=F32, **kw)


def _sigmoid(x):
    return 1.0 / (1.0 + jnp.exp(-x))


def _adaln_kernel(c_ref, w_ref, b_ref, o_ref):
    c = c_ref[...]
    cond = c * _sigmoid(c)
    o_ref[0] = _dot(cond, w_ref[0], precision=HIGHEST) + b_ref[0]


def _adaln(c, w_ada, b_ada):
    L, D, N = w_ada.shape
    B = c.shape[0]
    rows = 8
    cp = jnp.zeros((rows, D), F32).at[:B].set(c)
    out = pl.pallas_call(
        _adaln_kernel,
        out_shape=jax.ShapeDtypeStruct((L, rows, N), F32),
        grid=(L, N // ADALN_TN),
        in_specs=[
            pl.BlockSpec((rows, D), lambda l, j: (0, 0)),
            pl.BlockSpec((1, D, ADALN_TN), lambda l, j: (l, 0, j)),
            pl.BlockSpec((1, 1, ADALN_TN), lambda l, j: (l, 0, j)),
        ],
        out_specs=pl.BlockSpec((1, rows, ADALN_TN), lambda l, j: (l, 0, j)),
        compiler_params=_params("arbitrary", "arbitrary"),
        name="adaln",
    )(cp, w_ada, b_ada.reshape(L, 1, N))
    return out[:, :B].reshape(L, B, 6, D)


def _rms_mod(x, gain, scale, shift):
    ms = jnp.mean(x * x, axis=-1, keepdims=True)
    return (x * lax.rsqrt(ms + NORM_EPS) * gain) * (1.0 + scale) + shift


def _group_rms(x, gmat):
    sq = x * x
    hi = sq.astype(BF16)
    lo = (sq - hi.astype(F32)).astype(BF16)
    ms = _dot(hi, gmat) + _dot(lo, gmat)
    return x * lax.rsqrt(ms + NORM_EPS)


def _inproj_kernel(x_ref, mod_ref, gain_ref, w_ref, wvt_ref, gmat_ref, qkg_ref,
                   xg_ref, q1_ref, q2_ref, dk_ref, dvt_ref, sq_ref, sk_ref, sv_ref, gl_ref):
    x = x_ref[...]
    h = _rms_mod(x, gain_ref[...], mod_ref[0, 1:2, :], mod_ref[0, 0:1, :]).astype(BF16)
    gmat = gmat_ref[...]

    xg_ref[...] = _dot(h, w_ref[:, 0:1024]).astype(BF16)

    dd = _dot(h, w_ref[:, 1024:2048])
    qn = _group_rms(dd[:, 0:512], gmat) * qkg_ref[0:1, :]
    kn = _group_rms(dd[:, 512:1024], gmat) * qkg_ref[1:2, :]
    lane = lax.broadcasted_iota(jnp.int32, qn.shape, 1) % LANES
    q1_ref[...] = jnp.where(lane < DIFF_HD, qn, 0.0).astype(BF16)
    q2_ref[...] = jnp.where(lane >= DIFF_HD, qn, 0.0).astype(BF16)
    dk_ref[...] = kn.astype(BF16)
    dvt_ref[0] = _dot_nt(wvt_ref[...], h).astype(BF16)

    ss = _dot(h, w_ref[:, 2560:3328])
    sq_ref[...] = (_group_rms(ss[:, 0:512], gmat) * qkg_ref[2:3, :]).astype(BF16)
    sk_ref[...] = (_group_rms(ss[:, 512:640], gmat[0:128, 0:128]) * qkg_ref[3:4, 0:128]).astype(BF16)
    sv_ref[...] = ss[:, 640:768].astype(BF16)

    gl_ref[...] = _dot(h, w_ref[:, 3328:6400]).astype(BF16)


def _inproj(x2, mod, gain, w_in, wvt, gmat, qkg, seq):
    T, D = x2.shape
    tm = min(PROJ_TM, seq)
    per_b = seq // tm
    VW = wvt.shape[0]
    widths = (1024, 512, 512, 512, None, 512, 128, 128, 3072)
    row = lambda i: (i, 0)
    const = lambda i: (0, 0)
    vt_shape = jax.ShapeDtypeStruct((T // seq, VW, seq), BF16)
    vt_spec = pl.BlockSpec((1, VW, tm), lambda i: (i // per_b, 0, i % per_b))
    return pl.pallas_call(
        _inproj_kernel,
        out_shape=[vt_shape if w is None else jax.ShapeDtypeStruct((T, w), BF16) for w in widths],
        grid=(T // tm,),
        in_specs=[
            pl.BlockSpec((tm, D), row),
            pl.BlockSpec((1, 6, D), lambda i: (i // per_b, 0, 0)),
            pl.BlockSpec((1, D), const),
            pl.BlockSpec(w_in.shape, const, pipeline_mode=pl.Buffered(1)),
            pl.BlockSpec(wvt.shape, const),
            pl.BlockSpec(gmat.shape, const),
            pl.BlockSpec(qkg.shape, const),
        ],
        out_specs=[vt_spec if w is None else pl.BlockSpec((tm, w), row) for w in widths],
        compiler_params=_params("arbitrary"),
        name="inproj",
    )(x2, mod, gain, w_in, wvt, gmat, qkg)


def _rglru_kernel(xg_ref, cw_ref, cb_ref, wg_ref, bg_ref, sp_ref, o_ref, ext_ref, hc_ref):
    ts = xg_ref.shape[0]
    C = LRU_WIDTH

    @pl.when(pl.program_id(1) == 0)
    def _():
        ext_ref[0:8, :] = jnp.zeros((8, C), F32)
        hc_ref[...] = jnp.zeros_like(hc_ref)

    xr = xg_ref[:, 0:C].astype(F32)
    ext_ref[8:8 + ts, :] = xr
    xc = cb_ref[...] + xr * cw_ref[CONV_W - 1:CONV_W, :]
    for back in range(1, CONV_W):
        tap = CONV_W - 1 - back
        xc = xc + ext_ref[8 - back:8 - back + ts, :] * cw_ref[tap:tap + 1, :]
    ext_ref[0:8, :] = xr[ts - 8:ts, :]

    gates = _dot(xc.astype(BF16), wg_ref[...]) + bg_ref[...]
    r = _sigmoid(gates[:, 0:C])
    gi = _sigmoid(gates[:, C:2 * C])
    log_a = (-LRU_C) * r * sp_ref[...]
    a = jnp.exp(log_a)
    b = xc * gi * jnp.sqrt(1.0 - a * a)

    rows = lax.broadcasted_iota(jnp.int32, (ts, C), 0)
    d = 1
    while d < ts:
        keep = rows >= d
        a_sh = pltpu.roll(a, d, axis=0)
        b_sh = pltpu.roll(b, d, axis=0)
        b = jnp.where(keep, a * b_sh + b, b)
        a = jnp.where(keep, a * a_sh, a)
        d *= 2
    h = b + a * hc_ref[...]
    hc_ref[...] = h[ts - 1:ts, :]

    gr = xg_ref[:, C:2 * C].astype(F32)
    gelu = 0.5 * gr * (1.0 + jnp.tanh(math.sqrt(2.0 / math.pi) * (gr + 0.044715 * gr * gr * gr)))
    o_ref[...] = (h * gelu).astype(BF16)


def _rglru(xg, conv_w, conv_b, wg, bg, softplus_neg_lam, batch, seq):
    T = xg.shape[0]
    C = LRU_WIDTH
    ts = min(LRU_TS, seq)
    per_b = seq // ts
    const = lambda b, i: (0, 0)
    return pl.pallas_call(
        _rglru_kernel,
        out_shape=jax.ShapeDtypeStruct((T, C), BF16),
        grid=(batch, per_b),
        in_specs=[
            pl.BlockSpec((ts, 2 * C), lambda b, i: (b * per_b + i, 0)),
            pl.BlockSpec((CONV_W, C), const),
            pl.BlockSpec((1, C), const),
            pl.BlockSpec((C, 2 * C), const),
            pl.BlockSpec((1, 2 * C), const),
            pl.BlockSpec((1, C), const),
        ],
        out_specs=pl.BlockSpec((ts, C), lambda b, i: (b * per_b + i, 0)),
        scratch_shapes=[pltpu.VMEM((ts + 8, C), F32), pltpu.VMEM((1, C), F32)],
        compiler_params=_params("arbitrary", "arbitrary"),
        name="rglru",
    )(xg, conv_w, conv_b, wg, bg, softplus_neg_lam)


def _diff_attn_kernel(q1_ref, q2_ref, k_ref, vt_ref, bprev_ref, bdiag_ref, lam_ref, sub_ref, o_ref,
                      m_ref, l_ref, acc_ref, *, tb, lambda_init):
    i = pl.program_id(2)
    q = (q1_ref[0], q2_ref[0])

    m_ref[...] = jnp.full(m_ref.shape, NEG_INF, F32)
    l_ref[...] = jnp.zeros(l_ref.shape, F32)
    acc_ref[...] = jnp.zeros(acc_ref.shape, F32)

    def attend(start, bias):
        kb = k_ref[0, pl.ds(start, tb), :]
        vb = vt_ref[0, :, pl.ds(start, tb)]
        scores = [_dot_nt(kb, q[mp]) for mp in range(2)]
        for mp in range(2):
            s = scores[mp]
            if bias is not None:
                s = s + bias
            m_old = m_ref[mp]
            m_new = jnp.maximum(m_old, jnp.max(s, axis=0, keepdims=True))
            alpha = jnp.exp2(m_old - m_new)
            p = jnp.exp2(s - m_new)
            l_ref[mp] = alpha * l_ref[mp] + jnp.sum(p, axis=0, keepdims=True)
            acc_ref[mp] = alpha * acc_ref[mp] + _dot(vb, p.astype(BF16))
            m_ref[mp] = m_new

    def far(j, carry):
        attend(pl.multiple_of(j * tb, tb), None)
        return carry

    lax.fori_loop(0, jnp.maximum(i - 1, 0), far, 0)

    @pl.when(i > 0)
    def _():
        attend(pl.multiple_of((i - 1) * tb, tb), bprev_ref[0])

    attend(pl.multiple_of(i * tb, tb), bdiag_ref[0])

    lv = lam_ref[...]
    lam = (jnp.exp(jnp.sum(lv[0:1] * lv[1:2], axis=-1, keepdims=True))
           - jnp.exp(jnp.sum(lv[2:3] * lv[3:4], axis=-1, keepdims=True)) + lambda_init)
    o = acc_ref[0] / l_ref[0] - lam * (acc_ref[1] / l_ref[1])
    ms = jnp.mean(o * o, axis=0, keepdims=True)
    o = o * lax.rsqrt(ms + NORM_EPS) * sub_ref[...] * (1.0 - lambda_init)
    o_ref[0] = o.T.astype(BF16)


def _diff_attn(q1, q2, k, vt, bprev, bdiag, lam_vecs, subln, lambda_init, batch, seq):
    W = DIFF_HEADS * 2 * DIFF_HD
    tb = min(ATT_TB, seq)
    shp = (batch, seq, W)
    qspec = pl.BlockSpec((1, tb, LANES), lambda b, h, i: (b, i, h))
    kspec = pl.BlockSpec((1, seq, LANES), lambda b, h, i: (b, 0, h))
    vspec = pl.BlockSpec((1, LANES, seq), lambda b, h, i: (b, h, 0))
    bspec = pl.BlockSpec((1, tb, tb), lambda b, h, i: (h, 0, 0))
    const = lambda b, h, i: (0, 0)
    out = pl.pallas_call(
        functools.partial(_diff_attn_kernel, tb=tb, lambda_init=lambda_init),
        out_shape=jax.ShapeDtypeStruct(shp, BF16),
        grid=(batch, DIFF_HEADS, seq // tb),
        in_specs=[qspec, qspec, kspec, vspec, bspec, bspec,
                  pl.BlockSpec((4, DIFF_HD), const), pl.BlockSpec((2 * DIFF_HD, 1), const)],
        out_specs=qspec,
        scratch_shapes=[pltpu.VMEM((2, 1, tb), F32), pltpu.VMEM((2, 1, tb), F32),
                        pltpu.VMEM((2, LANES, tb), F32)],
        compiler_params=_params("arbitrary", "arbitrary", "arbitrary"),
        name="diff_attn",
    )(q1.reshape(shp), q2.reshape(shp), k.reshape(shp), vt, bprev, bdiag, lam_vecs, subln)
    return out.reshape(batch * seq, W)


def _swa_kernel(q_ref, k_ref, v_ref, bp_ref, bc_ref, sink_ref, o_ref, *, ts):
    i = pl.program_id(1)
    blk = WINDOW
    for sub in range(ts // blk):
        start = i * ts + sub * blk
        has_prev = start > 0
        pstart = pl.multiple_of(jnp.maximum(start - blk, 0), blk)
        cstart = pl.multiple_of(start, blk)
        kp = k_ref[0, pl.ds(pstart, blk), :]
        kc = k_ref[0, pl.ds(cstart, blk), :]
        vp = v_ref[0, pl.ds(pstart, blk), :]
        vc = v_ref[0, pl.ds(cstart, blk), :]
        qs = q_ref[0, sub * blk:(sub + 1) * blk, :]
        outs = []
        for hk in range(SWA_KV_HEADS):
            c0 = hk * SWA_GROUP * SWA_HD
            qh = jnp.concatenate(
                [qs[:, c0 + g * SWA_HD:c0 + (g + 1) * SWA_HD] for g in range(SWA_GROUP)], axis=0)
            ksl = slice(hk * SWA_HD, (hk + 1) * SWA_HD)
            s_p = _dot_nt(qh, kp[:, ksl]) + bp_ref[hk]
            s_p = jnp.where(has_prev, s_p, NEG_INF)
            s_c = _dot_nt(qh, kc[:, ksl]) + bc_ref[hk]
            sink = sink_ref[hk]
            m = jnp.maximum(jnp.maximum(jnp.max(s_p, axis=-1, keepdims=True),
                                        jnp.max(s_c, axis=-1, keepdims=True)), sink)
            p_p = jnp.exp(s_p - m)
            p_c = jnp.exp(s_c - m)
            den = (jnp.sum(p_p, axis=-1, keepdims=True) + jnp.sum(p_c, axis=-1, keepdims=True)
                   + jnp.exp(sink - m))
            o = _dot(p_p.astype(BF16), vp[:, ksl]) + _dot(p_c.astype(BF16), vc[:, ksl])
            o = o / den
            outs.extend(o[g * blk:(g + 1) * blk, :] for g in range(SWA_GROUP))
        o_ref[0, sub * blk:(sub + 1) * blk, :] = jnp.concatenate(outs, axis=1).astype(BF16)


def _swa(q, k, v, bias_prev, bias_cur, sinks, batch, seq):
    ts = min(SWA_TS, seq)
    WQ = SWA_HEADS * SWA_HD
    WK = SWA_KV_HEADS * SWA_HD
    rows = SWA_GROUP * WINDOW
    const3 = lambda b, i: (0, 0, 0)
    out = pl.pallas_call(
        functools.partial(_swa_kernel, ts=ts),
        out_shape=jax.ShapeDtypeStruct((batch, seq, WQ), BF16),
        grid=(batch, seq // ts),
        in_specs=[
            pl.BlockSpec((1, ts, WQ), lambda b, i: (b, i, 0)),
            pl.BlockSpec((1, seq, WK), lambda b, i: (b, 0, 0)),
            pl.BlockSpec((1, seq, WK), lambda b, i: (b, 0, 0)),
            pl.BlockSpec((SWA_KV_HEADS, rows, WINDOW), const3),
            pl.BlockSpec((SWA_KV_HEADS, rows, WINDOW), const3),
            pl.BlockSpec((SWA_KV_HEADS, rows, 1), const3),
        ],
        out_specs=pl.BlockSpec((1, ts, WQ), lambda b, i: (b, i, 0)),
        compiler_params=_params("arbitrary", "arbitrary"),
        name="swa_attn",
    )(q.reshape(batch, seq, WQ), k.reshape(batch, seq, WK), v.reshape(batch, seq, WK),
      bias_prev, bias_cur, sinks)
    return out.reshape(batch * seq, WQ)


def _merge_kernel(x_ref, mod_ref, lru_ref, diff_ref, swa_ref, gl_ref, wb_ref, wo_ref, o_ref):
    D = x_ref.shape[1]
    merged = None
    for n, br in enumerate((lru_ref, diff_ref, swa_ref)):
        gate = _sigmoid(gl_ref[:, n * D:(n + 1) * D].astype(F32))
        term = gate * _dot(br[...], wb_ref[n])
        merged = term if merged is None else merged + term
    out = _dot(merged.astype(BF16), wo_ref[...])
    o_ref[...] = x_ref[...] + mod_ref[0, 2:3, :] * out


def _merge(x2, mod, o_lru, o_diff, o_swa, gl, w_branch, w_out, seq):
    T, D = x2.shape
    tm = min(MERGE_TM, seq)
    per_b = seq // tm
    row = lambda i: (i, 0)
    return pl.pallas_call(
        _merge_kernel,
        out_shape=jax.ShapeDtypeStruct((T, D), F32),
        grid=(T // tm,),
        in_specs=[
            pl.BlockSpec((tm, D), row),
            pl.BlockSpec((1, 6, D), lambda i: (i // per_b, 0, 0)),
            pl.BlockSpec((tm, BRANCH_WIDTH), row),
            pl.BlockSpec((tm, BRANCH_WIDTH), row),
            pl.BlockSpec((tm, BRANCH_WIDTH), row),
            pl.BlockSpec((tm, N_BRANCHES * D), row),
            pl.BlockSpec(w_branch.shape, lambda i: (0, 0, 0)),
            pl.BlockSpec(w_out.shape, lambda i: (0, 0)),
        ],
        out_specs=pl.BlockSpec((tm, D), row),
        compiler_params=_params("arbitrary"),
        name="merge_outproj",
    )(x2, mod, o_lru, o_diff, o_swa, gl, w_branch, w_out)


def _router_kernel(x_ref, mod_ref, gain_ref, wr_ref, br_ref, tri_ref,
                   hp_ref, e_ref, pt_ref, rank_ref, cnt_ref, run_ref):
    @pl.when(pl.program_id(0) == 0)
    def _():
        run_ref[...] = jnp.zeros_like(run_ref)

    x = x_ref[...]
    tm, D = x.shape
    h = _rms_mod(x, gain_ref[...], mod_ref[0, 4:5, :], mod_ref[0, 3:4, :])
    bits = lax.bitcast_convert_type(h.astype(BF16).astype(F32), jnp.uint32)
    hp_ref[...] = (bits[:, :D // 2] >> 16) | bits[:, D // 2:]

    logits = _dot_nt(wr_ref[...], h, precision=HIGHEST) + br_ref[...]
    eidx = lax.broadcasted_iota(jnp.int32, logits.shape, 0)
    vals, idxs, hots = [], [], []
    for _ in range(TOP_K):
        mx = jnp.max(logits, axis=0, keepdims=True)
        sel = jnp.min(jnp.where(logits == mx, eidx, N_EXPERTS), axis=0, keepdims=True)
        hot = eidx == sel
        vals.append(mx)
        idxs.append(sel)
        hots.append(hot)
        logits = jnp.where(hot, -jnp.inf, logits)
    e_ref[...] = jnp.concatenate(idxs, axis=0)
    top_v = jnp.concatenate(vals, axis=0)
    ex = jnp.exp(top_v - top_v[0:1])
    p = ex / jnp.sum(ex, axis=0, keepdims=True)
    pt_ref[...] = jnp.concatenate([p, jnp.zeros((LANES - TOP_K, tm), F32)], axis=0).T

    member = hots[0]
    for hot in hots[1:]:
        member = member | hot
    member = jnp.where(member, 1.0, 0.0)
    before = _dot(member.astype(BF16), tri_ref[...]) + run_ref[:, 0:1]
    ranks = [jnp.sum(jnp.where(hot, before, 0.0), axis=0, keepdims=True) for hot in hots]
    rank_ref[...] = jnp.concatenate(ranks, axis=0).astype(jnp.int32)
    run_ref[...] = run_ref[...] + jnp.sum(member, axis=1, keepdims=True)
    cnt_ref[...] = run_ref[...]


def _router(x2, mod, gain, w_router_t, b_router, seq):
    T, D = x2.shape
    tm = min(ROUTER_TM, seq)
    per_b = seq // tm
    tri = (jnp.arange(tm)[:, None] < jnp.arange(tm)[None, :]).astype(BF16)
    const = lambda i: (0, 0)
    return pl.pallas_call(
        _router_kernel,
        out_shape=[jax.ShapeDtypeStruct((T, D // 2), jnp.uint32),
                   jax.ShapeDtypeStruct((TOP_K, T), jnp.int32),
                   jax.ShapeDtypeStruct((T, LANES), F32),
                   jax.ShapeDtypeStruct((TOP_K, T), jnp.int32),
                   jax.ShapeDtypeStruct((N_EXPERTS, LANES), F32)],
        grid=(T // tm,),
        in_specs=[
            pl.BlockSpec((tm, D), lambda i: (i, 0)),
            pl.BlockSpec((1, 6, D), lambda i: (i // per_b, 0, 0)),
            pl.BlockSpec((1, D), const),
            pl.BlockSpec((N_EXPERTS, D), const),
            pl.BlockSpec((N_EXPERTS, 1), const),
            pl.BlockSpec((tm, tm), const),
        ],
        out_specs=[pl.BlockSpec((tm, D // 2), lambda i: (i, 0)),
                   pl.BlockSpec((TOP_K, tm), lambda i: (0, i)),
                   pl.BlockSpec((tm, LANES), lambda i: (i, 0)),
                   pl.BlockSpec((TOP_K, tm), lambda i: (0, i)),
                   pl.BlockSpec((N_EXPERTS, LANES), const)],
        scratch_shapes=[pltpu.VMEM((N_EXPERTS, LANES), F32)],
        compiler_params=_params("arbitrary"),
        name="router",
    )(x2, mod, gain, w_router_t, b_router, tri)


def _tile_rows(idx, tm):
    K, T = idx.shape
    nb = T // tm
    return idx.reshape(K, nb, tm).transpose(1, 0, 2).reshape(nb, 1, K * tm)


def _dispatch_kernel(nused_ref, tok_ref, hp_hbm, xs_ref, sem, *, tm, per_moe):
    i = pl.program_id(0)

    @pl.when(i * per_moe < nused_ref[0])
    def _():
        def issue(r, carry):
            t = tok_ref[0, 0, r]
            pltpu.make_async_copy(hp_hbm.at[pl.ds(t, 1), :], xs_ref.at[pl.ds(r, 1), :], sem).start()
            return carry

        lax.fori_loop(0, tm, issue, 0, unroll=8)
        pltpu.make_async_copy(hp_hbm.at[pl.ds(0, tm), :], xs_ref, sem).wait()

    @pl.when(i * per_moe >= nused_ref[0])
    def _():
        xs_ref[...] = jnp.zeros(xs_ref.shape, xs_ref.dtype)


def _dispatch(n_used, row_token, hp):
    T, W = hp.shape
    n_rows = row_token.shape[0]
    tm = DISP_TM
    nb = n_rows // tm
    return pl.pallas_call(
        functools.partial(_dispatch_kernel, tm=tm, per_moe=tm // MOE_TM),
        out_shape=jax.ShapeDtypeStruct((n_rows, W), hp.dtype),
        grid_spec=pltpu.PrefetchScalarGridSpec(
            num_scalar_prefetch=1,
            grid=(nb,),
            in_specs=[
                pl.BlockSpec((1, 1, tm), lambda i, nu: (i, 0, 0), memory_space=pltpu.SMEM),
                pl.BlockSpec(memory_space=pl.ANY),
            ],
            out_specs=pl.BlockSpec((tm, W), lambda i, nu: (i, 0)),
            scratch_shapes=[pltpu.SemaphoreType.DMA],
        ),
        compiler_params=_params("arbitrary"),
        name="moe_dispatch",
    )(n_used, row_token.reshape(nb, 1, tm), hp)


def _expert_kernel(be_ref, nused_ref, xs_ref, w1_ref, b1_ref, w2_ref, b2_ref, y_ref,
                   w1b_ref, w2s_ref, w2b_ref, act_ref):
    i = pl.program_id(0)
    used = i < nused_ref[0]
    chunks = w1_ref.shape[3] // (2 * LANES)
    hl = LANES // 2

    @pl.when(used & ((i == 0) | (be_ref[i] != be_ref[jnp.maximum(i - 1, 0)])))
    def _():
        for q in range(chunks):
            cs = slice(2 * LANES * q, 2 * LANES * (q + 1))
            w1b_ref[:, cs] = w1_ref[0, 0, :, cs].astype(BF16)
        cols = w2_ref.shape[3] // LANES
        for q in range(w2_ref.shape[2] // LANES):
            for s in range(2):
                rows = w2_ref[0, 0, LANES * q + hl * s:LANES * q + hl * (s + 1), :]
                for c in range(cols):
                    w2s_ref[c, pl.ds(LANES * q + s, hl, stride=2), :] = rows[:, LANES * c:LANES * (c + 1)]
        for c in range(cols):
            w2b_ref[:, LANES * c:LANES * (c + 1)] = w2s_ref[c].astype(BF16)

    @pl.when(used)
    def _():
        tm, half = xs_ref.shape
        words = xs_ref[...]
        xa = lax.bitcast_convert_type(words << 16, F32).astype(BF16)
        xb = lax.bitcast_convert_type(words & jnp.uint32(0xFFFF0000), F32).astype(BF16)
        even = lax.broadcasted_iota(jnp.int32, (tm, LANES), 1) % 2 == 0
        for q in range(chunks):
            cs = slice(2 * LANES * q, 2 * LANES * (q + 1))
            hq = (_dot(xa, w1b_ref[0:half, cs]) + _dot(xb, w1b_ref[half:2 * half, cs])
                  + b1_ref[0, 0, :, cs])
            lo, hi = hq[:, :LANES], hq[:, LANES:]
            glu = jnp.where(even, lo, pltpu.roll(hi, 1, axis=1))
            lin = jnp.where(even, pltpu.roll(lo, LANES - 1, axis=1), hi)
            glu = jnp.minimum(glu, SWIGLU_LIMIT)
            lin = jnp.clip(lin, -SWIGLU_LIMIT, SWIGLU_LIMIT)
            act = glu * _sigmoid(SWIGLU_ALPHA * glu) * (lin + 1.0)
            act_ref[:, LANES * q:LANES * (q + 1)] = act.astype(BF16)
        y_ref[...] = _dot(act_ref[...], w2b_ref[...]) + b2_ref[0, 0]

    @pl.when(jnp.logical_not(used))
    def _():
        y_ref[...] = jnp.zeros(y_ref.shape, y_ref.dtype)


def _experts(layer, block_expert, n_used, xs, w1, b1, w2, b2):
    n_rows, half = xs.shape
    _, E, D, F2 = w1.shape
    F = F2 // 2
    tm = MOE_TM
    nb = n_rows // tm
    wmap = lambda i, be, nu: (layer, be[i], 0, 0)
    rmap = lambda i, be, nu: (jnp.maximum(jnp.minimum(i, nu[0] - 1), 0), 0)
    return pl.pallas_call(
        _expert_kernel,
        out_shape=jax.ShapeDtypeStruct((n_rows, D), F32),
        grid_spec=pltpu.PrefetchScalarGridSpec(
            num_scalar_prefetch=2,
            grid=(nb,),
            in_specs=[
                pl.BlockSpec((tm, half), rmap),
                pl.BlockSpec((1, 1, D, F2), wmap),
                pl.BlockSpec((1, 1, 1, F2), wmap),
                pl.BlockSpec((1, 1, F, D), wmap),
                pl.BlockSpec((1, 1, 1, D), wmap),
            ],
            out_specs=pl.BlockSpec((tm, D), lambda i, be, nu: (i, 0)),
            scratch_shapes=[pltpu.VMEM((D, F2), BF16), pltpu.VMEM((D // LANES, F, LANES), F32),
                            pltpu.VMEM((F, D), BF16),
                            pltpu.VMEM((tm, F), BF16)],
        ),
        compiler_params=_params("arbitrary"),
        name="moe_experts",
    )(block_expert, n_used, xs, w1, b1, w2, b2)


def _combine_kernel(dest_ref, dnext_ref, y_hbm, pt_ref, x_ref, mod_ref, o_ref, ybuf, sem, *, tm):
    n = TOP_K * tm
    i = pl.program_id(0)
    slot = i % 2

    def gather(idx_ref, s):
        def issue(r, carry):
            d = idx_ref[0, 0, r]
            pltpu.make_async_copy(y_hbm.at[pl.ds(d, 1), :], ybuf.at[s, pl.ds(r, 1), :], sem.at[s]).start()
            return carry

        lax.fori_loop(0, n, issue, 0, unroll=8)

    @pl.when(i == 0)
    def _():
        gather(dest_ref, 0)

    @pl.when(i + 1 < pl.num_programs(0))
    def _():
        gather(dnext_ref, 1 - slot)

    pltpu.make_async_copy(y_hbm.at[pl.ds(0, n), :], ybuf.at[slot], sem.at[slot]).wait()

    acc = ybuf[slot, 0:tm, :] * pt_ref[:, 0:1]
    for k in range(1, TOP_K):
        acc = acc + ybuf[slot, k * tm:(k + 1) * tm, :] * pt_ref[:, k:k + 1]
    o_ref[...] = x_ref[...] + mod_ref[0, 5:6, :] * acc


def _combine(dest, y, pt, x2, mod, seq):
    T, D = x2.shape
    tm = min(COMB_TM, seq)
    per_b = seq // tm
    nb = T // tm
    rows = _tile_rows(dest, tm)
    return pl.pallas_call(
        functools.partial(_combine_kernel, tm=tm),
        out_shape=jax.ShapeDtypeStruct((T, D), F32),
        grid=(nb,),
        in_specs=[
            pl.BlockSpec((1, 1, TOP_K * tm), lambda i: (i, 0, 0), memory_space=pltpu.SMEM),
            pl.BlockSpec((1, 1, TOP_K * tm), lambda i: (jnp.minimum(i + 1, nb - 1), 0, 0),
                         memory_space=pltpu.SMEM),
            pl.BlockSpec(memory_space=pl.ANY),
            pl.BlockSpec((tm, LANES), lambda i: (i, 0)),
            pl.BlockSpec((tm, D), lambda i: (i, 0)),
            pl.BlockSpec((1, 6, D), lambda i: (i // per_b, 0, 0)),
        ],
        out_specs=pl.BlockSpec((tm, D), lambda i: (i, 0)),
        scratch_shapes=[pltpu.VMEM((2, TOP_K * tm, D), F32), pltpu.SemaphoreType.DMA((2,))],
        compiler_params=_params("arbitrary"),
        name="moe_combine",
    )(rows, rows, y, pt, x2, mod)


def _t5_bucket(dist):
    exact = REL_BUCKETS // 2
    log_ratio = jnp.log(jnp.maximum(dist, 1).astype(F32) / exact) / math.log(REL_MAX_DIST / exact)
    large = exact + (log_ratio * (REL_BUCKETS - exact)).astype(jnp.int32)
    return jnp.where(dist < exact, dist, jnp.minimum(large, REL_BUCKETS - 1))


def _diff_bias_tiles(rel_bias, tb):
    table = rel_bias[:, :DIFF_HEADS].astype(F32)
    shifted = (table - table[REL_BUCKETS - 1]) * LOG2E
    kk = jnp.arange(tb)[:, None]
    qq = jnp.arange(tb)[None, :]

    def tile(dist):
        return _bucket_lookup(shifted, _t5_bucket(jnp.maximum(dist, 0)))

    diag = jnp.where((qq - kk >= 0)[None], tile(qq - kk), NEG_INF)
    prev = tile(qq + tb - kk)
    return prev, diag


def _bucket_lookup(table, bucket):
    out = jnp.zeros((table.shape[1],) + bucket.shape, F32)
    for b in range(table.shape[0]):
        out = out + jnp.where(bucket[None] == b, table[b].reshape((-1,) + (1,) * bucket.ndim), 0.0)
    return out


def _swa_bias_tiles(rel_bias):
    qi = jnp.arange(WINDOW)[:, None]
    kj = jnp.arange(2 * WINDOW)[None, :]
    dist = WINDOW + qi - kj
    ok = (dist >= 0) & (dist < WINDOW)
    bias = _bucket_lookup(rel_bias[:, DIFF_HEADS:].astype(F32),
                          _t5_bucket(jnp.clip(dist, 0, WINDOW - 1)))
    bias = jnp.where(ok[None], bias, NEG_INF)
    bias = bias.reshape(SWA_KV_HEADS, SWA_GROUP * WINDOW, 2 * WINDOW)
    return bias[..., :WINDOW], bias[..., WINDOW:]


def _block_diag(w):
    nb, n, _ = w.shape
    eye = jnp.eye(nb, dtype=w.dtype)
    return (eye[:, None, :, None] * w[:, :, None, :]).reshape(nb * n, nb * n)


def _routing(counts, top_e, rank, tm):
    K, T = top_e.shape
    counts = counts.astype(jnp.int32)
    padded = (counts + tm - 1) // tm * tm
    pad_ends = jnp.cumsum(padded)
    pad_starts = pad_ends - padded
    onehot = top_e[..., None] == jnp.arange(N_EXPERTS, dtype=jnp.int32)
    dest = rank + jnp.sum(jnp.where(onehot, pad_starts, 0), axis=-1)
    n_rows = K * T + N_EXPERTS * tm
    nb = n_rows // tm
    n_used = pad_ends[-1] // tm
    blk = jnp.minimum(jnp.arange(nb), n_used - 1) * tm
    block_expert = jnp.sum(blk[:, None] >= pad_ends[None, :], axis=-1)
    fill_ends = jnp.cumsum(padded - counts)
    fill_expert = jnp.sum(jnp.arange(N_EXPERTS * tm)[:, None] >= fill_ends[None, :], axis=-1)
    keys = jnp.concatenate([(top_e * T + jnp.arange(T, dtype=jnp.int32)[None, :]).reshape(-1),
                            (fill_expert * T + (T - 1)).astype(jnp.int32)])
    row_token = jnp.sort(keys) % T
    return (block_expert.astype(jnp.int32), n_used.astype(jnp.int32).reshape(1), dest.astype(jnp.int32),
            row_token.astype(jnp.int32))


def kernel(x, c, w_ada, b_ada, norm_mix, norm_ffn, w_in, conv_w, conv_b, lru_wa, lru_ba, lru_wx, lru_bx,
           lru_lambda, diff_qnorm, diff_knorm, diff_lambda, diff_subln, swa_qnorm, swa_knorm, swa_sinks,
           rel_bias, w_branch, w_out, w_router, b_router, w1, b1, w2, b2):
    B, S, D = x.shape
    L = w_ada.shape[0]
    T = B * S
    tb = min(ATT_TB, S)

    mods = _adaln(c, w_ada, b_ada)
    bprev, bdiag = _diff_bias_tiles(rel_bias, tb)
    sbias_prev, sbias_cur = _swa_bias_tiles(rel_bias)
    gidx = jnp.arange(512) // DIFF_HD
    gmat = jnp.where(gidx[:, None] == gidx[None, :], 1.0 / DIFF_HD, 0.0).astype(BF16)

    x2 = x.reshape(T, D)
    for l in range(L):
        lambda_init = 0.8 - 0.6 * math.exp(-0.3 * l)
        mod = mods[l]
        qkg = jnp.stack([
            jnp.tile(diff_qnorm[l], 8) * (DIFF_HD ** -0.5 * LOG2E),
            jnp.tile(diff_knorm[l], 8),
            jnp.tile(swa_qnorm[l], 8) * SWA_HD ** -0.5,
            jnp.tile(swa_knorm[l], 8),
        ]).astype(F32)
        w_in_l = w_in[l].astype(BF16)
        xg, q1, q2, dk, dvt, sq, sk, sv, gl = _inproj(
            x2, mod, norm_mix[l].reshape(1, D), w_in_l, w_in_l[:, 2048:2560].T, gmat, qkg, S)

        wg = jnp.concatenate([_block_diag(lru_wa[l]), _block_diag(lru_wx[l])], axis=1).astype(BF16)
        bg = jnp.concatenate([lru_ba[l], lru_bx[l]]).reshape(1, 2 * LRU_WIDTH)
        sp = jax.nn.softplus(-lru_lambda[l].astype(F32)).reshape(1, LRU_WIDTH)
        o_lru = _rglru(xg, conv_w[l], conv_b[l].reshape(1, LRU_WIDTH), wg, bg, sp, B, S)

        o_diff = _diff_attn(q1, q2, dk, dvt, bprev, bdiag, diff_lambda[l],
                            diff_subln[l].reshape(2 * DIFF_HD, 1), lambda_init, B, S)

        sinks = jnp.repeat(swa_sinks[l].astype(F32).reshape(SWA_KV_HEADS, SWA_GROUP), WINDOW, axis=1)
        o_swa = _swa(sq, sk, sv, sbias_prev, sbias_cur, sinks.reshape(SWA_KV_HEADS, -1, 1), B, S)

        x2 = _merge(x2, mod, o_lru, o_diff, o_swa, gl, w_branch[l].astype(BF16), w_out[l].astype(BF16), S)

        hp, top_e, pt, rank, counts = _router(x2, mod, norm_ffn[l].reshape(1, D), w_router[l].T,
                                              b_router[l].reshape(N_EXPERTS, 1), S)
        block_expert, n_used, dest, row_token = _routing(counts[:, 0], top_e, rank, MOE_TM)
        xs = _dispatch(n_used, row_token, hp)
        y = _experts(l, block_expert, n_used, xs, w1, b1[:, :, None, :], w2, b2[:, :, None, :])
        x2 = _combine(dest, y, pt, x2, mod, S)
    return x2.reshape(B, S, D)
```

```python
import functools
import math

import jax
import jax.numpy as jnp
from jax import lax
from jax.experimental import pallas as pl
from jax.experimental.pallas import tpu as pltpu

F32 = jnp.float32
BF16 = jnp.bfloat16

LRU_WIDTH = 512
LRU_BLOCKS = 8
LRU_C = 8.0
CONV_W = 4
DIFF_HEADS = 4
DIFF_HD = 64
SWA_HEADS = 8
SWA_KV_HEADS = 2
SWA_GROUP = SWA_HEADS // SWA_KV_HEADS
SWA_HD = 64
WINDOW = 128
N_BRANCHES = 3
BRANCH_WIDTH = 512
REL_BUCKETS = 32
REL_MAX_DIST = 128
N_EXPERTS = 32
TOP_K = 4
SWIGLU_LIMIT = 7.0
SWIGLU_ALPHA = 1.702
NORM_EPS = 1e-6
NEG_INF = -1e30
LOG2E = math.log2(math.e)

VMEM_LIMIT_BYTES = 56 * 1024 * 1024
LANES = 128
SUBLANES = 8

ADALN_TN = 1536
PROJ_TM = 512
LRU_TS = 512
ATT_TB = 512
SWA_TS = 512
MERGE_TM = 512
ROUTER_TM = 512
DISP_TM = 512
MOE_TM = 512
COMB_TM = 128

HIGHEST = lax.Precision.HIGHEST


def _params(*sem):
    return pltpu.CompilerParams(dimension_semantics=sem, vmem_limit_bytes=VMEM_LIMIT_BYTES)


def _dot(a, b, **kw):
    return jnp.dot(a, b, preferred_element_type=F32, **kw)


def _dot_nt(a, b, **kw):
    return lax.dot_general(a, b, (((1,), (1,)), ((), ())), preferred_element_type=F32, **kw)


def _sigmoid(x):
    return 1.0 / (1.0 + jnp.exp(-x))


def _adaln_kernel(c_ref, w_ref, b_ref, o_ref):
    c = c_ref[...]
    cond = c * _sigmoid(c)
    o_ref[0] = _dot(cond, w_ref[0], precision=HIGHEST) + b_ref[0]


def _adaln(c, w_ada, b_ada):
    L, D, N = w_ada.shape
    B = c.shape[0]
    rows = 8
    cp = jnp.zeros((rows, D), F32).at[:B].set(c)
    out = pl.pallas_call(
        _adaln_kernel,
        out_shape=jax.ShapeDtypeStruct((L, rows, N), F32),
        grid=(L, N // ADALN_TN),
        in_specs=[
            pl.BlockSpec((rows, D), lambda l, j: (0, 0)),
            pl.BlockSpec((1, D, ADALN_TN), lambda l, j: (l, 0, j)),
            pl.BlockSpec((1, 1, ADALN_TN), lambda l, j: (l, 0, j)),
        ],
        out_specs=pl.BlockSpec((1, rows, ADALN_TN), lambda l, j: (l, 0, j)),
        compiler_params=_params("arbitrary", "arbitrary"),
        name="adaln",
    )(cp, w_ada, b_ada.reshape(L, 1, N))
    return out[:, :B].reshape(L, B, 6, D)


def _rms_mod(x, gain, scale, shift):
    ms = jnp.mean(x * x, axis=-1, keepdims=True)
    return (x * lax.rsqrt(ms + NORM_EPS) * gain) * (1.0 + scale) + shift


def _group_rms(x, gmat):
    sq = x * x
    hi = sq.astype(BF16)
    lo = (sq - hi.astype(F32)).astype(BF16)
    ms = _dot(hi, gmat) + _dot(lo, gmat)
    return x * lax.rsqrt(ms + NORM_EPS)


def _inproj_kernel(x_ref, mod_ref, gain_ref, w_ref, wvt_ref, gmat_ref, qkg_ref,
                   xg_ref, q1_ref, q2_ref, dk_ref, dvt_ref, sq_ref, sk_ref, sv_ref, gl_ref):
    x = x_ref[...]
    h = _rms_mod(x, gain_ref[...], mod_ref[0, 1:2, :], mod_ref[0, 0:1, :]).astype(BF16)
    gmat = gmat_ref[...]

    xg_ref[...] = _dot(h, w_ref[:, 0:1024]).astype(BF16)

    dd = _dot(h, w_ref[:, 1024:2048])
    qn = _group_rms(dd[:, 0:512], gmat) * qkg_ref[0:1, :]
    kn = _group_rms(dd[:, 512:1024], gmat) * qkg_ref[1:2, :]
    lane = lax.broadcasted_iota(jnp.int32, qn.shape, 1) % LANES
    q1_ref[...] = jnp.where(lane < DIFF_HD, qn, 0.0).astype(BF16)
    q2_ref[...] = jnp.where(lane >= DIFF_HD, qn, 0.0).astype(BF16)
    dk_ref[...] = kn.astype(BF16)
    dvt_ref[0] = _dot_nt(wvt_ref[...], h).astype(BF16)

    ss = _dot(h, w_ref[:, 2560:3328])
    sq_ref[...] = (_group_rms(ss[:, 0:512], gmat) * qkg_ref[2:3, :]).astype(BF16)
    sk_ref[...] = (_group_rms(ss[:, 512:640], gmat[0:128, 0:128]) * qkg_ref[3:4, 0:128]).astype(BF16)
    sv_ref[...] = ss[:, 640:768].astype(BF16)

    gl_ref[...] = _dot(h, w_ref[:, 3328:6400]).astype(BF16)


def _inproj(x2, mod, gain, w_in, wvt, gmat, qkg, seq):
    T, D = x2.shape
    tm = min(PROJ_TM, seq)
    per_b = seq // tm
    VW = wvt.shape[0]
    widths = (1024, 512, 512, 512, None, 512, 128, 128, 3072)
    row = lambda i: (i, 0)
    const = lambda i: (0, 0)
    vt_shape = jax.ShapeDtypeStruct((T // seq, VW, seq), BF16)
    vt_spec = pl.BlockSpec((1, VW, tm), lambda i: (i // per_b, 0, i % per_b))
    return pl.pallas_call(
        _inproj_kernel,
        out_shape=[vt_shape if w is None else jax.ShapeDtypeStruct((T, w), BF16) for w in widths],
        grid=(T // tm,),
        in_specs=[
            pl.BlockSpec((tm, D), row),
            pl.BlockSpec((1, 6, D), lambda i: (i // per_b, 0, 0)),
            pl.BlockSpec((1, D), const),
            pl.BlockSpec(w_in.shape, const, pipeline_mode=pl.Buffered(1)),
            pl.BlockSpec(wvt.shape, const),
            pl.BlockSpec(gmat.shape, const),
            pl.BlockSpec(qkg.shape, const),
        ],
        out_specs=[vt_spec if w is None else pl.BlockSpec((tm, w), row) for w in widths],
        compiler_params=_params("arbitrary"),
        name="inproj",
    )(x2, mod, gain, w_in, wvt, gmat, qkg)


def _rglru_kernel(xg_ref, cw_ref, cb_ref, wg_ref, bg_ref, sp_ref, o_ref, ext_ref, hc_ref):
    ts = xg_ref.shape[0]
    C = LRU_WIDTH

    @pl.when(pl.program_id(1) == 0)
    def _():
        ext_ref[0:8, :] = jnp.zeros((8, C), F32)
        hc_ref[...] = jnp.zeros_like(hc_ref)

    xr = xg_ref[:, 0:C].astype(F32)
    ext_ref[8:8 + ts, :] = xr
    xc = cb_ref[...] + xr * cw_ref[CONV_W - 1:CONV_W, :]
    for back in range(1, CONV_W):
        tap = CONV_W - 1 - back
        xc = xc + ext_ref[8 - back:8 - back + ts, :] * cw_ref[tap:tap + 1, :]
    ext_ref[0:8, :] = xr[ts - 8:ts, :]

    gates = _dot(xc.astype(BF16), wg_ref[...]) + bg_ref[...]
    r = _sigmoid(gates[:, 0:C])
    gi = _sigmoid(gates[:, C:2 * C])
    log_a = (-LRU_C) * r * sp_ref[...]
    a = jnp.exp(log_a)
    b = xc * gi * jnp.sqrt(1.0 - a * a)

    rows = lax.broadcasted_iota(jnp.int32, (ts, C), 0)
    d = 1
    while d < ts:
        keep = rows >= d
        a_sh = pltpu.roll(a, d, axis=0)
        b_sh = pltpu.roll(b, d, axis=0)
        b = jnp.where(keep, a * b_sh + b, b)
        a = jnp.where(keep, a * a_sh, a)
        d *= 2
    h = b + a * hc_ref[...]
    hc_ref[...] = h[ts - 1:ts, :]

    gr = xg_ref[:, C:2 * C].astype(F32)
    gelu = 0.5 * gr * (1.0 + jnp.tanh(math.sqrt(2.0 / math.pi) * (gr + 0.044715 * gr * gr * gr)))
    o_ref[...] = (h * gelu).astype(BF16)


def _rglru(xg, conv_w, conv_b, wg, bg, softplus_neg_lam, batch, seq):
    T = xg.shape[0]
    C = LRU_WIDTH
    ts = min(LRU_TS, seq)
    per_b = seq // ts
    const = lambda b, i: (0, 0)
    return pl.pallas_call(
        _rglru_kernel,
        out_shape=jax.ShapeDtypeStruct((T, C), BF16),
        grid=(batch, per_b),
        in_specs=[
            pl.BlockSpec((ts, 2 * C), lambda b, i: (b * per_b + i, 0)),
            pl.BlockSpec((CONV_W, C), const),
            pl.BlockSpec((1, C), const),
            pl.BlockSpec((C, 2 * C), const),
            pl.BlockSpec((1, 2 * C), const),
            pl.BlockSpec((1, C), const),
        ],
        out_specs=pl.BlockSpec((ts, C), lambda b, i: (b * per_b + i, 0)),
        scratch_shapes=[pltpu.VMEM((ts + 8, C), F32), pltpu.VMEM((1, C), F32)],
        compiler_params=_params("arbitrary", "arbitrary"),
        name="rglru",
    )(xg, conv_w, conv_b, wg, bg, softplus_neg_lam)


def _diff_attn_kernel(q1_ref, q2_ref, k_ref, vt_ref, bprev_ref, bdiag_ref, lam_ref, sub_ref, o_ref,
                      m_ref, l_ref, acc_ref, *, tb, lambda_init):
    i = pl.program_id(2)
    q = (q1_ref[0], q2_ref[0])

    m_ref[...] = jnp.full(m_ref.shape, NEG_INF, F32)
    l_ref[...] = jnp.zeros(l_ref.shape, F32)
    acc_ref[...] = jnp.zeros(acc_ref.shape, F32)

    def attend(start, tk, bias):
        kb = k_ref[0, pl.ds(start, tk), :]
        vb = vt_ref[0, :, pl.ds(start, tk)]
        scores = [_dot_nt(kb, q[mp]) for mp in range(2)]
        for mp in range(2):
            s = scores[mp]
            if bias is not None:
                s = s + bias
            m_old = m_ref[mp]
            m_new = jnp.maximum(m_old, jnp.max(s, axis=0, keepdims=True))
            alpha = jnp.exp2(m_old - m_new)
            p = jnp.exp2(s - m_new)
            l_ref[mp] = alpha * l_ref[mp] + jnp.sum(p, axis=0, keepdims=True)
            acc_ref[mp] = alpha * acc_ref[mp] + _dot(vb, p.astype(BF16))
            m_ref[mp] = m_new

    n_far = jnp.maximum(i - 1, 0)

    def far(j, carry):
        attend(pl.multiple_of(j * (2 * tb), 2 * tb), 2 * tb, None)
        return carry

    lax.fori_loop(0, n_far // 2, far, 0)

    @pl.when(n_far % 2 == 1)
    def _():
        attend(pl.multiple_of((n_far - 1) * tb, tb), tb, None)

    @pl.when(i > 0)
    def _():
        attend(pl.multiple_of((i - 1) * tb, tb), tb, bprev_ref[0])

    attend(pl.multiple_of(i * tb, tb), tb, bdiag_ref[0])

    lv = lam_ref[...]
    lam = (jnp.exp(jnp.sum(lv[0:1] * lv[1:2], axis=-1, keepdims=True))
           - jnp.exp(jnp.sum(lv[2:3] * lv[3:4], axis=-1, keepdims=True)) + lambda_init)
    o = acc_ref[0] / l_ref[0] - lam * (acc_ref[1] / l_ref[1])
    ms = jnp.mean(o * o, axis=0, keepdims=True)
    o = o * lax.rsqrt(ms + NORM_EPS) * sub_ref[...] * (1.0 - lambda_init)
    o_ref[0] = o.T.astype(BF16)


def _diff_attn(q1, q2, k, vt, bprev, bdiag, lam_vecs, subln, lambda_init, batch, seq):
    W = DIFF_HEADS * 2 * DIFF_HD
    tb = min(ATT_TB, seq)
    shp = (batch, seq, W)
    qspec = pl.BlockSpec((1, tb, LANES), lambda b, h, i: (b, i, h))
    kspec = pl.BlockSpec((1, seq, LANES), lambda b, h, i: (b, 0, h))
    vspec = pl.BlockSpec((1, LANES, seq), lambda b, h, i: (b, h, 0))
    bspec = pl.BlockSpec((1, tb, tb), lambda b, h, i: (h, 0, 0))
    const = lambda b, h, i: (0, 0)
    out = pl.pallas_call(
        functools.partial(_diff_attn_kernel, tb=tb, lambda_init=lambda_init),
        out_shape=jax.ShapeDtypeStruct(shp, BF16),
        grid=(batch, DIFF_HEADS, seq // tb),
        in_specs=[qspec, qspec, kspec, vspec, bspec, bspec,
                  pl.BlockSpec((4, DIFF_HD), const), pl.BlockSpec((2 * DIFF_HD, 1), const)],
        out_specs=qspec,
        scratch_shapes=[pltpu.VMEM((2, 1, tb), F32), pltpu.VMEM((2, 1, tb), F32),
                        pltpu.VMEM((2, LANES, tb), F32)],
        compiler_params=_params("arbitrary", "arbitrary", "arbitrary"),
        name="diff_attn",
    )(q1.reshape(shp), q2.reshape(shp), k.reshape(shp), vt, bprev, bdiag, lam_vecs, subln)
    return out.reshape(batch * seq, W)


def _swa_kernel(q_ref, k_ref, v_ref, bp_ref, bc_ref, sink_ref, o_ref, *, ts):
    i = pl.program_id(1)
    blk = WINDOW
    for sub in range(ts // blk):
        start = i * ts + sub * blk
        has_prev = start > 0
        pstart = pl.multiple_of(jnp.maximum(start - blk, 0), blk)
        cstart = pl.multiple_of(start, blk)
        kp = k_ref[0, pl.ds(pstart, blk), :]
        kc = k_ref[0, pl.ds(cstart, blk), :]
        vp = v_ref[0, pl.ds(pstart, blk), :]
        vc = v_ref[0, pl.ds(cstart, blk), :]
        qs = q_ref[0, sub * blk:(sub + 1) * blk, :]
        outs = []
        for hk in range(SWA_KV_HEADS):
            c0 = hk * SWA_GROUP * SWA_HD
            qh = jnp.concatenate(
                [qs[:, c0 + g * SWA_HD:c0 + (g + 1) * SWA_HD] for g in range(SWA_GROUP)], axis=0)
            ksl = slice(hk * SWA_HD, (hk + 1) * SWA_HD)
            s_p = _dot_nt(qh, kp[:, ksl]) + bp_ref[hk]
            s_p = jnp.where(has_prev, s_p, NEG_INF)
            s_c = _dot_nt(qh, kc[:, ksl]) + bc_ref[hk]
            sink = sink_ref[hk]
            m = jnp.maximum(jnp.maximum(jnp.max(s_p, axis=-1, keepdims=True),
                                        jnp.max(s_c, axis=-1, keepdims=True)), sink)
            p_p = jnp.exp(s_p - m)
            p_c = jnp.exp(s_c - m)
            den = (jnp.sum(p_p, axis=-1, keepdims=True) + jnp.sum(p_c, axis=-1, keepdims=True)
                   + jnp.exp(sink - m))
            o = _dot(p_p.astype(BF16), vp[:, ksl]) + _dot(p_c.astype(BF16), vc[:, ksl])
            o = o / den
            outs.extend(o[g * blk:(g + 1) * blk, :] for g in range(SWA_GROUP))
        o_ref[0, sub * blk:(sub + 1) * blk, :] = jnp.concatenate(outs, axis=1).astype(BF16)


def _swa(q, k, v, bias_prev, bias_cur, sinks, batch, seq):
    ts = min(SWA_TS, seq)
    WQ = SWA_HEADS * SWA_HD
    WK = SWA_KV_HEADS * SWA_HD
    rows = SWA_GROUP * WINDOW
    const3 = lambda b, i: (0, 0, 0)
    out = pl.pallas_call(
        functools.partial(_swa_kernel, ts=ts),
        out_shape=jax.ShapeDtypeStruct((batch, seq, WQ), BF16),
        grid=(batch, seq // ts),
        in_specs=[
            pl.BlockSpec((1, ts, WQ), lambda b, i: (b, i, 0)),
            pl.BlockSpec((1, seq, WK), lambda b, i: (b, 0, 0)),
            pl.BlockSpec((1, seq, WK), lambda b, i: (b, 0, 0)),
            pl.BlockSpec((SWA_KV_HEADS, rows, WINDOW), const3),
            pl.BlockSpec((SWA_KV_HEADS, rows, WINDOW), const3),
            pl.BlockSpec((SWA_KV_HEADS, rows, 1), const3),
        ],
        out_specs=pl.BlockSpec((1, ts, WQ), lambda b, i: (b, i, 0)),
        compiler_params=_params("arbitrary", "arbitrary"),
        name="swa_attn",
    )(q.reshape(batch, seq, WQ), k.reshape(batch, seq, WK), v.reshape(batch, seq, WK),
      bias_prev, bias_cur, sinks)
    return out.reshape(batch * seq, WQ)


def _merge_kernel(x_ref, mod_ref, lru_ref, diff_ref, swa_ref, gl_ref, wb_ref, wo_ref, o_ref):
    D = x_ref.shape[1]
    merged = None
    for n, br in enumerate((lru_ref, diff_ref, swa_ref)):
        gate = _sigmoid(gl_ref[:, n * D:(n + 1) * D].astype(F32))
        term = gate * _dot(br[...], wb_ref[n])
        merged = term if merged is None else merged + term
    out = _dot(merged.astype(BF16), wo_ref[...])
    o_ref[...] = x_ref[...] + mod_ref[0, 2:3, :] * out


def _merge(x2, mod, o_lru, o_diff, o_swa, gl, w_branch, w_out, seq):
    T, D = x2.shape
    tm = min(MERGE_TM, seq)
    per_b = seq // tm
    row = lambda i: (i, 0)
    return pl.pallas_call(
        _merge_kernel,
        out_shape=jax.ShapeDtypeStruct((T, D), F32),
        grid=(T // tm,),
        in_specs=[
            pl.BlockSpec((tm, D), row),
            pl.BlockSpec((1, 6, D), lambda i: (i // per_b, 0, 0)),
            pl.BlockSpec((tm, BRANCH_WIDTH), row),
            pl.BlockSpec((tm, BRANCH_WIDTH), row),
            pl.BlockSpec((tm, BRANCH_WIDTH), row),
            pl.BlockSpec((tm, N_BRANCHES * D), row),
            pl.BlockSpec(w_branch.shape, lambda i: (0, 0, 0)),
            pl.BlockSpec(w_out.shape, lambda i: (0, 0)),
        ],
        out_specs=pl.BlockSpec((tm, D), row),
        compiler_params=_params("arbitrary"),
        name="merge_outproj",
    )(x2, mod, o_lru, o_diff, o_swa, gl, w_branch, w_out)


def _router_kernel(x_ref, mod_ref, gain_ref, wr_ref, br_ref, tri_ref,
                   hp_ref, e_ref, pt_ref, rank_ref, cnt_ref, run_ref):
    @pl.when(pl.program_id(0) == 0)
    def _():
        run_ref[...] = jnp.zeros_like(run_ref)

    x = x_ref[...]
    tm, D = x.shape
    h = _rms_mod(x, gain_ref[...], mod_ref[0, 4:5, :], mod_ref[0, 3:4, :])
    for c in range(D // LANES):
        hp_ref[pl.ds(c, tm, stride=SUBLANES), :] = h[:, LANES * c:LANES * (c + 1)]

    logits = _dot_nt(wr_ref[...], h, precision=HIGHEST) + br_ref[...]
    eidx = lax.broadcasted_iota(jnp.int32, logits.shape, 0)
    vals, idxs, hots = [], [], []
    for _ in range(TOP_K):
        mx = jnp.max(logits, axis=0, keepdims=True)
        sel = jnp.min(jnp.where(logits == mx, eidx, N_EXPERTS), axis=0, keepdims=True)
        hot = eidx == sel
        vals.append(mx)
        idxs.append(sel)
        hots.append(hot)
        logits = jnp.where(hot, -jnp.inf, logits)
    e_ref[...] = jnp.concatenate(idxs, axis=0)
    top_v = jnp.concatenate(vals, axis=0)
    ex = jnp.exp(top_v - top_v[0:1])
    p = ex / jnp.sum(ex, axis=0, keepdims=True)
    pt_ref[...] = jnp.concatenate([p, jnp.zeros((LANES - TOP_K, tm), F32)], axis=0).T

    member = hots[0]
    for hot in hots[1:]:
        member = member | hot
    member = jnp.where(member, 1.0, 0.0)
    before = _dot(member.astype(BF16), tri_ref[...]) + run_ref[:, 0:1]
    ranks = [jnp.sum(jnp.where(hot, before, 0.0), axis=0, keepdims=True) for hot in hots]
    rank_ref[...] = jnp.concatenate(ranks, axis=0).astype(jnp.int32)
    run_ref[...] = run_ref[...] + jnp.sum(member, axis=1, keepdims=True)
    cnt_ref[...] = run_ref[...]


def _router(x2, mod, gain, w_router_t, b_router, seq):
    T, D = x2.shape
    tm = min(ROUTER_TM, seq)
    per_b = seq // tm
    tri = (jnp.arange(tm)[:, None] < jnp.arange(tm)[None, :]).astype(BF16)
    const = lambda i: (0, 0)
    return pl.pallas_call(
        _router_kernel,
        out_shape=[jax.ShapeDtypeStruct((T * SUBLANES, LANES), F32),
                   jax.ShapeDtypeStruct((TOP_K, T), jnp.int32),
                   jax.ShapeDtypeStruct((T, LANES), F32),
                   jax.ShapeDtypeStruct((TOP_K, T), jnp.int32),
                   jax.ShapeDtypeStruct((N_EXPERTS, LANES), F32)],
        grid=(T // tm,),
        in_specs=[
            pl.BlockSpec((tm, D), lambda i: (i, 0)),
            pl.BlockSpec((1, 6, D), lambda i: (i // per_b, 0, 0)),
            pl.BlockSpec((1, D), const),
            pl.BlockSpec((N_EXPERTS, D), const),
            pl.BlockSpec((N_EXPERTS, 1), const),
            pl.BlockSpec((tm, tm), const),
        ],
        out_specs=[pl.BlockSpec((tm * SUBLANES, LANES), lambda i: (i, 0)),
                   pl.BlockSpec((TOP_K, tm), lambda i: (0, i)),
                   pl.BlockSpec((tm, LANES), lambda i: (i, 0)),
                   pl.BlockSpec((TOP_K, tm), lambda i: (0, i)),
                   pl.BlockSpec((N_EXPERTS, LANES), const)],
        scratch_shapes=[pltpu.VMEM((N_EXPERTS, LANES), F32)],
        compiler_params=_params("arbitrary"),
        name="router",
    )(x2, mod, gain, w_router_t, b_router, tri)


def _tile_rows(idx, tm):
    K, T = idx.shape
    nb = T // tm
    return idx.reshape(K, nb, tm).transpose(1, 0, 2).reshape(nb, 1, K * tm)


def _dispatch_kernel(nused_ref, tok_ref, hp_hbm, xs_ref, sem, *, tm, per_moe):
    i = pl.program_id(0)

    @pl.when(i * per_moe < nused_ref[0])
    def _():
        def issue(r, carry):
            src = pl.multiple_of(tok_ref[0, 0, r] * SUBLANES, SUBLANES)
            dst = pl.multiple_of(r * SUBLANES, SUBLANES)
            pltpu.make_async_copy(hp_hbm.at[pl.ds(src, SUBLANES), :], xs_ref.at[pl.ds(dst, SUBLANES), :],
                                  sem).start()
            return carry

        lax.fori_loop(0, tm, issue, 0, unroll=8)
        pltpu.make_async_copy(hp_hbm.at[pl.ds(0, tm * SUBLANES), :], xs_ref, sem).wait()

    @pl.when(i * per_moe >= nused_ref[0])
    def _():
        xs_ref[...] = jnp.zeros(xs_ref.shape, xs_ref.dtype)


def _dispatch(n_used, row_token, hp):
    n_rows = row_token.shape[0]
    tm = DISP_TM
    nb = n_rows // tm
    return pl.pallas_call(
        functools.partial(_dispatch_kernel, tm=tm, per_moe=tm // MOE_TM),
        out_shape=jax.ShapeDtypeStruct((n_rows * SUBLANES, LANES), hp.dtype),
        grid_spec=pltpu.PrefetchScalarGridSpec(
            num_scalar_prefetch=1,
            grid=(nb,),
            in_specs=[
                pl.BlockSpec((1, 1, tm), lambda i, nu: (i, 0, 0), memory_space=pltpu.SMEM),
                pl.BlockSpec(memory_space=pl.ANY),
            ],
            out_specs=pl.BlockSpec((tm * SUBLANES, LANES), lambda i, nu: (i, 0)),
            scratch_shapes=[pltpu.SemaphoreType.DMA],
        ),
        compiler_params=_params("arbitrary"),
        name="moe_dispatch",
    )(n_used, row_token.reshape(nb, 1, tm), hp)


def _expert_kernel(be_ref, nused_ref, xs_ref, w1_ref, b1_ref, w2_ref, b2_ref, y_ref,
                   w1b_ref, w2s_ref, w2b_ref, act_ref):
    i = pl.program_id(0)
    used = i < nused_ref[0]
    chunks = w1_ref.shape[3] // (2 * LANES)
    hl = LANES // 2

    @pl.when(used & ((i == 0) | (be_ref[i] != be_ref[jnp.maximum(i - 1, 0)])))
    def _():
        for q in range(chunks):
            cs = slice(2 * LANES * q, 2 * LANES * (q + 1))
            w1b_ref[:, cs] = w1_ref[0, 0, :, cs].astype(BF16)
        cols = w2_ref.shape[3] // LANES
        for q in range(w2_ref.shape[2] // LANES):
            for s in range(2):
                rows = w2_ref[0, 0, LANES * q + hl * s:LANES * q + hl * (s + 1), :]
                for c in range(cols):
                    w2s_ref[c, pl.ds(LANES * q + s, hl, stride=2), :] = rows[:, LANES * c:LANES * (c + 1)]
        for c in range(cols):
            w2b_ref[:, LANES * c:LANES * (c + 1)] = w2s_ref[c].astype(BF16)

    @pl.when(used)
    def _():
        tm = xs_ref.shape[0] // SUBLANES
        x = jnp.concatenate([xs_ref[pl.ds(c, tm, stride=SUBLANES), :] for c in range(SUBLANES)],
                            axis=1).astype(BF16)
        even = lax.broadcasted_iota(jnp.int32, (tm, LANES), 1) % 2 == 0
        for q in range(chunks):
            cs = slice(2 * LANES * q, 2 * LANES * (q + 1))
            hq = _dot(x, w1b_ref[:, cs]) + b1_ref[0, 0, :, cs]
            lo, hi = hq[:, :LANES], hq[:, LANES:]
            glu = jnp.where(even, lo, pltpu.roll(hi, 1, axis=1))
            lin = jnp.where(even, pltpu.roll(lo, LANES - 1, axis=1), hi)
            glu = jnp.minimum(glu, SWIGLU_LIMIT)
            lin = jnp.clip(lin, -SWIGLU_LIMIT, SWIGLU_LIMIT)
            act = glu * _sigmoid(SWIGLU_ALPHA * glu) * (lin + 1.0)
            act_ref[:, LANES * q:LANES * (q + 1)] = act.astype(BF16)
        y = _dot(act_ref[...], w2b_ref[...]) + b2_ref[0, 0]
        for c in range(SUBLANES):
            y_ref[pl.ds(c, tm, stride=SUBLANES), :] = y[:, LANES * c:LANES * (c + 1)]

    @pl.when(jnp.logical_not(used))
    def _():
        y_ref[...] = jnp.zeros(y_ref.shape, y_ref.dtype)


def _experts(layer, block_expert, n_used, xs, w1, b1, w2, b2):
    _, E, D, F2 = w1.shape
    assert D == SUBLANES * LANES
    F = F2 // 2
    tm = MOE_TM
    nb = xs.shape[0] // (tm * SUBLANES)
    wmap = lambda i, be, nu: (layer, be[i], 0, 0)
    rmap = lambda i, be, nu: (jnp.maximum(jnp.minimum(i, nu[0] - 1), 0), 0)
    return pl.pallas_call(
        _expert_kernel,
        out_shape=jax.ShapeDtypeStruct(xs.shape, F32),
        grid_spec=pltpu.PrefetchScalarGridSpec(
            num_scalar_prefetch=2,
            grid=(nb,),
            in_specs=[
                pl.BlockSpec((tm * SUBLANES, LANES), rmap),
                pl.BlockSpec((1, 1, D, F2), wmap),
                pl.BlockSpec((1, 1, 1, F2), wmap),
                pl.BlockSpec((1, 1, F, D), wmap),
                pl.BlockSpec((1, 1, 1, D), wmap),
            ],
            out_specs=pl.BlockSpec((tm * SUBLANES, LANES), lambda i, be, nu: (i, 0)),
            scratch_shapes=[pltpu.VMEM((D, F2), BF16), pltpu.VMEM((D // LANES, F, LANES), F32),
                            pltpu.VMEM((F, D), BF16),
                            pltpu.VMEM((tm, F), BF16)],
        ),
        compiler_params=_params("arbitrary"),
        name="moe_experts",
    )(block_expert, n_used, xs, w1, b1, w2, b2)


def _combine_kernel(dest_ref, dnext_ref, y_hbm, pt_ref, x_ref, mod_ref, o_ref, ybuf, sem, *, tm):
    n = TOP_K * tm
    i = pl.program_id(0)
    slot = i % 2

    def gather(idx_ref, s):
        def issue(r, carry):
            src = pl.multiple_of(idx_ref[0, 0, r] * SUBLANES, SUBLANES)
            dst = pl.multiple_of(r * SUBLANES, SUBLANES)
            pltpu.make_async_copy(y_hbm.at[pl.ds(src, SUBLANES), :], ybuf.at[s, pl.ds(dst, SUBLANES), :],
                                  sem.at[s]).start()
            return carry

        lax.fori_loop(0, n, issue, 0, unroll=8)

    @pl.when(i == 0)
    def _():
        gather(dest_ref, 0)

    @pl.when(i + 1 < pl.num_programs(0))
    def _():
        gather(dnext_ref, 1 - slot)

    pltpu.make_async_copy(y_hbm.at[pl.ds(0, n * SUBLANES), :], ybuf.at[slot], sem.at[slot]).wait()

    rows = ybuf.at[slot]
    for c in range(SUBLANES):
        cs = slice(LANES * c, LANES * (c + 1))
        acc = rows[pl.ds(c, tm, stride=SUBLANES), :] * pt_ref[:, 0:1]
        for k in range(1, TOP_K):
            acc = acc + rows[pl.ds(k * tm * SUBLANES + c, tm, stride=SUBLANES), :] * pt_ref[:, k:k + 1]
        o_ref[:, cs] = x_ref[:, cs] + mod_ref[0, 5:6, cs] * acc


def _combine(dest, y, pt, x2, mod, seq):
    T, D = x2.shape
    tm = min(COMB_TM, seq)
    per_b = seq // tm
    nb = T // tm
    rows = _tile_rows(dest, tm)
    return pl.pallas_call(
        functools.partial(_combine_kernel, tm=tm),
        out_shape=jax.ShapeDtypeStruct((T, D), F32),
        grid=(nb,),
        in_specs=[
            pl.BlockSpec((1, 1, TOP_K * tm), lambda i: (i, 0, 0), memory_space=pltpu.SMEM),
            pl.BlockSpec((1, 1, TOP_K * tm), lambda i: (jnp.minimum(i + 1, nb - 1), 0, 0),
                         memory_space=pltpu.SMEM),
            pl.BlockSpec(memory_space=pl.ANY),
            pl.BlockSpec((tm, LANES), lambda i: (i, 0)),
            pl.BlockSpec((tm, D), lambda i: (i, 0)),
            pl.BlockSpec((1, 6, D), lambda i: (i // per_b, 0, 0)),
        ],
        out_specs=pl.BlockSpec((tm, D), lambda i: (i, 0)),
        scratch_shapes=[pltpu.VMEM((2, TOP_K * tm * SUBLANES, LANES), F32), pltpu.SemaphoreType.DMA((2,))],
        compiler_params=_params("arbitrary"),
        name="moe_combine",
    )(rows, rows, y, pt, x2, mod)


def _t5_bucket(dist):
    exact = REL_BUCKETS // 2
    log_ratio = jnp.log(jnp.maximum(dist, 1).astype(F32) / exact) / math.log(REL_MAX_DIST / exact)
    large = exact + (log_ratio * (REL_BUCKETS - exact)).astype(jnp.int32)
    return jnp.where(dist < exact, dist, jnp.minimum(large, REL_BUCKETS - 1))


def _diff_bias_tiles(rel_bias, tb):
    table = rel_bias[:, :DIFF_HEADS].astype(F32)
    shifted = (table - table[REL_BUCKETS - 1]) * LOG2E
    kk = jnp.arange(tb)[:, None]
    qq = jnp.arange(tb)[None, :]

    def tile(dist):
        return _bucket_lookup(shifted, _t5_bucket(jnp.maximum(dist, 0)))

    diag = jnp.where((qq - kk >= 0)[None], tile(qq - kk), NEG_INF)
    prev = tile(qq + tb - kk)
    return prev, diag


def _bucket_lookup(table, bucket):
    out = jnp.zeros((table.shape[1],) + bucket.shape, F32)
    for b in range(table.shape[0]):
        out = out + jnp.where(bucket[None] == b, table[b].reshape((-1,) + (1,) * bucket.ndim), 0.0)
    return out


def _swa_bias_tiles(rel_bias):
    qi = jnp.arange(WINDOW)[:, None]
    kj = jnp.arange(2 * WINDOW)[None, :]
    dist = WINDOW + qi - kj
    ok = (dist >= 0) & (dist < WINDOW)
    bias = _bucket_lookup(rel_bias[:, DIFF_HEADS:].astype(F32),
                          _t5_bucket(jnp.clip(dist, 0, WINDOW - 1)))
    bias = jnp.where(ok[None], bias, NEG_INF)
    bias = bias.reshape(SWA_KV_HEADS, SWA_GROUP * WINDOW, 2 * WINDOW)
    return bias[..., :WINDOW], bias[..., WINDOW:]


def _block_diag(w):
    nb, n, _ = w.shape
    eye = jnp.eye(nb, dtype=w.dtype)
    return (eye[:, None, :, None] * w[:, :, None, :]).reshape(nb * n, nb * n)


def _routing(counts, top_e, rank, tm):
    K, T = top_e.shape
    counts = counts.astype(jnp.int32)
    padded = (counts + tm - 1) // tm * tm
    pad_ends = jnp.cumsum(padded)
    pad_starts = pad_ends - padded
    onehot = top_e[..., None] == jnp.arange(N_EXPERTS, dtype=jnp.int32)
    dest = rank + jnp.sum(jnp.where(onehot, pad_starts, 0), axis=-1)
    n_rows = K * T + N_EXPERTS * tm
    nb = n_rows // tm
    n_used = pad_ends[-1] // tm
    blk = jnp.minimum(jnp.arange(nb), n_used - 1) * tm
    block_expert = jnp.sum(blk[:, None] >= pad_ends[None, :], axis=-1)
    fill_ends = jnp.cumsum(padded - counts)
    fill_expert = jnp.sum(jnp.arange(N_EXPERTS * tm)[:, None] >= fill_ends[None, :], axis=-1)
    keys = jnp.concatenate([(top_e * T + jnp.arange(T, dtype=jnp.int32)[None, :]).reshape(-1),
                            (fill_expert * T + (T - 1)).astype(jnp.int32)])
    row_token = jnp.sort(keys) % T
    return (block_expert.astype(jnp.int32), n_used.astype(jnp.int32).reshape(1), dest.astype(jnp.int32),
            row_token.astype(jnp.int32))


def kernel(x, c, w_ada, b_ada, norm_mix, norm_ffn, w_in, conv_w, conv_b, lru_wa, lru_ba, lru_wx, lru_bx,
           lru_lambda, diff_qnorm, diff_knorm, diff_lambda, diff_subln, swa_qnorm, swa_knorm, swa_sinks,
           rel_bias, w_branch, w_out, w_router, b_router, w1, b1, w2, b2):
    B, S, D = x.shape
    L = w_ada.shape[0]
    T = B * S
    tb = min(ATT_TB, S)

    mods = _adaln(c, w_ada, b_ada)
    bprev, bdiag = _diff_bias_tiles(rel_bias, tb)
    sbias_prev, sbias_cur = _swa_bias_tiles(rel_bias)
    gidx = jnp.arange(512) // DIFF_HD
    gmat = jnp.where(gidx[:, None] == gidx[None, :], 1.0 / DIFF_HD, 0.0).astype(BF16)

    x2 = x.reshape(T, D)
    for l in range(L):
        lambda_init = 0.8 - 0.6 * math.exp(-0.3 * l)
        mod = mods[l]
        qkg = jnp.stack([
            jnp.tile(diff_qnorm[l], 8) * (DIFF_HD ** -0.5 * LOG2E),
            jnp.tile(diff_knorm[l], 8),
            jnp.tile(swa_qnorm[l], 8) * SWA_HD ** -0.5,
            jnp.tile(swa_knorm[l], 8),
        ]).astype(F32)
        w_in_l = w_in[l].astype(BF16)
        xg, q1, q2, dk, dvt, sq, sk, sv, gl = _inproj(
            x2, mod, norm_mix[l].reshape(1, D), w_in_l, w_in_l[:, 2048:2560].T, gmat, qkg, S)

        wg = jnp.concatenate([_block_diag(lru_wa[l]), _block_diag(lru_wx[l])], axis=1).astype(BF16)
        bg = jnp.concatenate([lru_ba[l], lru_bx[l]]).reshape(1, 2 * LRU_WIDTH)
        sp = jax.nn.softplus(-lru_lambda[l].astype(F32)).reshape(1, LRU_WIDTH)
        o_lru = _rglru(xg, conv_w[l], conv_b[l].reshape(1, LRU_WIDTH), wg, bg, sp, B, S)

        o_diff = _diff_attn(q1, q2, dk, dvt, bprev, bdiag, diff_lambda[l],
                            diff_subln[l].reshape(2 * DIFF_HD, 1), lambda_init, B, S)

        sinks = jnp.repeat(swa_sinks[l].astype(F32).reshape(SWA_KV_HEADS, SWA_GROUP), WINDOW, axis=1)
        o_swa = _swa(sq, sk, sv, sbias_prev, sbias_cur, sinks.reshape(SWA_KV_HEADS, -1, 1), B, S)

        x2 = _merge(x2, mod, o_lru, o_diff, o_swa, gl, w_branch[l].astype(BF16), w_out[l].astype(BF16), S)

        hp, top_e, pt, rank, counts = _router(x2, mod, norm_ffn[l].reshape(1, D), w_router[l].T,
                                              b_router[l].reshape(N_EXPERTS, 1), S)
        block_expert, n_used, dest, row_token = _routing(counts[:, 0], top_e, rank, MOE_TM)
        xs = _dispatch(n_used, row_token, hp)
        y = _experts(l, block_expert, n_used, xs, w1, b1[:, :, None, :], w2, b2[:, :, None, :])
        x2 = _combine(dest, y, pt, x2, mod, S)
    return x2.reshape(B, S, D)
```

```python
import functools
import math

import jax
import jax.numpy as jnp
from jax import lax
from jax.experimental import pallas as pl
from jax.experimental.pallas import tpu as pltpu

F32 = jnp.float32
BF16 = jnp.bfloat16

LRU_WIDTH = 512
LRU_BLOCKS = 8
LRU_C = 8.0
CONV_W = 4
DIFF_HEADS = 4
DIFF_HD = 64
SWA_HEADS = 8
SWA_KV_HEADS = 2
SWA_GROUP = SWA_HEADS // SWA_KV_HEADS
SWA_HD = 64
WINDOW = 128
N_BRANCHES = 3
BRANCH_WIDTH = 512
REL_BUCKETS = 32
REL_MAX_DIST = 128
N_EXPERTS = 32
TOP_K = 4
SWIGLU_LIMIT = 7.0
SWIGLU_ALPHA = 1.702
NORM_EPS = 1e-6
NEG_INF = -1e30
LOG2E = math.log2(math.e)

VMEM_LIMIT_BYTES = 56 * 1024 * 1024
LANES = 128
SUBLANES = 8

ADALN_TN = 1536
PROJ_TM = 512
LRU_TS = 512
ATT_TB = 512
SWA_TS = 512
MERGE_TM = 512
ROUTER_TM = 512
DISP_TM = 512
MOE_TM = 512
COMB_TM = 128

HIGHEST = lax.Precision.HIGHEST


def _params(*sem):
    return pltpu.CompilerParams(dimension_semantics=sem, vmem_limit_bytes=VMEM_LIMIT_BYTES)


def _dot(a, b, **kw):
    return jnp.dot(a, b, preferred_element_type=F32, **kw)


def _dot_nt(a, b, **kw):
    return lax.dot_general(a, b, (((1,), (1,)), ((), ())), preferred_element_type=F32, **kw)


def _sigmoid(x):
    return 1.0 / (1.0 + jnp.exp(-x))


def _adaln_kernel(c_ref, w_ref, b_ref, o_ref):
    c = c_ref[...]
    cond = c * _sigmoid(c)
    o_ref[0] = _dot(cond, w_ref[0], precision=HIGHEST) + b_ref[0]


def _adaln(c, w_ada, b_ada):
    L, D, N = w_ada.shape
    B = c.shape[0]
    rows = 8
    cp = jnp.zeros((rows, D), F32).at[:B].set(c)
    out = pl.pallas_call(
        _adaln_kernel,
        out_shape=jax.ShapeDtypeStruct((L, rows, N), F32),
        grid=(L, N // ADALN_TN),
        in_specs=[
            pl.BlockSpec((rows, D), lambda l, j: (0, 0)),
            pl.BlockSpec((1, D, ADALN_TN), lambda l, j: (l, 0, j)),
            pl.BlockSpec((1, 1, ADALN_TN), lambda l, j: (l, 0, j)),
        ],
        out_specs=pl.BlockSpec((1, rows, ADALN_TN), lambda l, j: (l, 0, j)),
        compiler_params=_params("arbitrary", "arbitrary"),
        name="adaln",
    )(cp, w_ada, b_ada.reshape(L, 1, N))
    return out[:, :B].reshape(L, B, 6, D)


def _rms_mod(x, gain, scale, shift):
    ms = jnp.mean(x * x, axis=-1, keepdims=True)
    return (x * lax.rsqrt(ms + NORM_EPS) * gain) * (1.0 + scale) + shift


def _group_rms(x, gmat):
    sq = x * x
    hi = sq.astype(BF16)
    lo = (sq - hi.astype(F32)).astype(BF16)
    ms = _dot(hi, gmat) + _dot(lo, gmat)
    return x * lax.rsqrt(ms + NORM_EPS)


def _inproj_kernel(x_ref, mod_ref, gain_ref, w_ref, wvt_ref, wsk_ref, gmat_ref, qkg_ref,
                   xg_ref, q1_ref, q2_ref, dk_ref, dvt_ref, sq_ref, sk_ref, svt_ref, gl_ref):
    x = x_ref[...]
    h = _rms_mod(x, gain_ref[...], mod_ref[0, 1:2, :], mod_ref[0, 0:1, :]).astype(BF16)
    gmat = gmat_ref[...]

    xg_ref[...] = _dot(h, w_ref[:, 0:1024]).astype(BF16)

    dd = _dot(h, w_ref[:, 1024:2048])
    qn = _group_rms(dd[:, 0:512], gmat) * qkg_ref[0:1, :]
    kn = _group_rms(dd[:, 512:1024], gmat) * qkg_ref[1:2, :]
    lane = lax.broadcasted_iota(jnp.int32, qn.shape, 1) % LANES
    q1_ref[...] = jnp.where(lane < DIFF_HD, qn, 0.0).astype(BF16)
    q2_ref[...] = jnp.where(lane >= DIFF_HD, qn, 0.0).astype(BF16)
    dk_ref[...] = kn.astype(BF16)
    vt = _dot_nt(wvt_ref[...], h)
    nd = dvt_ref.shape[1]
    dvt_ref[0] = vt[0:nd].astype(BF16)
    svt_ref[0] = vt[nd:].astype(BF16)

    sq_ref[...] = (_group_rms(_dot(h, w_ref[:, 2560:3072]), gmat) * qkg_ref[2:3, :]).astype(BF16)
    nk = sk_ref.shape[1]
    sk_ref[...] = (_group_rms(_dot(h, wsk_ref[...]), gmat[0:nk, 0:nk]) * qkg_ref[3:4, 0:nk]).astype(BF16)

    gl_ref[...] = _dot(h, w_ref[:, 3328:6400]).astype(BF16)


def _inproj(x2, mod, gain, w_in, wvt, wsk, gmat, qkg, seq):
    T, D = x2.shape
    tm = min(PROJ_TM, seq)
    per_b = seq // tm
    dv_w = DIFF_HEADS * 2 * DIFF_HD
    sv_w = SWA_KV_HEADS * SWA_HD
    assert wvt.shape[0] == dv_w + sv_w
    widths = (1024, 512, 512, 512, -dv_w, 512, wsk.shape[1], -sv_w, 3072)
    row = lambda i: (i, 0)
    const = lambda i: (0, 0)
    shapes = [jax.ShapeDtypeStruct((T // seq, -w, seq) if w < 0 else (T, w), BF16) for w in widths]
    specs = [pl.BlockSpec((1, -w, tm), lambda i: (i // per_b, 0, i % per_b)) if w < 0
             else pl.BlockSpec((tm, w), row) for w in widths]
    return pl.pallas_call(
        _inproj_kernel,
        out_shape=shapes,
        grid=(T // tm,),
        in_specs=[
            pl.BlockSpec((tm, D), row),
            pl.BlockSpec((1, 6, D), lambda i: (i // per_b, 0, 0)),
            pl.BlockSpec((1, D), const),
            pl.BlockSpec(w_in.shape, const, pipeline_mode=pl.Buffered(1)),
            pl.BlockSpec(wvt.shape, const),
            pl.BlockSpec(wsk.shape, const),
            pl.BlockSpec(gmat.shape, const),
            pl.BlockSpec(qkg.shape, const),
        ],
        out_specs=specs,
        compiler_params=_params("arbitrary"),
        name="inproj",
    )(x2, mod, gain, w_in, wvt, wsk, gmat, qkg)


def _rglru_kernel(xg_ref, cw_ref, cb_ref, wg_ref, bg_ref, sp_ref, o_ref, ext_ref, hc_ref):
    ts = xg_ref.shape[0]
    C = LRU_WIDTH

    @pl.when(pl.program_id(1) == 0)
    def _():
        ext_ref[0:8, :] = jnp.zeros((8, C), F32)
        hc_ref[...] = jnp.zeros_like(hc_ref)

    xr = xg_ref[:, 0:C].astype(F32)
    ext_ref[8:8 + ts, :] = xr
    xc = cb_ref[...] + xr * cw_ref[CONV_W - 1:CONV_W, :]
    for back in range(1, CONV_W):
        tap = CONV_W - 1 - back
        xc = xc + ext_ref[8 - back:8 - back + ts, :] * cw_ref[tap:tap + 1, :]
    ext_ref[0:8, :] = xr[ts - 8:ts, :]

    gates = _dot(xc.astype(BF16), wg_ref[...]) + bg_ref[...]
    r = _sigmoid(gates[:, 0:C])
    gi = _sigmoid(gates[:, C:2 * C])
    log_a = (-LRU_C) * r * sp_ref[...]
    a = jnp.exp(log_a)
    b = xc * gi * jnp.sqrt(1.0 - a * a)

    rows = lax.broadcasted_iota(jnp.int32, (ts, C), 0)
    d = 1
    while d < ts:
        keep = rows >= d
        a_sh = pltpu.roll(a, d, axis=0)
        b_sh = pltpu.roll(b, d, axis=0)
        b = jnp.where(keep, a * b_sh + b, b)
        a = jnp.where(keep, a * a_sh, a)
        d *= 2
    h = b + a * hc_ref[...]
    hc_ref[...] = h[ts - 1:ts, :]

    gr = xg_ref[:, C:2 * C].astype(F32)
    gelu = 0.5 * gr * (1.0 + jnp.tanh(math.sqrt(2.0 / math.pi) * (gr + 0.044715 * gr * gr * gr)))
    o_ref[...] = (h * gelu).astype(BF16)


def _rglru(xg, conv_w, conv_b, wg, bg, softplus_neg_lam, batch, seq):
    T = xg.shape[0]
    C = LRU_WIDTH
    ts = min(LRU_TS, seq)
    per_b = seq // ts
    const = lambda b, i: (0, 0)
    return pl.pallas_call(
        _rglru_kernel,
        out_shape=jax.ShapeDtypeStruct((T, C), BF16),
        grid=(batch, per_b),
        in_specs=[
            pl.BlockSpec((ts, 2 * C), lambda b, i: (b * per_b + i, 0)),
            pl.BlockSpec((CONV_W, C), const),
            pl.BlockSpec((1, C), const),
            pl.BlockSpec((C, 2 * C), const),
            pl.BlockSpec((1, 2 * C), const),
            pl.BlockSpec((1, C), const),
        ],
        out_specs=pl.BlockSpec((ts, C), lambda b, i: (b * per_b + i, 0)),
        scratch_shapes=[pltpu.VMEM((ts + 8, C), F32), pltpu.VMEM((1, C), F32)],
        compiler_params=_params("arbitrary", "arbitrary"),
        name="rglru",
    )(xg, conv_w, conv_b, wg, bg, softplus_neg_lam)


def _diff_attn_kernel(q1_ref, q2_ref, k_ref, vt_ref, bprev_ref, bdiag_ref, lam_ref, sub_ref, o_ref,
                      m_ref, l_ref, acc_ref, *, tb, lambda_init):
    i = pl.program_id(2)
    q = (q1_ref[0], q2_ref[0])

    m_ref[...] = jnp.full(m_ref.shape, NEG_INF, F32)
    l_ref[...] = jnp.zeros(l_ref.shape, F32)
    acc_ref[...] = jnp.zeros(acc_ref.shape, F32)

    def attend(start, tk, bias):
        kb = k_ref[0, pl.ds(start, tk), :]
        vb = vt_ref[0, :, pl.ds(start, tk)]
        scores = [_dot_nt(kb, q[mp]) for mp in range(2)]
        for mp in range(2):
            s = scores[mp]
            if bias is not None:
                s = s + bias
            m_old = m_ref[mp]
            m_new = jnp.maximum(m_old, jnp.max(s, axis=0, keepdims=True))
            alpha = jnp.exp2(m_old - m_new)
            p = jnp.exp2(s - m_new)
            l_ref[mp] = alpha * l_ref[mp] + jnp.sum(p, axis=0, keepdims=True)
            acc_ref[mp] = alpha * acc_ref[mp] + _dot(vb, p.astype(BF16))
            m_ref[mp] = m_new

    n_far = jnp.maximum(i - 1, 0)

    def far(j, carry):
        attend(pl.multiple_of(j * (2 * tb), 2 * tb), 2 * tb, None)
        return carry

    lax.fori_loop(0, n_far // 2, far, 0)

    @pl.when(n_far % 2 == 1)
    def _():
        attend(pl.multiple_of((n_far - 1) * tb, tb), tb, None)

    @pl.when(i > 0)
    def _():
        attend(pl.multiple_of((i - 1) * tb, tb), tb, bprev_ref[0])

    attend(pl.multiple_of(i * tb, tb), tb, bdiag_ref[0])

    lv = lam_ref[...]
    lam = (jnp.exp(jnp.sum(lv[0:1] * lv[1:2], axis=-1, keepdims=True))
           - jnp.exp(jnp.sum(lv[2:3] * lv[3:4], axis=-1, keepdims=True)) + lambda_init)
    o = acc_ref[0] / l_ref[0] - lam * (acc_ref[1] / l_ref[1])
    ms = jnp.mean(o * o, axis=0, keepdims=True)
    o = o * lax.rsqrt(ms + NORM_EPS) * sub_ref[...] * (1.0 - lambda_init)
    o_ref[0] = o.T.astype(BF16)


def _diff_attn(q1, q2, k, vt, bprev, bdiag, lam_vecs, subln, lambda_init, batch, seq):
    W = DIFF_HEADS * 2 * DIFF_HD
    tb = min(ATT_TB, seq)
    shp = (batch, seq, W)
    qspec = pl.BlockSpec((1, tb, LANES), lambda b, h, i: (b, i, h))
    kspec = pl.BlockSpec((1, seq, LANES), lambda b, h, i: (b, 0, h))
    vspec = pl.BlockSpec((1, LANES, seq), lambda b, h, i: (b, h, 0))
    bspec = pl.BlockSpec((1, tb, tb), lambda b, h, i: (h, 0, 0))
    const = lambda b, h, i: (0, 0)
    out = pl.pallas_call(
        functools.partial(_diff_attn_kernel, tb=tb, lambda_init=lambda_init),
        out_shape=jax.ShapeDtypeStruct(shp, BF16),
        grid=(batch, DIFF_HEADS, seq // tb),
        in_specs=[qspec, qspec, kspec, vspec, bspec, bspec,
                  pl.BlockSpec((4, DIFF_HD), const), pl.BlockSpec((2 * DIFF_HD, 1), const)],
        out_specs=qspec,
        scratch_shapes=[pltpu.VMEM((2, 1, tb), F32), pltpu.VMEM((2, 1, tb), F32),
                        pltpu.VMEM((2, LANES, tb), F32)],
        compiler_params=_params("arbitrary", "arbitrary", "arbitrary"),
        name="diff_attn",
    )(q1.reshape(shp), q2.reshape(shp), k.reshape(shp), vt, bprev, bdiag, lam_vecs, subln)
    return out.reshape(batch * seq, W)


def _swa_kernel(q_ref, k_ref, vt_ref, bp_ref, bc_ref, sink_ref, o_ref, *, ts):
    i = pl.program_id(1)
    blk = WINDOW
    low_half = lax.broadcasted_iota(jnp.int32, (blk, LANES), 1) < SWA_HD
    for sub in range(ts // blk):
        start = i * ts + sub * blk
        has_prev = start > 0
        pstart = pl.multiple_of(jnp.maximum(start - blk, 0), blk)
        cstart = pl.multiple_of(start, blk)
        outs = []
        for hk in range(SWA_KV_HEADS):
            kp = k_ref[0, pl.ds(pstart, blk), LANES * hk:LANES * (hk + 1)]
            kc = k_ref[0, pl.ds(cstart, blk), LANES * hk:LANES * (hk + 1)]
            vp = vt_ref[0, SWA_HD * hk:SWA_HD * (hk + 1), pl.ds(pstart, blk)]
            vc = vt_ref[0, SWA_HD * hk:SWA_HD * (hk + 1), pl.ds(cstart, blk)]
            keys = [(jnp.where(sel, kp, 0), jnp.where(sel, kc, 0))
                    for sel in (low_half, jnp.logical_not(low_half))]
            for g in range(SWA_GROUP):
                head = hk * SWA_GROUP + g
                qt = q_ref[0, sub * blk:(sub + 1) * blk, LANES * (head // 2):LANES * (head // 2 + 1)]
                kph, kch = keys[head % 2]
                s_p = jnp.where(has_prev, _dot_nt(kph, qt) + bp_ref[head], NEG_INF)
                s_c = _dot_nt(kch, qt) + bc_ref[head]
                sink = sink_ref[head]
                m = jnp.maximum(jnp.maximum(jnp.max(s_p, axis=0, keepdims=True),
                                            jnp.max(s_c, axis=0, keepdims=True)), sink)
                p_p = jnp.exp(s_p - m)
                p_c = jnp.exp(s_c - m)
                den = (jnp.sum(p_p, axis=0, keepdims=True) + jnp.sum(p_c, axis=0, keepdims=True)
                       + jnp.exp(sink - m))
                o = _dot(vp, p_p.astype(BF16)) + _dot(vc, p_c.astype(BF16))
                outs.append(o / den)
        o_ref[0, sub * blk:(sub + 1) * blk, :] = jnp.concatenate(outs, axis=0).T.astype(BF16)


def _swa(q, k2, vt, bias_prev, bias_cur, sinks, batch, seq):
    ts = min(SWA_TS, seq)
    WQ = SWA_HEADS * SWA_HD
    WK = SWA_KV_HEADS * LANES
    WV = SWA_KV_HEADS * SWA_HD
    const3 = lambda b, i: (0, 0, 0)
    out = pl.pallas_call(
        functools.partial(_swa_kernel, ts=ts),
        out_shape=jax.ShapeDtypeStruct((batch, seq, WQ), BF16),
        grid=(batch, seq // ts),
        in_specs=[
            pl.BlockSpec((1, ts, WQ), lambda b, i: (b, i, 0)),
            pl.BlockSpec((1, seq, WK), lambda b, i: (b, 0, 0)),
            pl.BlockSpec((1, WV, seq), lambda b, i: (b, 0, 0)),
            pl.BlockSpec((SWA_HEADS, WINDOW, WINDOW), const3),
            pl.BlockSpec((SWA_HEADS, WINDOW, WINDOW), const3),
            pl.BlockSpec((SWA_HEADS, 1, LANES), const3),
        ],
        out_specs=pl.BlockSpec((1, ts, WQ), lambda b, i: (b, i, 0)),
        compiler_params=_params("arbitrary", "arbitrary"),
        name="swa_attn",
    )(q.reshape(batch, seq, WQ), k2.reshape(batch, seq, WK), vt, bias_prev, bias_cur, sinks)
    return out.reshape(batch * seq, WQ)


def _merge_kernel(x_ref, mod_ref, lru_ref, diff_ref, swa_ref, gl_ref, wb_ref, wo_ref, o_ref):
    D = x_ref.shape[1]
    merged = None
    for n, br in enumerate((lru_ref, diff_ref, swa_ref)):
        gate = _sigmoid(gl_ref[:, n * D:(n + 1) * D].astype(F32))
        term = gate * _dot(br[...], wb_ref[n])
        merged = term if merged is None else merged + term
    out = _dot(merged.astype(BF16), wo_ref[...])
    o_ref[...] = x_ref[...] + mod_ref[0, 2:3, :] * out


def _merge(x2, mod, o_lru, o_diff, o_swa, gl, w_branch, w_out, seq):
    T, D = x2.shape
    tm = min(MERGE_TM, seq)
    per_b = seq // tm
    row = lambda i: (i, 0)
    return pl.pallas_call(
        _merge_kernel,
        out_shape=jax.ShapeDtypeStruct((T, D), F32),
        grid=(T // tm,),
        in_specs=[
            pl.BlockSpec((tm, D), row),
            pl.BlockSpec((1, 6, D), lambda i: (i // per_b, 0, 0)),
            pl.BlockSpec((tm, BRANCH_WIDTH), row),
            pl.BlockSpec((tm, BRANCH_WIDTH), row),
            pl.BlockSpec((tm, BRANCH_WIDTH), row),
            pl.BlockSpec((tm, N_BRANCHES * D), row),
            pl.BlockSpec(w_branch.shape, lambda i: (0, 0, 0)),
            pl.BlockSpec(w_out.shape, lambda i: (0, 0)),
        ],
        out_specs=pl.BlockSpec((tm, D), row),
        compiler_params=_params("arbitrary"),
        name="merge_outproj",
    )(x2, mod, o_lru, o_diff, o_swa, gl, w_branch, w_out)


def _router_kernel(x_ref, mod_ref, gain_ref, wr_ref, br_ref, tri_ref,
                   hp_ref, e_ref, pt_ref, rank_ref, cnt_ref, run_ref):
    @pl.when(pl.program_id(0) == 0)
    def _():
        run_ref[...] = jnp.zeros_like(run_ref)

    x = x_ref[...]
    tm, D = x.shape
    h = _rms_mod(x, gain_ref[...], mod_ref[0, 4:5, :], mod_ref[0, 3:4, :])
    for c in range(D // LANES):
        hp_ref[pl.ds(c, tm, stride=SUBLANES), :] = h[:, LANES * c:LANES * (c + 1)]

    logits = _dot_nt(wr_ref[...], h, precision=HIGHEST) + br_ref[...]
    eidx = lax.broadcasted_iota(jnp.int32, logits.shape, 0)
    vals, idxs, hots = [], [], []
    for _ in range(TOP_K):
        mx = jnp.max(logits, axis=0, keepdims=True)
        sel = jnp.min(jnp.where(logits == mx, eidx, N_EXPERTS), axis=0, keepdims=True)
        hot = eidx == sel
        vals.append(mx)
        idxs.append(sel)
        hots.append(hot)
        logits = jnp.where(hot, -jnp.inf, logits)
    e_ref[...] = jnp.concatenate(idxs, axis=0)
    top_v = jnp.concatenate(vals, axis=0)
    ex = jnp.exp(top_v - top_v[0:1])
    p = ex / jnp.sum(ex, axis=0, keepdims=True)
    pt_ref[...] = jnp.concatenate([p, jnp.zeros((LANES - TOP_K, tm), F32)], axis=0).T

    member = hots[0]
    for hot in hots[1:]:
        member = member | hot
    member = jnp.where(member, 1.0, 0.0)
    before = _dot(member.astype(BF16), tri_ref[...]) + run_ref[:, 0:1]
    ranks = [jnp.sum(jnp.where(hot, before, 0.0), axis=0, keepdims=True) for hot in hots]
    rank_ref[...] = jnp.concatenate(ranks, axis=0).astype(jnp.int32)
    run_ref[...] = run_ref[...] + jnp.sum(member, axis=1, keepdims=True)
    cnt_ref[...] = run_ref[...]


def _router(x2, mod, gain, w_router_t, b_router, seq):
    T, D = x2.shape
    tm = min(ROUTER_TM, seq)
    per_b = seq // tm
    tri = (jnp.arange(tm)[:, None] < jnp.arange(tm)[None, :]).astype(BF16)
    const = lambda i: (0, 0)
    return pl.pallas_call(
        _router_kernel,
        out_shape=[jax.ShapeDtypeStruct((T * SUBLANES, LANES), F32),
                   jax.ShapeDtypeStruct((TOP_K, T), jnp.int32),
                   jax.ShapeDtypeStruct((T, LANES), F32),
                   jax.ShapeDtypeStruct((TOP_K, T), jnp.int32),
                   jax.ShapeDtypeStruct((N_EXPERTS, LANES), F32)],
        grid=(T // tm,),
        in_specs=[
            pl.BlockSpec((tm, D), lambda i: (i, 0)),
            pl.BlockSpec((1, 6, D), lambda i: (i // per_b, 0, 0)),
            pl.BlockSpec((1, D), const),
            pl.BlockSpec((N_EXPERTS, D), const),
            pl.BlockSpec((N_EXPERTS, 1), const),
            pl.BlockSpec((tm, tm), const),
        ],
        out_specs=[pl.BlockSpec((tm * SUBLANES, LANES), lambda i: (i, 0)),
                   pl.BlockSpec((TOP_K, tm), lambda i: (0, i)),
                   pl.BlockSpec((tm, LANES), lambda i: (i, 0)),
                   pl.BlockSpec((TOP_K, tm), lambda i: (0, i)),
                   pl.BlockSpec((N_EXPERTS, LANES), const)],
        scratch_shapes=[pltpu.VMEM((N_EXPERTS, LANES), F32)],
        compiler_params=_params("arbitrary"),
        name="router",
    )(x2, mod, gain, w_router_t, b_router, tri)


def _tile_rows(idx, tm):
    K, T = idx.shape
    nb = T // tm
    return idx.reshape(K, nb, tm).transpose(1, 0, 2).reshape(nb, 1, K * tm)


def _dispatch_kernel(nused_ref, tok_ref, tnext_ref, hp_hbm, xs_ref, xbuf, sem, *, tm, per_moe):
    i = pl.program_id(0)
    slot = i % 2
    used = i * per_moe < nused_ref[0]
    next_used = ((i + 1) * per_moe < nused_ref[0]) & (i + 1 < pl.num_programs(0))

    def gather(idx_ref, s):
        def issue(r, carry):
            src = pl.multiple_of(idx_ref[0, 0, r] * SUBLANES, SUBLANES)
            dst = pl.multiple_of(r * SUBLANES, SUBLANES)
            pltpu.make_async_copy(hp_hbm.at[pl.ds(src, SUBLANES), :], xbuf.at[s, pl.ds(dst, SUBLANES), :],
                                  sem.at[s]).start()
            return carry

        lax.fori_loop(0, tm, issue, 0, unroll=8)

    @pl.when((i == 0) & used)
    def _():
        gather(tok_ref, 0)

    @pl.when(next_used)
    def _():
        gather(tnext_ref, 1 - slot)

    @pl.when(used)
    def _():
        pltpu.make_async_copy(hp_hbm.at[pl.ds(0, tm * SUBLANES), :], xbuf.at[slot], sem.at[slot]).wait()
        xs_ref[...] = xbuf[slot]

    @pl.when(jnp.logical_not(used))
    def _():
        xs_ref[...] = jnp.zeros(xs_ref.shape, xs_ref.dtype)


def _dispatch(n_used, row_token, hp):
    n_rows = row_token.shape[0]
    tm = DISP_TM
    nb = n_rows // tm
    rows = row_token.reshape(nb, 1, tm)
    return pl.pallas_call(
        functools.partial(_dispatch_kernel, tm=tm, per_moe=tm // MOE_TM),
        out_shape=jax.ShapeDtypeStruct((n_rows * SUBLANES, LANES), hp.dtype),
        grid_spec=pltpu.PrefetchScalarGridSpec(
            num_scalar_prefetch=1,
            grid=(nb,),
            in_specs=[
                pl.BlockSpec((1, 1, tm), lambda i, nu: (i, 0, 0), memory_space=pltpu.SMEM),
                pl.BlockSpec((1, 1, tm), lambda i, nu: (jnp.minimum(i + 1, nb - 1), 0, 0),
                             memory_space=pltpu.SMEM),
                pl.BlockSpec(memory_space=pl.ANY),
            ],
            out_specs=pl.BlockSpec((tm * SUBLANES, LANES), lambda i, nu: (i, 0)),
            scratch_shapes=[pltpu.VMEM((2, tm * SUBLANES, LANES), hp.dtype), pltpu.SemaphoreType.DMA((2,))],
        ),
        compiler_params=_params("arbitrary"),
        name="moe_dispatch",
    )(n_used, rows, rows, hp)


def _expert_kernel(be_ref, nused_ref, xs_ref, w1_ref, b1_ref, w2_ref, b2_ref, y_ref,
                   w1b_ref, w2s_ref, w2b_ref, act_ref):
    i = pl.program_id(0)
    used = i < nused_ref[0]
    chunks = w1_ref.shape[3] // (2 * LANES)
    hl = LANES // 2

    @pl.when(used & ((i == 0) | (be_ref[i] != be_ref[jnp.maximum(i - 1, 0)])))
    def _():
        for q in range(chunks):
            cs = slice(2 * LANES * q, 2 * LANES * (q + 1))
            w1b_ref[:, cs] = w1_ref[0, 0, :, cs].astype(BF16)
        cols = w2_ref.shape[3] // LANES
        for q in range(w2_ref.shape[2] // LANES):
            for s in range(2):
                rows = w2_ref[0, 0, LANES * q + hl * s:LANES * q + hl * (s + 1), :]
                for c in range(cols):
                    w2s_ref[c, pl.ds(LANES * q + s, hl, stride=2), :] = rows[:, LANES * c:LANES * (c + 1)]
        for c in range(cols):
            w2b_ref[:, LANES * c:LANES * (c + 1)] = w2s_ref[c].astype(BF16)

    @pl.when(used)
    def _():
        tm = xs_ref.shape[0] // SUBLANES
        x = jnp.concatenate([xs_ref[pl.ds(c, tm, stride=SUBLANES), :] for c in range(SUBLANES)],
                            axis=1).astype(BF16)
        even = lax.broadcasted_iota(jnp.int32, (tm, LANES), 1) % 2 == 0
        for q in range(chunks):
            cs = slice(2 * LANES * q, 2 * LANES * (q + 1))
            hq = _dot(x, w1b_ref[:, cs]) + b1_ref[0, 0, :, cs]
            lo, hi = hq[:, :LANES], hq[:, LANES:]
            glu = jnp.where(even, lo, pltpu.roll(hi, 1, axis=1))
            lin = jnp.where(even, pltpu.roll(lo, LANES - 1, axis=1), hi)
            glu = jnp.minimum(glu, SWIGLU_LIMIT)
            lin = jnp.clip(lin, -SWIGLU_LIMIT, SWIGLU_LIMIT)
            act = glu * _sigmoid(SWIGLU_ALPHA * glu) * (lin + 1.0)
            act_ref[:, LANES * q:LANES * (q + 1)] = act.astype(BF16)
        y = _dot(act_ref[...], w2b_ref[...]) + b2_ref[0, 0]
        for c in range(SUBLANES):
            y_ref[pl.ds(c, tm, stride=SUBLANES), :] = y[:, LANES * c:LANES * (c + 1)]

    @pl.when(jnp.logical_not(used))
    def _():
        y_ref[...] = jnp.zeros(y_ref.shape, y_ref.dtype)


def _experts(layer, block_expert, n_used, xs, w1, b1, w2, b2):
    _, E, D, F2 = w1.shape
    assert D == SUBLANES * LANES
    F = F2 // 2
    tm = MOE_TM
    nb = xs.shape[0] // (tm * SUBLANES)
    wmap = lambda i, be, nu: (layer, be[i], 0, 0)
    rmap = lambda i, be, nu: (jnp.maximum(jnp.minimum(i, nu[0] - 1), 0), 0)
    return pl.pallas_call(
        _expert_kernel,
        out_shape=jax.ShapeDtypeStruct(xs.shape, F32),
        grid_spec=pltpu.PrefetchScalarGridSpec(
            num_scalar_prefetch=2,
            grid=(nb,),
            in_specs=[
                pl.BlockSpec((tm * SUBLANES, LANES), rmap),
                pl.BlockSpec((1, 1, D, F2), wmap),
                pl.BlockSpec((1, 1, 1, F2), wmap),
                pl.BlockSpec((1, 1, F, D), wmap),
                pl.BlockSpec((1, 1, 1, D), wmap),
            ],
            out_specs=pl.BlockSpec((tm * SUBLANES, LANES), lambda i, be, nu: (i, 0)),
            scratch_shapes=[pltpu.VMEM((D, F2), BF16), pltpu.VMEM((D // LANES, F, LANES), F32),
                            pltpu.VMEM((F, D), BF16),
                            pltpu.VMEM((tm, F), BF16)],
        ),
        compiler_params=_params("arbitrary"),
        name="moe_experts",
    )(block_expert, n_used, xs, w1, b1, w2, b2)


def _combine_kernel(dest_ref, dnext_ref, y_hbm, pt_ref, x_ref, mod_ref, o_ref, ybuf, sem, *, tm):
    n = TOP_K * tm
    i = pl.program_id(0)
    slot = i % 2

    def gather(idx_ref, s):
        def issue(r, carry):
            src = pl.multiple_of(idx_ref[0, 0, r] * SUBLANES, SUBLANES)
            dst = pl.multiple_of(r * SUBLANES, SUBLANES)
            pltpu.make_async_copy(y_hbm.at[pl.ds(src, SUBLANES), :], ybuf.at[s, pl.ds(dst, SUBLANES), :],
                                  sem.at[s]).start()
            return carry

        lax.fori_loop(0, n, issue, 0, unroll=8)

    @pl.when(i == 0)
    def _():
        gather(dest_ref, 0)

    @pl.when(i + 1 < pl.num_programs(0))
    def _():
        gather(dnext_ref, 1 - slot)

    pltpu.make_async_copy(y_hbm.at[pl.ds(0, n * SUBLANES), :], ybuf.at[slot], sem.at[slot]).wait()

    rows = ybuf.at[slot]
    for c in range(SUBLANES):
        cs = slice(LANES * c, LANES * (c + 1))
        acc = rows[pl.ds(c, tm, stride=SUBLANES), :] * pt_ref[:, 0:1]
        for k in range(1, TOP_K):
            acc = acc + rows[pl.ds(k * tm * SUBLANES + c, tm, stride=SUBLANES), :] * pt_ref[:, k:k + 1]
        o_ref[:, cs] = x_ref[:, cs] + mod_ref[0, 5:6, cs] * acc


def _combine(dest, y, pt, x2, mod, seq):
    T, D = x2.shape
    tm = min(COMB_TM, seq)
    per_b = seq // tm
    nb = T // tm
    rows = _tile_rows(dest, tm)
    return pl.pallas_call(
        functools.partial(_combine_kernel, tm=tm),
        out_shape=jax.ShapeDtypeStruct((T, D), F32),
        grid=(nb,),
        in_specs=[
            pl.BlockSpec((1, 1, TOP_K * tm), lambda i: (i, 0, 0), memory_space=pltpu.SMEM),
            pl.BlockSpec((1, 1, TOP_K * tm), lambda i: (jnp.minimum(i + 1, nb - 1), 0, 0),
                         memory_space=pltpu.SMEM),
            pl.BlockSpec(memory_space=pl.ANY),
            pl.BlockSpec((tm, LANES), lambda i: (i, 0)),
            pl.BlockSpec((tm, D), lambda i: (i, 0)),
            pl.BlockSpec((1, 6, D), lambda i: (i // per_b, 0, 0)),
        ],
        out_specs=pl.BlockSpec((tm, D), lambda i: (i, 0)),
        scratch_shapes=[pltpu.VMEM((2, TOP_K * tm * SUBLANES, LANES), F32), pltpu.SemaphoreType.DMA((2,))],
        compiler_params=_params("arbitrary"),
        name="moe_combine",
    )(rows, rows, y, pt, x2, mod)


def _t5_bucket(dist):
    exact = REL_BUCKETS // 2
    log_ratio = jnp.log(jnp.maximum(dist, 1).astype(F32) / exact) / math.log(REL_MAX_DIST / exact)
    large = exact + (log_ratio * (REL_BUCKETS - exact)).astype(jnp.int32)
    return jnp.where(dist < exact, dist, jnp.minimum(large, REL_BUCKETS - 1))


def _diff_bias_tiles(rel_bias, tb):
    table = rel_bias[:, :DIFF_HEADS].astype(F32)
    shifted = (table - table[REL_BUCKETS - 1]) * LOG2E
    kk = jnp.arange(tb)[:, None]
    qq = jnp.arange(tb)[None, :]

    def tile(dist):
        return _bucket_lookup(shifted, _t5_bucket(jnp.maximum(dist, 0)))

    diag = jnp.where((qq - kk >= 0)[None], tile(qq - kk), NEG_INF)
    prev = tile(qq + tb - kk)
    return prev, diag


def _bucket_lookup(table, bucket):
    out = jnp.zeros((table.shape[1],) + bucket.shape, F32)
    for b in range(table.shape[0]):
        out = out + jnp.where(bucket[None] == b, table[b].reshape((-1,) + (1,) * bucket.ndim), 0.0)
    return out


def _swa_bias_tiles(rel_bias):
    qi = jnp.arange(WINDOW)[:, None]
    kj = jnp.arange(2 * WINDOW)[None, :]
    dist = WINDOW + qi - kj
    ok = (dist >= 0) & (dist < WINDOW)
    bias = _bucket_lookup(rel_bias[:, DIFF_HEADS:].astype(F32),
                          _t5_bucket(jnp.clip(dist, 0, WINDOW - 1)))
    bias = jnp.where(ok[None], bias, NEG_INF).transpose(0, 2, 1)
    return bias[:, :WINDOW], bias[:, WINDOW:]


def _block_diag(w):
    nb, n, _ = w.shape
    eye = jnp.eye(nb, dtype=w.dtype)
    return (eye[:, None, :, None] * w[:, :, None, :]).reshape(nb * n, nb * n)


def _routing(counts, top_e, rank, tm):
    K, T = top_e.shape
    counts = counts.astype(jnp.int32)
    padded = (counts + tm - 1) // tm * tm
    pad_ends = jnp.cumsum(padded)
    pad_starts = pad_ends - padded
    onehot = top_e[..., None] == jnp.arange(N_EXPERTS, dtype=jnp.int32)
    dest = rank + jnp.sum(jnp.where(onehot, pad_starts, 0), axis=-1)
    n_rows = K * T + N_EXPERTS * tm
    nb = n_rows // tm
    n_used = pad_ends[-1] // tm
    blk = jnp.minimum(jnp.arange(nb), n_used - 1) * tm
    block_expert = jnp.sum(blk[:, None] >= pad_ends[None, :], axis=-1)
    fill_ends = jnp.cumsum(padded - counts)
    fill_expert = jnp.sum(jnp.arange(N_EXPERTS * tm)[:, None] >= fill_ends[None, :], axis=-1)
    keys = jnp.concatenate([(top_e * T + jnp.arange(T, dtype=jnp.int32)[None, :]).reshape(-1),
                            (fill_expert * T + (T - 1)).astype(jnp.int32)])
    row_token = jnp.sort(keys) % T
    return (block_expert.astype(jnp.int32), n_used.astype(jnp.int32).reshape(1), dest.astype(jnp.int32),
            row_token.astype(jnp.int32))


def kernel(x, c, w_ada, b_ada, norm_mix, norm_ffn, w_in, conv_w, conv_b, lru_wa, lru_ba, lru_wx, lru_bx,
           lru_lambda, diff_qnorm, diff_knorm, diff_lambda, diff_subln, swa_qnorm, swa_knorm, swa_sinks,
           rel_bias, w_branch, w_out, w_router, b_router, w1, b1, w2, b2):
    B, S, D = x.shape
    L = w_ada.shape[0]
    T = B * S
    tb = min(ATT_TB, S)

    mods = _adaln(c, w_ada, b_ada)
    bprev, bdiag = _diff_bias_tiles(rel_bias, tb)
    sbias_prev, sbias_cur = _swa_bias_tiles(rel_bias)
    gidx = jnp.arange(512) // DIFF_HD
    gmat = jnp.where(gidx[:, None] == gidx[None, :], 1.0 / DIFF_HD, 0.0).astype(BF16)

    x2 = x.reshape(T, D)
    for l in range(L):
        lambda_init = 0.8 - 0.6 * math.exp(-0.3 * l)
        mod = mods[l]
        qkg = jnp.stack([
            jnp.tile(diff_qnorm[l], 8) * (DIFF_HD ** -0.5 * LOG2E),
            jnp.tile(diff_knorm[l], 8),
            jnp.tile(swa_qnorm[l], 8) * SWA_HD ** -0.5,
            jnp.tile(swa_knorm[l], 8),
        ]).astype(F32)
        w_in_l = w_in[l].astype(BF16)
        wvt = jnp.concatenate([w_in_l[:, 2048:2560], w_in_l[:, 3200:3328]], axis=1).T
        wsk = jnp.concatenate([w_in_l[:, 3072 + SWA_HD * (j // 2):3072 + SWA_HD * (j // 2 + 1)]
                               for j in range(2 * SWA_KV_HEADS)], axis=1)
        xg, q1, q2, dk, dvt, sq, sk2, svt, gl = _inproj(
            x2, mod, norm_mix[l].reshape(1, D), w_in_l, wvt, wsk, gmat, qkg, S)

        wg = jnp.concatenate([_block_diag(lru_wa[l]), _block_diag(lru_wx[l])], axis=1).astype(BF16)
        bg = jnp.concatenate([lru_ba[l], lru_bx[l]]).reshape(1, 2 * LRU_WIDTH)
        sp = jax.nn.softplus(-lru_lambda[l].astype(F32)).reshape(1, LRU_WIDTH)
        o_lru = _rglru(xg, conv_w[l], conv_b[l].reshape(1, LRU_WIDTH), wg, bg, sp, B, S)

        o_diff = _diff_attn(q1, q2, dk, dvt, bprev, bdiag, diff_lambda[l],
                            diff_subln[l].reshape(2 * DIFF_HD, 1), lambda_init, B, S)

        sinks = jnp.broadcast_to(swa_sinks[l].astype(F32)[:, None, None], (SWA_HEADS, 1, LANES))
        o_swa = _swa(sq, sk2, svt, sbias_prev, sbias_cur, sinks, B, S)

        x2 = _merge(x2, mod, o_lru, o_diff, o_swa, gl, w_branch[l].astype(BF16), w_out[l].astype(BF16), S)

        hp, top_e, pt, rank, counts = _router(x2, mod, norm_ffn[l].reshape(1, D), w_router[l].T,
                                              b_router[l].reshape(N_EXPERTS, 1), S)
        block_expert, n_used, dest, row_token = _routing(counts[:, 0], top_e, rank, MOE_TM)
        xs = _dispatch(n_used, row_token, hp)
        y = _experts(l, block_expert, n_used, xs, w1, b1[:, :, None, :], w2, b2[:, :, None, :])
        x2 = _combine(dest, y, pt, x2, mod, S)
    return x2.reshape(B, S, D)
```

```python
import functools
import math

import jax
import jax.numpy as jnp
from jax import lax
from jax.experimental import pallas as pl
from jax.experimental.pallas import tpu as pltpu

F32 = jnp.float32
BF16 = jnp.bfloat16

LRU_WIDTH = 512
LRU_BLOCKS = 8
LRU_C = 8.0
CONV_W = 4
DIFF_HEADS = 4
DIFF_HD = 64
SWA_HEADS = 8
SWA_KV_HEADS = 2
SWA_GROUP = SWA_HEADS // SWA_KV_HEADS
SWA_HD = 64
WINDOW = 128
N_BRANCHES = 3
BRANCH_WIDTH = 512
REL_BUCKETS = 32
REL_MAX_DIST = 128
N_EXPERTS = 32
TOP_K = 4
SWIGLU_LIMIT = 7.0
SWIGLU_ALPHA = 1.702
NORM_EPS = 1e-6
NEG_INF = -1e30
LOG2E = math.log2(math.e)

VMEM_LIMIT_BYTES = 56 * 1024 * 1024
LANES = 128
SUBLANES = 8

ADALN_TN = 1536
PROJ_TM = 512
LRU_TS = 512
ATT_TB = 512
SWA_TS = 512
MERGE_TM = 512
ROUTER_TM = 512
MOE_TM = 512
COMB_TM = 128

HIGHEST = lax.Precision.HIGHEST


def _params(*sem):
    return pltpu.CompilerParams(dimension_semantics=sem, vmem_limit_bytes=VMEM_LIMIT_BYTES)


def _dot(a, b, **kw):
    return jnp.dot(a, b, preferred_element_type=F32, **kw)


def _dot_nt(a, b, **kw):
    return lax.dot_general(a, b, (((1,), (1,)), ((), ())), preferred_element_type=F32, **kw)


def _sigmoid(x):
    return 1.0 / (1.0 + jnp.exp(-x))


def _adaln_kernel(c_ref, w_ref, b_ref, o_ref):
    c = c_ref[...]
    cond = c * _sigmoid(c)
    o_ref[0] = _dot(cond, w_ref[0], precision=HIGHEST) + b_ref[0]


def _adaln(c, w_ada, b_ada):
    L, D, N = w_ada.shape
    B = c.shape[0]
    rows = 8
    cp = jnp.zeros((rows, D), F32).at[:B].set(c)
    out = pl.pallas_call(
        _adaln_kernel,
        out_shape=jax.ShapeDtypeStruct((L, rows, N), F32),
        grid=(L, N // ADALN_TN),
        in_specs=[
            pl.BlockSpec((rows, D), lambda l, j: (0, 0)),
            pl.BlockSpec((1, D, ADALN_TN), lambda l, j: (l, 0, j)),
            pl.BlockSpec((1, 1, ADALN_TN), lambda l, j: (l, 0, j)),
        ],
        out_specs=pl.BlockSpec((1, rows, ADALN_TN), lambda l, j: (l, 0, j)),
        compiler_params=_params("arbitrary", "arbitrary"),
        name="adaln",
    )(cp, w_ada, b_ada.reshape(L, 1, N))
    return out[:, :B].reshape(L, B, 6, D)


def _rms_mod(x, gain, scale, shift):
    ms = jnp.mean(x * x, axis=-1, keepdims=True)
    return (x * lax.rsqrt(ms + NORM_EPS) * gain) * (1.0 + scale) + shift


def _group_rms(x, gmat):
    sq = x * x
    hi = sq.astype(BF16)
    lo = (sq - hi.astype(F32)).astype(BF16)
    ms = _dot(hi, gmat) + _dot(lo, gmat)
    return x * lax.rsqrt(ms + NORM_EPS)


def _inproj_kernel(x_ref, mod_ref, gain_ref, w_ref, wvt_ref, gmat_ref, qkg_ref,
                   xg_ref, q1_ref, q2_ref, dk_ref, dvt_ref, sq_ref, sk_ref, sv_ref, gl_ref):
    x = x_ref[...]
    h = _rms_mod(x, gain_ref[...], mod_ref[0, 1:2, :], mod_ref[0, 0:1, :]).astype(BF16)
    gmat = gmat_ref[...]

    xg_ref[...] = _dot(h, w_ref[:, 0:1024]).astype(BF16)

    dd = _dot(h, w_ref[:, 1024:2048])
    qn = _group_rms(dd[:, 0:512], gmat) * qkg_ref[0:1, :]
    kn = _group_rms(dd[:, 512:1024], gmat) * qkg_ref[1:2, :]
    lane = lax.broadcasted_iota(jnp.int32, qn.shape, 1) % LANES
    q1_ref[...] = jnp.where(lane < DIFF_HD, qn, 0.0).astype(BF16)
    q2_ref[...] = jnp.where(lane >= DIFF_HD, qn, 0.0).astype(BF16)
    dk_ref[...] = kn.astype(BF16)
    dvt_ref[0] = _dot_nt(wvt_ref[...], h).astype(BF16)

    ss = _dot(h, w_ref[:, 2560:3328])
    sq_ref[...] = (_group_rms(ss[:, 0:512], gmat) * qkg_ref[2:3, :]).astype(BF16)
    sk_ref[...] = (_group_rms(ss[:, 512:640], gmat[0:128, 0:128]) * qkg_ref[3:4, 0:128]).astype(BF16)
    sv_ref[...] = ss[:, 640:768].astype(BF16)

    gl_ref[...] = _dot(h, w_ref[:, 3328:6400]).astype(BF16)


def _inproj(x2, mod, gain, w_in, wvt, gmat, qkg, seq):
    T, D = x2.shape
    tm = min(PROJ_TM, seq)
    per_b = seq // tm
    VW = wvt.shape[0]
    widths = (1024, 512, 512, 512, None, 512, 128, 128, 3072)
    row = lambda i: (i, 0)
    const = lambda i: (0, 0)
    vt_shape = jax.ShapeDtypeStruct((T // seq, VW, seq), BF16)
    vt_spec = pl.BlockSpec((1, VW, tm), lambda i: (i // per_b, 0, i % per_b))
    return pl.pallas_call(
        _inproj_kernel,
        out_shape=[vt_shape if w is None else jax.ShapeDtypeStruct((T, w), BF16) for w in widths],
        grid=(T // tm,),
        in_specs=[
            pl.BlockSpec((tm, D), row),
            pl.BlockSpec((1, 6, D), lambda i: (i // per_b, 0, 0)),
            pl.BlockSpec((1, D), const),
            pl.BlockSpec(w_in.shape, const, pipeline_mode=pl.Buffered(1)),
            pl.BlockSpec(wvt.shape, const),
            pl.BlockSpec(gmat.shape, const),
            pl.BlockSpec(qkg.shape, const),
        ],
        out_specs=[vt_spec if w is None else pl.BlockSpec((tm, w), row) for w in widths],
        compiler_params=_params("arbitrary"),
        name="inproj",
    )(x2, mod, gain, w_in, wvt, gmat, qkg)


def _rglru_kernel(xg_ref, cw_ref, cb_ref, wg_ref, bg_ref, sp_ref, o_ref, ext_ref, hc_ref):
    ts = xg_ref.shape[0]
    C = LRU_WIDTH

    @pl.when(pl.program_id(1) == 0)
    def _():
        ext_ref[0:8, :] = jnp.zeros((8, C), F32)
        hc_ref[...] = jnp.zeros_like(hc_ref)

    xr = xg_ref[:, 0:C].astype(F32)
    ext_ref[8:8 + ts, :] = xr
    xc = cb_ref[...] + xr * cw_ref[CONV_W - 1:CONV_W, :]
    for back in range(1, CONV_W):
        tap = CONV_W - 1 - back
        xc = xc + ext_ref[8 - back:8 - back + ts, :] * cw_ref[tap:tap + 1, :]
    ext_ref[0:8, :] = xr[ts - 8:ts, :]

    gates = _dot(xc.astype(BF16), wg_ref[...]) + bg_ref[...]
    r = _sigmoid(gates[:, 0:C])
    gi = _sigmoid(gates[:, C:2 * C])
    log_a = (-LRU_C) * r * sp_ref[...]
    a = jnp.exp(log_a)
    b = xc * gi * jnp.sqrt(1.0 - a * a)

    rows = lax.broadcasted_iota(jnp.int32, (ts, C), 0)
    d = 1
    while d < ts:
        keep = rows >= d
        a_sh = pltpu.roll(a, d, axis=0)
        b_sh = pltpu.roll(b, d, axis=0)
        b = jnp.where(keep, a * b_sh + b, b)
        a = jnp.where(keep, a * a_sh, a)
        d *= 2
    h = b + a * hc_ref[...]
    hc_ref[...] = h[ts - 1:ts, :]

    gr = xg_ref[:, C:2 * C].astype(F32)
    gelu = 0.5 * gr * (1.0 + jnp.tanh(math.sqrt(2.0 / math.pi) * (gr + 0.044715 * gr * gr * gr)))
    o_ref[...] = (h * gelu).astype(BF16)


def _rglru(xg, conv_w, conv_b, wg, bg, softplus_neg_lam, batch, seq):
    T = xg.shape[0]
    C = LRU_WIDTH
    ts = min(LRU_TS, seq)
    per_b = seq // ts
    const = lambda b, i: (0, 0)
    return pl.pallas_call(
        _rglru_kernel,
        out_shape=jax.ShapeDtypeStruct((T, C), BF16),
        grid=(batch, per_b),
        in_specs=[
            pl.BlockSpec((ts, 2 * C), lambda b, i: (b * per_b + i, 0)),
            pl.BlockSpec((CONV_W, C), const),
            pl.BlockSpec((1, C), const),
            pl.BlockSpec((C, 2 * C), const),
            pl.BlockSpec((1, 2 * C), const),
            pl.BlockSpec((1, C), const),
        ],
        out_specs=pl.BlockSpec((ts, C), lambda b, i: (b * per_b + i, 0)),
        scratch_shapes=[pltpu.VMEM((ts + 8, C), F32), pltpu.VMEM((1, C), F32)],
        compiler_params=_params("arbitrary", "arbitrary"),
        name="rglru",
    )(xg, conv_w, conv_b, wg, bg, softplus_neg_lam)


def _diff_attn_kernel(q1_ref, q2_ref, k_ref, vt_ref, bprev_ref, bdiag_ref, lam_ref, sub_ref, o_ref,
                      m_ref, l_ref, acc_ref, *, tb, lambda_init):
    i = pl.program_id(2)
    q = (q1_ref[0], q2_ref[0])

    m_ref[...] = jnp.full(m_ref.shape, NEG_INF, F32)
    l_ref[...] = jnp.zeros(l_ref.shape, F32)
    acc_ref[...] = jnp.zeros(acc_ref.shape, F32)

    def attend(start, tk, bias):
        kb = k_ref[0, pl.ds(start, tk), :]
        vb = vt_ref[0, :, pl.ds(start, tk)]
        scores = [_dot_nt(kb, q[mp]) for mp in range(2)]
        for mp in range(2):
            s = scores[mp]
            if bias is not None:
                s = s + bias
            m_old = m_ref[mp]
            m_new = jnp.maximum(m_old, jnp.max(s, axis=0, keepdims=True))
            alpha = jnp.exp2(m_old - m_new)
            p = jnp.exp2(s - m_new)
            l_ref[mp] = alpha * l_ref[mp] + jnp.sum(p, axis=0, keepdims=True)
            acc_ref[mp] = alpha * acc_ref[mp] + _dot(vb, p.astype(BF16))
            m_ref[mp] = m_new

    n_far = jnp.maximum(i - 1, 0)

    def far(j, carry):
        attend(pl.multiple_of(j * (2 * tb), 2 * tb), 2 * tb, None)
        return carry

    lax.fori_loop(0, n_far // 2, far, 0)

    @pl.when(n_far % 2 == 1)
    def _():
        attend(pl.multiple_of((n_far - 1) * tb, tb), tb, None)

    @pl.when(i > 0)
    def _():
        attend(pl.multiple_of((i - 1) * tb, tb), tb, bprev_ref[0])

    attend(pl.multiple_of(i * tb, tb), tb, bdiag_ref[0])

    lv = lam_ref[...]
    lam = (jnp.exp(jnp.sum(lv[0:1] * lv[1:2], axis=-1, keepdims=True))
           - jnp.exp(jnp.sum(lv[2:3] * lv[3:4], axis=-1, keepdims=True)) + lambda_init)
    o = acc_ref[0] / l_ref[0] - lam * (acc_ref[1] / l_ref[1])
    ms = jnp.mean(o * o, axis=0, keepdims=True)
    o = o * lax.rsqrt(ms + NORM_EPS) * sub_ref[...] * (1.0 - lambda_init)
    o_ref[0] = o.T.astype(BF16)


def _diff_attn(q1, q2, k, vt, bprev, bdiag, lam_vecs, subln, lambda_init, batch, seq):
    W = DIFF_HEADS * 2 * DIFF_HD
    tb = min(ATT_TB, seq)
    shp = (batch, seq, W)
    qspec = pl.BlockSpec((1, tb, LANES), lambda b, h, i: (b, i, h))
    kspec = pl.BlockSpec((1, seq, LANES), lambda b, h, i: (b, 0, h))
    vspec = pl.BlockSpec((1, LANES, seq), lambda b, h, i: (b, h, 0))
    bspec = pl.BlockSpec((1, tb, tb), lambda b, h, i: (h, 0, 0))
    const = lambda b, h, i: (0, 0)
    out = pl.pallas_call(
        functools.partial(_diff_attn_kernel, tb=tb, lambda_init=lambda_init),
        out_shape=jax.ShapeDtypeStruct(shp, BF16),
        grid=(batch, DIFF_HEADS, seq // tb),
        in_specs=[qspec, qspec, kspec, vspec, bspec, bspec,
                  pl.BlockSpec((4, DIFF_HD), const), pl.BlockSpec((2 * DIFF_HD, 1), const)],
        out_specs=qspec,
        scratch_shapes=[pltpu.VMEM((2, 1, tb), F32), pltpu.VMEM((2, 1, tb), F32),
                        pltpu.VMEM((2, LANES, tb), F32)],
        compiler_params=_params("arbitrary", "arbitrary", "arbitrary"),
        name="diff_attn",
    )(q1.reshape(shp), q2.reshape(shp), k.reshape(shp), vt, bprev, bdiag, lam_vecs, subln)
    return out.reshape(batch * seq, W)


def _swa_kernel(q_ref, k_ref, v_ref, bp_ref, bc_ref, sink_ref, o_ref, *, ts):
    i = pl.program_id(1)
    blk = WINDOW
    for sub in range(ts // blk):
        start = i * ts + sub * blk
        has_prev = start > 0
        pstart = pl.multiple_of(jnp.maximum(start - blk, 0), blk)
        cstart = pl.multiple_of(start, blk)
        kp = k_ref[0, pl.ds(pstart, blk), :]
        kc = k_ref[0, pl.ds(cstart, blk), :]
        vp = v_ref[0, pl.ds(pstart, blk), :]
        vc = v_ref[0, pl.ds(cstart, blk), :]
        qs = q_ref[0, sub * blk:(sub + 1) * blk, :]
        outs = []
        for hk in range(SWA_KV_HEADS):
            c0 = hk * SWA_GROUP * SWA_HD
            qh = jnp.concatenate(
                [qs[:, c0 + g * SWA_HD:c0 + (g + 1) * SWA_HD] for g in range(SWA_GROUP)], axis=0)
            ksl = slice(hk * SWA_HD, (hk + 1) * SWA_HD)
            s_p = _dot_nt(qh, kp[:, ksl]) + bp_ref[hk]
            s_p = jnp.where(has_prev, s_p, NEG_INF)
            s_c = _dot_nt(qh, kc[:, ksl]) + bc_ref[hk]
            sink = sink_ref[hk]
            m = jnp.maximum(jnp.maximum(jnp.max(s_p, axis=-1, keepdims=True),
                                        jnp.max(s_c, axis=-1, keepdims=True)), sink)
            p_p = jnp.exp(s_p - m)
            p_c = jnp.exp(s_c - m)
            den = (jnp.sum(p_p, axis=-1, keepdims=True) + jnp.sum(p_c, axis=-1, keepdims=True)
                   + jnp.exp(sink - m))
            o = _dot(p_p.astype(BF16), vp[:, ksl]) + _dot(p_c.astype(BF16), vc[:, ksl])
            o = o / den
            outs.extend(o[g * blk:(g + 1) * blk, :] for g in range(SWA_GROUP))
        o_ref[0, sub * blk:(sub + 1) * blk, :] = jnp.concatenate(outs, axis=1).astype(BF16)


def _swa(q, k, v, bias_prev, bias_cur, sinks, batch, seq):
    ts = min(SWA_TS, seq)
    WQ = SWA_HEADS * SWA_HD
    WK = SWA_KV_HEADS * SWA_HD
    rows = SWA_GROUP * WINDOW
    const3 = lambda b, i: (0, 0, 0)
    out = pl.pallas_call(
        functools.partial(_swa_kernel, ts=ts),
        out_shape=jax.ShapeDtypeStruct((batch, seq, WQ), BF16),
        grid=(batch, seq // ts),
        in_specs=[
            pl.BlockSpec((1, ts, WQ), lambda b, i: (b, i, 0)),
            pl.BlockSpec((1, seq, WK), lambda b, i: (b, 0, 0)),
            pl.BlockSpec((1, seq, WK), lambda b, i: (b, 0, 0)),
            pl.BlockSpec((SWA_KV_HEADS, rows, WINDOW), const3),
            pl.BlockSpec((SWA_KV_HEADS, rows, WINDOW), const3),
            pl.BlockSpec((SWA_KV_HEADS, rows, 1), const3),
        ],
        out_specs=pl.BlockSpec((1, ts, WQ), lambda b, i: (b, i, 0)),
        compiler_params=_params("arbitrary", "arbitrary"),
        name="swa_attn",
    )(q.reshape(batch, seq, WQ), k.reshape(batch, seq, WK), v.reshape(batch, seq, WK),
      bias_prev, bias_cur, sinks)
    return out.reshape(batch * seq, WQ)


def _merge_kernel(x_ref, mod_ref, lru_ref, diff_ref, swa_ref, gl_ref, wb_ref, wo_ref, o_ref):
    D = x_ref.shape[1]
    merged = None
    for n, br in enumerate((lru_ref, diff_ref, swa_ref)):
        gate = _sigmoid(gl_ref[:, n * D:(n + 1) * D].astype(F32))
        term = gate * _dot(br[...], wb_ref[n])
        merged = term if merged is None else merged + term
    out = _dot(merged.astype(BF16), wo_ref[...])
    o_ref[...] = x_ref[...] + mod_ref[0, 2:3, :] * out


def _merge(x2, mod, o_lru, o_diff, o_swa, gl, w_branch, w_out, seq):
    T, D = x2.shape
    tm = min(MERGE_TM, seq)
    per_b = seq // tm
    row = lambda i: (i, 0)
    return pl.pallas_call(
        _merge_kernel,
        out_shape=jax.ShapeDtypeStruct((T, D), F32),
        grid=(T // tm,),
        in_specs=[
            pl.BlockSpec((tm, D), row),
            pl.BlockSpec((1, 6, D), lambda i: (i // per_b, 0, 0)),
            pl.BlockSpec((tm, BRANCH_WIDTH), row),
            pl.BlockSpec((tm, BRANCH_WIDTH), row),
            pl.BlockSpec((tm, BRANCH_WIDTH), row),
            pl.BlockSpec((tm, N_BRANCHES * D), row),
            pl.BlockSpec(w_branch.shape, lambda i: (0, 0, 0)),
            pl.BlockSpec(w_out.shape, lambda i: (0, 0)),
        ],
        out_specs=pl.BlockSpec((tm, D), row),
        compiler_params=_params("arbitrary"),
        name="merge_outproj",
    )(x2, mod, o_lru, o_diff, o_swa, gl, w_branch, w_out)


def _router_kernel(x_ref, mod_ref, gain_ref, wr_ref, br_ref, tri_ref,
                   hp_ref, e_ref, pt_ref, rank_ref, cnt_ref, run_ref):
    @pl.when(pl.program_id(0) == 0)
    def _():
        run_ref[...] = jnp.zeros_like(run_ref)

    x = x_ref[...]
    tm, D = x.shape
    h = _rms_mod(x, gain_ref[...], mod_ref[0, 4:5, :], mod_ref[0, 3:4, :])
    for c in range(D // LANES):
        hp_ref[pl.ds(c, tm, stride=SUBLANES), :] = h[:, LANES * c:LANES * (c + 1)]

    logits = _dot_nt(wr_ref[...], h, precision=HIGHEST) + br_ref[...]
    eidx = lax.broadcasted_iota(jnp.int32, logits.shape, 0)
    vals, idxs, hots = [], [], []
    for _ in range(TOP_K):
        mx = jnp.max(logits, axis=0, keepdims=True)
        sel = jnp.min(jnp.where(logits == mx, eidx, N_EXPERTS), axis=0, keepdims=True)
        hot = eidx == sel
        vals.append(mx)
        idxs.append(sel)
        hots.append(hot)
        logits = jnp.where(hot, -jnp.inf, logits)
    e_ref[...] = jnp.concatenate(idxs, axis=0)
    top_v = jnp.concatenate(vals, axis=0)
    ex = jnp.exp(top_v - top_v[0:1])
    p = ex / jnp.sum(ex, axis=0, keepdims=True)
    pt_ref[...] = jnp.concatenate([p, jnp.zeros((LANES - TOP_K, tm), F32)], axis=0).T

    member = hots[0]
    for hot in hots[1:]:
        member = member | hot
    member = jnp.where(member, 1.0, 0.0)
    before = _dot(member.astype(BF16), tri_ref[...]) + run_ref[:, 0:1]
    ranks = [jnp.sum(jnp.where(hot, before, 0.0), axis=0, keepdims=True) for hot in hots]
    rank_ref[...] = jnp.concatenate(ranks, axis=0).astype(jnp.int32)
    run_ref[...] = run_ref[...] + jnp.sum(member, axis=1, keepdims=True)
    cnt_ref[...] = run_ref[...]


def _router(x2, mod, gain, w_router_t, b_router, seq):
    T, D = x2.shape
    tm = min(ROUTER_TM, seq)
    per_b = seq // tm
    tri = (jnp.arange(tm)[:, None] < jnp.arange(tm)[None, :]).astype(BF16)
    const = lambda i: (0, 0)
    return pl.pallas_call(
        _router_kernel,
        out_shape=[jax.ShapeDtypeStruct((T * SUBLANES, LANES), F32),
                   jax.ShapeDtypeStruct((TOP_K, T), jnp.int32),
                   jax.ShapeDtypeStruct((T, LANES), F32),
                   jax.ShapeDtypeStruct((TOP_K, T), jnp.int32),
                   jax.ShapeDtypeStruct((N_EXPERTS, LANES), F32)],
        grid=(T // tm,),
        in_specs=[
            pl.BlockSpec((tm, D), lambda i: (i, 0)),
            pl.BlockSpec((1, 6, D), lambda i: (i // per_b, 0, 0)),
            pl.BlockSpec((1, D), const),
            pl.BlockSpec((N_EXPERTS, D), const),
            pl.BlockSpec((N_EXPERTS, 1), const),
            pl.BlockSpec((tm, tm), const),
        ],
        out_specs=[pl.BlockSpec((tm * SUBLANES, LANES), lambda i: (i, 0)),
                   pl.BlockSpec((TOP_K, tm), lambda i: (0, i)),
                   pl.BlockSpec((tm, LANES), lambda i: (i, 0)),
                   pl.BlockSpec((TOP_K, tm), lambda i: (0, i)),
                   pl.BlockSpec((N_EXPERTS, LANES), const)],
        scratch_shapes=[pltpu.VMEM((N_EXPERTS, LANES), F32)],
        compiler_params=_params("arbitrary"),
        name="router",
    )(x2, mod, gain, w_router_t, b_router, tri)


def _tile_rows(idx, tm):
    K, T = idx.shape
    nb = T // tm
    return idx.reshape(K, nb, tm).transpose(1, 0, 2).reshape(nb, 1, K * tm)


def _expert_kernel(be_ref, nused_ref, tok_ref, tnext_ref, hp_hbm, w1_ref, b1_ref, w2_ref, b2_ref, y_ref,
                   xbuf, sem, w1b_ref, w2s_ref, w2b_ref, act_ref, *, tm):
    i = pl.program_id(0)
    slot = i % 2
    used = i < nused_ref[0]
    next_used = (i + 1 < nused_ref[0]) & (i + 1 < pl.num_programs(0))
    chunks = w1_ref.shape[3] // (2 * LANES)
    hl = LANES // 2

    def gather(idx_ref, s):
        def issue(r, carry):
            src = pl.multiple_of(idx_ref[0, 0, r] * SUBLANES, SUBLANES)
            dst = pl.multiple_of(r * SUBLANES, SUBLANES)
            pltpu.make_async_copy(hp_hbm.at[pl.ds(src, SUBLANES), :], xbuf.at[s, pl.ds(dst, SUBLANES), :],
                                  sem.at[s]).start()
            return carry

        lax.fori_loop(0, tm, issue, 0, unroll=8)

    @pl.when((i == 0) & used)
    def _():
        gather(tok_ref, 0)

    @pl.when(next_used)
    def _():
        gather(tnext_ref, 1 - slot)

    @pl.when(used & ((i == 0) | (be_ref[i] != be_ref[jnp.maximum(i - 1, 0)])))
    def _():
        for q in range(chunks):
            cs = slice(2 * LANES * q, 2 * LANES * (q + 1))
            w1b_ref[:, cs] = w1_ref[0, 0, :, cs].astype(BF16)
        cols = w2_ref.shape[3] // LANES
        for q in range(w2_ref.shape[2] // LANES):
            for s in range(2):
                rows = w2_ref[0, 0, LANES * q + hl * s:LANES * q + hl * (s + 1), :]
                for c in range(cols):
                    w2s_ref[c, pl.ds(LANES * q + s, hl, stride=2), :] = rows[:, LANES * c:LANES * (c + 1)]
        for c in range(cols):
            w2b_ref[:, LANES * c:LANES * (c + 1)] = w2s_ref[c].astype(BF16)

    @pl.when(used)
    def _():
        pltpu.make_async_copy(hp_hbm.at[pl.ds(0, tm * SUBLANES), :], xbuf.at[slot], sem.at[slot]).wait()
        xs = xbuf.at[slot]
        x = jnp.concatenate([xs[pl.ds(c, tm, stride=SUBLANES), :] for c in range(SUBLANES)],
                            axis=1).astype(BF16)
        even = lax.broadcasted_iota(jnp.int32, (tm, LANES), 1) % 2 == 0
        for q in range(chunks):
            cs = slice(2 * LANES * q, 2 * LANES * (q + 1))
            hq = _dot(x, w1b_ref[:, cs]) + b1_ref[0, 0, :, cs]
            lo, hi = hq[:, :LANES], hq[:, LANES:]
            glu = jnp.where(even, lo, pltpu.roll(hi, 1, axis=1))
            lin = jnp.where(even, pltpu.roll(lo, LANES - 1, axis=1), hi)
            glu = jnp.minimum(glu, SWIGLU_LIMIT)
            lin = jnp.clip(lin, -SWIGLU_LIMIT, SWIGLU_LIMIT)
            act = glu * _sigmoid(SWIGLU_ALPHA * glu) * (lin + 1.0)
            act_ref[:, LANES * q:LANES * (q + 1)] = act.astype(BF16)
        y = _dot(act_ref[...], w2b_ref[...]) + b2_ref[0, 0]
        for c in range(SUBLANES):
            y_ref[pl.ds(c, tm, stride=SUBLANES), :] = y[:, LANES * c:LANES * (c + 1)]

    @pl.when(jnp.logical_not(used))
    def _():
        y_ref[...] = jnp.zeros(y_ref.shape, y_ref.dtype)


def _experts(layer, block_expert, n_used, row_token, hp, w1, b1, w2, b2):
    _, E, D, F2 = w1.shape
    assert D == SUBLANES * LANES
    F = F2 // 2
    tm = MOE_TM
    n_rows = row_token.shape[0]
    nb = n_rows // tm
    rows = row_token.reshape(nb, 1, tm)
    wmap = lambda i, be, nu: (layer, be[i], 0, 0)
    return pl.pallas_call(
        functools.partial(_expert_kernel, tm=tm),
        out_shape=jax.ShapeDtypeStruct((n_rows * SUBLANES, LANES), F32),
        grid_spec=pltpu.PrefetchScalarGridSpec(
            num_scalar_prefetch=2,
            grid=(nb,),
            in_specs=[
                pl.BlockSpec((1, 1, tm), lambda i, be, nu: (i, 0, 0), memory_space=pltpu.SMEM),
                pl.BlockSpec((1, 1, tm), lambda i, be, nu: (jnp.minimum(i + 1, nb - 1), 0, 0),
                             memory_space=pltpu.SMEM),
                pl.BlockSpec(memory_space=pl.ANY),
                pl.BlockSpec((1, 1, D, F2), wmap),
                pl.BlockSpec((1, 1, 1, F2), wmap),
                pl.BlockSpec((1, 1, F, D), wmap),
                pl.BlockSpec((1, 1, 1, D), wmap),
            ],
            out_specs=pl.BlockSpec((tm * SUBLANES, LANES), lambda i, be, nu: (i, 0)),
            scratch_shapes=[pltpu.VMEM((2, tm * SUBLANES, LANES), F32), pltpu.SemaphoreType.DMA((2,)),
                            pltpu.VMEM((D, F2), BF16), pltpu.VMEM((D // LANES, F, LANES), F32),
                            pltpu.VMEM((F, D), BF16),
                            pltpu.VMEM((tm, F), BF16)],
        ),
        compiler_params=_params("arbitrary"),
        name="moe_experts",
    )(block_expert, n_used, rows, rows, hp, w1, b1, w2, b2)


def _combine_kernel(dest_ref, dnext_ref, y_hbm, pt_ref, x_ref, mod_ref, o_ref, ybuf, sem, *, tm):
    n = TOP_K * tm
    i = pl.program_id(0)
    slot = i % 2

    def gather(idx_ref, s):
        def issue(r, carry):
            src = pl.multiple_of(idx_ref[0, 0, r] * SUBLANES, SUBLANES)
            dst = pl.multiple_of(r * SUBLANES, SUBLANES)
            pltpu.make_async_copy(y_hbm.at[pl.ds(src, SUBLANES), :], ybuf.at[s, pl.ds(dst, SUBLANES), :],
                                  sem.at[s]).start()
            return carry

        lax.fori_loop(0, n, issue, 0, unroll=8)

    @pl.when(i == 0)
    def _():
        gather(dest_ref, 0)

    @pl.when(i + 1 < pl.num_programs(0))
    def _():
        gather(dnext_ref, 1 - slot)

    pltpu.make_async_copy(y_hbm.at[pl.ds(0, n * SUBLANES), :], ybuf.at[slot], sem.at[slot]).wait()

    rows = ybuf.at[slot]
    for c in range(SUBLANES):
        cs = slice(LANES * c, LANES * (c + 1))
        acc = rows[pl.ds(c, tm, stride=SUBLANES), :] * pt_ref[:, 0:1]
        for k in range(1, TOP_K):
            acc = acc + rows[pl.ds(k * tm * SUBLANES + c, tm, stride=SUBLANES), :] * pt_ref[:, k:k + 1]
        o_ref[:, cs] = x_ref[:, cs] + mod_ref[0, 5:6, cs] * acc


def _combine(dest, y, pt, x2, mod, seq):
    T, D = x2.shape
    tm = min(COMB_TM, seq)
    per_b = seq // tm
    nb = T // tm
    rows = _tile_rows(dest, tm)
    return pl.pallas_call(
        functools.partial(_combine_kernel, tm=tm),
        out_shape=jax.ShapeDtypeStruct((T, D), F32),
        grid=(nb,),
        in_specs=[
            pl.BlockSpec((1, 1, TOP_K * tm), lambda i: (i, 0, 0), memory_space=pltpu.SMEM),
            pl.BlockSpec((1, 1, TOP_K * tm), lambda i: (jnp.minimum(i + 1, nb - 1), 0, 0),
                         memory_space=pltpu.SMEM),
            pl.BlockSpec(memory_space=pl.ANY),
            pl.BlockSpec((tm, LANES), lambda i: (i, 0)),
            pl.BlockSpec((tm, D), lambda i: (i, 0)),
            pl.BlockSpec((1, 6, D), lambda i: (i // per_b, 0, 0)),
        ],
        out_specs=pl.BlockSpec((tm, D), lambda i: (i, 0)),
        scratch_shapes=[pltpu.VMEM((2, TOP_K * tm * SUBLANES, LANES), F32), pltpu.SemaphoreType.DMA((2,))],
        compiler_params=_params("arbitrary"),
        name="moe_combine",
    )(rows, rows, y, pt, x2, mod)


def _t5_bucket(dist):
    exact = REL_BUCKETS // 2
    log_ratio = jnp.log(jnp.maximum(dist, 1).astype(F32) / exact) / math.log(REL_MAX_DIST / exact)
    large = exact + (log_ratio * (REL_BUCKETS - exact)).astype(jnp.int32)
    return jnp.where(dist < exact, dist, jnp.minimum(large, REL_BUCKETS - 1))


def _diff_bias_tiles(rel_bias, tb):
    table = rel_bias[:, :DIFF_HEADS].astype(F32)
    shifted = (table - table[REL_BUCKETS - 1]) * LOG2E
    kk = jnp.arange(tb)[:, None]
    qq = jnp.arange(tb)[None, :]

    def tile(dist):
        return _bucket_lookup(shifted, _t5_bucket(jnp.maximum(dist, 0)))

    diag = jnp.where((qq - kk >= 0)[None], tile(qq - kk), NEG_INF)
    prev = tile(qq + tb - kk)
    return prev, diag


def _bucket_lookup(table, bucket):
    out = jnp.zeros((table.shape[1],) + bucket.shape, F32)
    for b in range(table.shape[0]):
        out = out + jnp.where(bucket[None] == b, table[b].reshape((-1,) + (1,) * bucket.ndim), 0.0)
    return out


def _swa_bias_tiles(rel_bias):
    qi = jnp.arange(WINDOW)[:, None]
    kj = jnp.arange(2 * WINDOW)[None, :]
    dist = WINDOW + qi - kj
    ok = (dist >= 0) & (dist < WINDOW)
    bias = _bucket_lookup(rel_bias[:, DIFF_HEADS:].astype(F32),
                          _t5_bucket(jnp.clip(dist, 0, WINDOW - 1)))
    bias = jnp.where(ok[None], bias, NEG_INF)
    bias = bias.reshape(SWA_KV_HEADS, SWA_GROUP * WINDOW, 2 * WINDOW)
    return bias[..., :WINDOW], bias[..., WINDOW:]


def _block_diag(w):
    nb, n, _ = w.shape
    eye = jnp.eye(nb, dtype=w.dtype)
    return (eye[:, None, :, None] * w[:, :, None, :]).reshape(nb * n, nb * n)


def _routing(counts, top_e, rank, tm):
    K, T = top_e.shape
    counts = counts.astype(jnp.int32)
    padded = (counts + tm - 1) // tm * tm
    pad_ends = jnp.cumsum(padded)
    pad_starts = pad_ends - padded
    onehot = top_e[..., None] == jnp.arange(N_EXPERTS, dtype=jnp.int32)
    dest = rank + jnp.sum(jnp.where(onehot, pad_starts, 0), axis=-1)
    n_rows = K * T + N_EXPERTS * tm
    nb = n_rows // tm
    n_used = pad_ends[-1] // tm
    blk = jnp.minimum(jnp.arange(nb), n_used - 1) * tm
    block_expert = jnp.sum(blk[:, None] >= pad_ends[None, :], axis=-1)
    fill_ends = jnp.cumsum(padded - counts)
    fill_expert = jnp.sum(jnp.arange(N_EXPERTS * tm)[:, None] >= fill_ends[None, :], axis=-1)
    keys = jnp.concatenate([(top_e * T + jnp.arange(T, dtype=jnp.int32)[None, :]).reshape(-1),
                            (fill_expert * T + (T - 1)).astype(jnp.int32)])
    row_token = jnp.sort(keys) % T
    return (block_expert.astype(jnp.int32), n_used.astype(jnp.int32).reshape(1), dest.astype(jnp.int32),
            row_token.astype(jnp.int32))


def kernel(x, c, w_ada, b_ada, norm_mix, norm_ffn, w_in, conv_w, conv_b, lru_wa, lru_ba, lru_wx, lru_bx,
           lru_lambda, diff_qnorm, diff_knorm, diff_lambda, diff_subln, swa_qnorm, swa_knorm, swa_sinks,
           rel_bias, w_branch, w_out, w_router, b_router, w1, b1, w2, b2):
    B, S, D = x.shape
    L = w_ada.shape[0]
    T = B * S
    tb = min(ATT_TB, S)

    mods = _adaln(c, w_ada, b_ada)
    bprev, bdiag = _diff_bias_tiles(rel_bias, tb)
    sbias_prev, sbias_cur = _swa_bias_tiles(rel_bias)
    gidx = jnp.arange(512) // DIFF_HD
    gmat = jnp.where(gidx[:, None] == gidx[None, :], 1.0 / DIFF_HD, 0.0).astype(BF16)

    x2 = x.reshape(T, D)
    for l in range(L):
        lambda_init = 0.8 - 0.6 * math.exp(-0.3 * l)
        mod = mods[l]
        qkg = jnp.stack([
            jnp.tile(diff_qnorm[l], 8) * (DIFF_HD ** -0.5 * LOG2E),
            jnp.tile(diff_knorm[l], 8),
            jnp.tile(swa_qnorm[l], 8) * SWA_HD ** -0.5,
            jnp.tile(swa_knorm[l], 8),
        ]).astype(F32)
        w_in_l = w_in[l].astype(BF16)
        xg, q1, q2, dk, dvt, sq, sk, sv, gl = _inproj(
            x2, mod, norm_mix[l].reshape(1, D), w_in_l, w_in_l[:, 2048:2560].T, gmat, qkg, S)

        wg = jnp.concatenate([_block_diag(lru_wa[l]), _block_diag(lru_wx[l])], axis=1).astype(BF16)
        bg = jnp.concatenate([lru_ba[l], lru_bx[l]]).reshape(1, 2 * LRU_WIDTH)
        sp = jax.nn.softplus(-lru_lambda[l].astype(F32)).reshape(1, LRU_WIDTH)
        o_lru = _rglru(xg, conv_w[l], conv_b[l].reshape(1, LRU_WIDTH), wg, bg, sp, B, S)

        o_diff = _diff_attn(q1, q2, dk, dvt, bprev, bdiag, diff_lambda[l],
                            diff_subln[l].reshape(2 * DIFF_HD, 1), lambda_init, B, S)

        sinks = jnp.repeat(swa_sinks[l].astype(F32).reshape(SWA_KV_HEADS, SWA_GROUP), WINDOW, axis=1)
        o_swa = _swa(sq, sk, sv, sbias_prev, sbias_cur, sinks.reshape(SWA_KV_HEADS, -1, 1), B, S)

        x2 = _merge(x2, mod, o_lru, o_diff, o_swa, gl, w_branch[l].astype(BF16), w_out[l].astype(BF16), S)

        hp, top_e, pt, rank, counts = _router(x2, mod, norm_ffn[l].reshape(1, D), w_router[l].T,
                                              b_router[l].reshape(N_EXPERTS, 1), S)
        block_expert, n_used, dest, row_token = _routing(counts[:, 0], top_e, rank, MOE_TM)
        y = _experts(l, block_expert, n_used, row_token, hp, w1, b1[:, :, None, :], w2, b2[:, :, None, :])
        x2 = _combine(dest, y, pt, x2, mod, S)
    return x2.reshape(B, S, D)
```

```python
import functools
import math

import jax
import jax.numpy as jnp
from jax import lax
from jax.experimental import pallas as pl
from jax.experimental.pallas import tpu as pltpu

F32 = jnp.float32
BF16 = jnp.bfloat16

LRU_WIDTH = 512
LRU_BLOCKS = 8
LRU_C = 8.0
CONV_W = 4
DIFF_HEADS = 4
DIFF_HD = 64
SWA_HEADS = 8
SWA_KV_HEADS = 2
SWA_GROUP = SWA_HEADS // SWA_KV_HEADS
SWA_HD = 64
WINDOW = 128
N_BRANCHES = 3
BRANCH_WIDTH = 512
REL_BUCKETS = 32
REL_MAX_DIST = 128
N_EXPERTS = 32
TOP_K = 4
SWIGLU_LIMIT = 7.0
SWIGLU_ALPHA = 1.702
NORM_EPS = 1e-6
NEG_INF = -1e30
LOG2E = math.log2(math.e)

VMEM_LIMIT_BYTES = 56 * 1024 * 1024
LANES = 128
SUBLANES = 8

ADALN_TN = 1536
PROJ_TM = 512
LRU_TS = 512
ATT_TB = 512
SWA_TS = 512
MERGE_TM = 512
ROUTER_TM = 512
MOE_TM = 512
COMB_TM = 128

HIGHEST = lax.Precision.HIGHEST


def _params(*sem):
    return pltpu.CompilerParams(dimension_semantics=sem, vmem_limit_bytes=VMEM_LIMIT_BYTES)


def _dot(a, b, **kw):
    return jnp.dot(a, b, preferred_element_type=F32, **kw)


def _dot_nt(a, b, **kw):
    return lax.dot_general(a, b, (((1,), (1,)), ((), ())), preferred_element_type=F32, **kw)


def _sigmoid(x):
    return 1.0 / (1.0 + jnp.exp(-x))


def _adaln_kernel(c_ref, w_ref, b_ref, o_ref):
    c = c_ref[...]
    cond = c * _sigmoid(c)
    o_ref[0] = _dot(cond, w_ref[0], precision=HIGHEST) + b_ref[0]


def _adaln(c, w_ada, b_ada):
    L, D, N = w_ada.shape
    B = c.shape[0]
    rows = 8
    cp = jnp.zeros((rows, D), F32).at[:B].set(c)
    out = pl.pallas_call(
        _adaln_kernel,
        out_shape=jax.ShapeDtypeStruct((L, rows, N), F32),
        grid=(L, N // ADALN_TN),
        in_specs=[
            pl.BlockSpec((rows, D), lambda l, j: (0, 0)),
            pl.BlockSpec((1, D, ADALN_TN), lambda l, j: (l, 0, j)),
            pl.BlockSpec((1, 1, ADALN_TN), lambda l, j: (l, 0, j)),
        ],
        out_specs=pl.BlockSpec((1, rows, ADALN_TN), lambda l, j: (l, 0, j)),
        compiler_params=_params("arbitrary", "arbitrary"),
        name="adaln",
    )(cp, w_ada, b_ada.reshape(L, 1, N))
    return out[:, :B].reshape(L, B, 6, D)


def _rms_mod(x, gain, scale, shift):
    ms = jnp.mean(x * x, axis=-1, keepdims=True)
    return (x * lax.rsqrt(ms + NORM_EPS) * gain) * (1.0 + scale) + shift


def _group_rms(x, gmat):
    sq = x * x
    hi = sq.astype(BF16)
    lo = (sq - hi.astype(F32)).astype(BF16)
    ms = _dot(hi, gmat) + _dot(lo, gmat)
    return x * lax.rsqrt(ms + NORM_EPS)


def _inproj_kernel(x_ref, mod_ref, gain_ref, w_ref, wvt_ref, gmat_ref, qkg_ref,
                   xg_ref, q1_ref, q2_ref, dk_ref, dvt_ref, sq_ref, sk_ref, sv_ref, gl_ref):
    x = x_ref[...]
    h = _rms_mod(x, gain_ref[...], mod_ref[0, 1:2, :], mod_ref[0, 0:1, :]).astype(BF16)
    gmat = gmat_ref[...]

    xg_ref[...] = _dot(h, w_ref[:, 0:1024]).astype(BF16)

    dd = _dot(h, w_ref[:, 1024:2048])
    qn = _group_rms(dd[:, 0:512], gmat) * qkg_ref[0:1, :]
    kn = _group_rms(dd[:, 512:1024], gmat) * qkg_ref[1:2, :]
    lane = lax.broadcasted_iota(jnp.int32, qn.shape, 1) % LANES
    q1_ref[...] = jnp.where(lane < DIFF_HD, qn, 0.0).astype(BF16)
    q2_ref[...] = jnp.where(lane >= DIFF_HD, qn, 0.0).astype(BF16)
    dk_ref[...] = kn.astype(BF16)
    dvt_ref[0] = _dot_nt(wvt_ref[...], h).astype(BF16)

    ss = _dot(h, w_ref[:, 2560:3328])
    sq_ref[...] = (_group_rms(ss[:, 0:512], gmat) * qkg_ref[2:3, :]).astype(BF16)
    sk_ref[...] = (_group_rms(ss[:, 512:640], gmat[0:128, 0:128]) * qkg_ref[3:4, 0:128]).astype(BF16)
    sv_ref[...] = ss[:, 640:768].astype(BF16)

    gl_ref[...] = _dot(h, w_ref[:, 3328:6400]).astype(BF16)


def _inproj(x2, mod, gain, w_in, wvt, gmat, qkg, seq):
    T, D = x2.shape
    tm = min(PROJ_TM, seq)
    per_b = seq // tm
    VW = wvt.shape[0]
    widths = (1024, 512, 512, 512, None, 512, 128, 128, 3072)
    row = lambda i: (i, 0)
    const = lambda i: (0, 0)
    vt_shape = jax.ShapeDtypeStruct((T // seq, VW, seq), BF16)
    vt_spec = pl.BlockSpec((1, VW, tm), lambda i: (i // per_b, 0, i % per_b))
    return pl.pallas_call(
        _inproj_kernel,
        out_shape=[vt_shape if w is None else jax.ShapeDtypeStruct((T, w), BF16) for w in widths],
        grid=(T // tm,),
        in_specs=[
            pl.BlockSpec((tm, D), row),
            pl.BlockSpec((1, 6, D), lambda i: (i // per_b, 0, 0)),
            pl.BlockSpec((1, D), const),
            pl.BlockSpec(w_in.shape, const, pipeline_mode=pl.Buffered(1)),
            pl.BlockSpec(wvt.shape, const),
            pl.BlockSpec(gmat.shape, const),
            pl.BlockSpec(qkg.shape, const),
        ],
        out_specs=[vt_spec if w is None else pl.BlockSpec((tm, w), row) for w in widths],
        compiler_params=_params("arbitrary"),
        name="inproj",
    )(x2, mod, gain, w_in, wvt, gmat, qkg)


def _rglru_kernel(xg_ref, cw_ref, cb_ref, wg_ref, bg_ref, sp_ref, o_ref, ext_ref, hc_ref):
    ts = xg_ref.shape[0]
    C = LRU_WIDTH

    @pl.when(pl.program_id(1) == 0)
    def _():
        ext_ref[0:8, :] = jnp.zeros((8, C), F32)
        hc_ref[...] = jnp.zeros_like(hc_ref)

    xr = xg_ref[:, 0:C].astype(F32)
    ext_ref[8:8 + ts, :] = xr
    xc = cb_ref[...] + xr * cw_ref[CONV_W - 1:CONV_W, :]
    for back in range(1, CONV_W):
        tap = CONV_W - 1 - back
        xc = xc + ext_ref[8 - back:8 - back + ts, :] * cw_ref[tap:tap + 1, :]
    ext_ref[0:8, :] = xr[ts - 8:ts, :]

    gates = _dot(xc.astype(BF16), wg_ref[...]) + bg_ref[...]
    r = _sigmoid(gates[:, 0:C])
    gi = _sigmoid(gates[:, C:2 * C])
    log_a = (-LRU_C) * r * sp_ref[...]
    a = jnp.exp(log_a)
    b = xc * gi * jnp.sqrt(1.0 - a * a)

    rows = lax.broadcasted_iota(jnp.int32, (ts, C), 0)
    d = 1
    while d < ts:
        keep = rows >= d
        a_sh = pltpu.roll(a, d, axis=0)
        b_sh = pltpu.roll(b, d, axis=0)
        b = jnp.where(keep, a * b_sh + b, b)
        a = jnp.where(keep, a * a_sh, a)
        d *= 2
    h = b + a * hc_ref[...]
    hc_ref[...] = h[ts - 1:ts, :]

    gr = xg_ref[:, C:2 * C].astype(F32)
    gelu = 0.5 * gr * (1.0 + jnp.tanh(math.sqrt(2.0 / math.pi) * (gr + 0.044715 * gr * gr * gr)))
    o_ref[...] = (h * gelu).astype(BF16)


def _rglru(xg, conv_w, conv_b, wg, bg, softplus_neg_lam, batch, seq):
    T = xg.shape[0]
    C = LRU_WIDTH
    ts = min(LRU_TS, seq)
    per_b = seq // ts
    const = lambda b, i: (0, 0)
    return pl.pallas_call(
        _rglru_kernel,
        out_shape=jax.ShapeDtypeStruct((T, C), BF16),
        grid=(batch, per_b),
        in_specs=[
            pl.BlockSpec((ts, 2 * C), lambda b, i: (b * per_b + i, 0)),
            pl.BlockSpec((CONV_W, C), const),
            pl.BlockSpec((1, C), const),
            pl.BlockSpec((C, 2 * C), const),
            pl.BlockSpec((1, 2 * C), const),
            pl.BlockSpec((1, C), const),
        ],
        out_specs=pl.BlockSpec((ts, C), lambda b, i: (b * per_b + i, 0)),
        scratch_shapes=[pltpu.VMEM((ts + 8, C), F32), pltpu.VMEM((1, C), F32)],
        compiler_params=_params("arbitrary", "arbitrary"),
        name="rglru",
    )(xg, conv_w, conv_b, wg, bg, softplus_neg_lam)


def _diff_attn_kernel(q1_ref, q2_ref, k_ref, vt_ref, bprev_ref, bdiag_ref, lam_ref, sub_ref, o_ref,
                      m_ref, l_ref, acc_ref, *, tb, lambda_init):
    i = pl.program_id(2)
    q = (q1_ref[0], q2_ref[0])

    m_ref[...] = jnp.full(m_ref.shape, NEG_INF, F32)
    l_ref[...] = jnp.zeros(l_ref.shape, F32)
    acc_ref[...] = jnp.zeros(acc_ref.shape, F32)

    def attend(start, tk, bias):
        kb = k_ref[0, pl.ds(start, tk), :]
        vb = vt_ref[0, :, pl.ds(start, tk)]
        scores = [_dot_nt(kb, q[mp]) for mp in range(2)]
        for mp in range(2):
            s = scores[mp]
            if bias is not None:
                s = s + bias
            m_old = m_ref[mp]
            m_new = jnp.maximum(m_old, jnp.max(s, axis=0, keepdims=True))
            alpha = jnp.exp2(m_old - m_new)
            p = jnp.exp2(s - m_new)
            l_ref[mp] = alpha * l_ref[mp] + jnp.sum(p, axis=0, keepdims=True)
            acc_ref[mp] = alpha * acc_ref[mp] + _dot(vb, p.astype(BF16))
            m_ref[mp] = m_new

    n_far = jnp.maximum(i - 1, 0)

    def far(j, carry):
        attend(pl.multiple_of(j * (2 * tb), 2 * tb), 2 * tb, None)
        return carry

    lax.fori_loop(0, n_far // 2, far, 0)

    @pl.when(n_far % 2 == 1)
    def _():
        attend(pl.multiple_of((n_far - 1) * tb, tb), tb, None)

    @pl.when(i > 0)
    def _():
        attend(pl.multiple_of((i - 1) * tb, tb), tb, bprev_ref[0])

    attend(pl.multiple_of(i * tb, tb), tb, bdiag_ref[0])

    lv = lam_ref[...]
    lam = (jnp.exp(jnp.sum(lv[0:1] * lv[1:2], axis=-1, keepdims=True))
           - jnp.exp(jnp.sum(lv[2:3] * lv[3:4], axis=-1, keepdims=True)) + lambda_init)
    o = acc_ref[0] / l_ref[0] - lam * (acc_ref[1] / l_ref[1])
    ms = jnp.mean(o * o, axis=0, keepdims=True)
    o = o * lax.rsqrt(ms + NORM_EPS) * sub_ref[...] * (1.0 - lambda_init)
    o_ref[0] = o.T.astype(BF16)


def _diff_attn(q1, q2, k, vt, bprev, bdiag, lam_vecs, subln, lambda_init, batch, seq):
    W = DIFF_HEADS * 2 * DIFF_HD
    tb = min(ATT_TB, seq)
    shp = (batch, seq, W)
    qspec = pl.BlockSpec((1, tb, LANES), lambda b, h, i: (b, i, h))
    kspec = pl.BlockSpec((1, seq, LANES), lambda b, h, i: (b, 0, h))
    vspec = pl.BlockSpec((1, LANES, seq), lambda b, h, i: (b, h, 0))
    bspec = pl.BlockSpec((1, tb, tb), lambda b, h, i: (h, 0, 0))
    const = lambda b, h, i: (0, 0)
    out = pl.pallas_call(
        functools.partial(_diff_attn_kernel, tb=tb, lambda_init=lambda_init),
        out_shape=jax.ShapeDtypeStruct(shp, BF16),
        grid=(batch, DIFF_HEADS, seq // tb),
        in_specs=[qspec, qspec, kspec, vspec, bspec, bspec,
                  pl.BlockSpec((4, DIFF_HD), const), pl.BlockSpec((2 * DIFF_HD, 1), const)],
        out_specs=qspec,
        scratch_shapes=[pltpu.VMEM((2, 1, tb), F32), pltpu.VMEM((2, 1, tb), F32),
                        pltpu.VMEM((2, LANES, tb), F32)],
        compiler_params=_params("arbitrary", "arbitrary", "arbitrary"),
        name="diff_attn",
    )(q1.reshape(shp), q2.reshape(shp), k.reshape(shp), vt, bprev, bdiag, lam_vecs, subln)
    return out.reshape(batch * seq, W)


def _swa_kernel(q_ref, k_ref, v_ref, bp_ref, bc_ref, sink_ref, o_ref, *, ts):
    i = pl.program_id(1)
    blk = WINDOW
    for sub in range(ts // blk):
        start = i * ts + sub * blk
        has_prev = start > 0
        pstart = pl.multiple_of(jnp.maximum(start - blk, 0), blk)
        cstart = pl.multiple_of(start, blk)
        kp = k_ref[0, pl.ds(pstart, blk), :]
        kc = k_ref[0, pl.ds(cstart, blk), :]
        vp = v_ref[0, pl.ds(pstart, blk), :]
        vc = v_ref[0, pl.ds(cstart, blk), :]
        qs = q_ref[0, sub * blk:(sub + 1) * blk, :]
        outs = []
        for hk in range(SWA_KV_HEADS):
            c0 = hk * SWA_GROUP * SWA_HD
            qh = jnp.concatenate(
                [qs[:, c0 + g * SWA_HD:c0 + (g + 1) * SWA_HD] for g in range(SWA_GROUP)], axis=0)
            ksl = slice(hk * SWA_HD, (hk + 1) * SWA_HD)
            s_p = _dot_nt(qh, kp[:, ksl]) + bp_ref[hk]
            s_p = jnp.where(has_prev, s_p, NEG_INF)
            s_c = _dot_nt(qh, kc[:, ksl]) + bc_ref[hk]
            sink = sink_ref[hk]
            m = jnp.maximum(jnp.maximum(jnp.max(s_p, axis=-1, keepdims=True),
                                        jnp.max(s_c, axis=-1, keepdims=True)), sink)
            p_p = jnp.exp(s_p - m)
            p_c = jnp.exp(s_c - m)
            den = (jnp.sum(p_p, axis=-1, keepdims=True) + jnp.sum(p_c, axis=-1, keepdims=True)
                   + jnp.exp(sink - m))
            o = _dot(p_p.astype(BF16), vp[:, ksl]) + _dot(p_c.astype(BF16), vc[:, ksl])
            o = o / den
            outs.extend(o[g * blk:(g + 1) * blk, :] for g in range(SWA_GROUP))
        o_ref[0, sub * blk:(sub + 1) * blk, :] = jnp.concatenate(outs, axis=1).astype(BF16)


def _swa(q, k, v, bias_prev, bias_cur, sinks, batch, seq):
    ts = min(SWA_TS, seq)
    WQ = SWA_HEADS * SWA_HD
    WK = SWA_KV_HEADS * SWA_HD
    rows = SWA_GROUP * WINDOW
    const3 = lambda b, i: (0, 0, 0)
    out = pl.pallas_call(
        functools.partial(_swa_kernel, ts=ts),
        out_shape=jax.ShapeDtypeStruct((batch, seq, WQ), BF16),
        grid=(batch, seq // ts),
        in_specs=[
            pl.BlockSpec((1, ts, WQ), lambda b, i: (b, i, 0)),
            pl.BlockSpec((1, seq, WK), lambda b, i: (b, 0, 0)),
            pl.BlockSpec((1, seq, WK), lambda b, i: (b, 0, 0)),
            pl.BlockSpec((SWA_KV_HEADS, rows, WINDOW), const3),
            pl.BlockSpec((SWA_KV_HEADS, rows, WINDOW), const3),
            pl.BlockSpec((SWA_KV_HEADS, rows, 1), const3),
        ],
        out_specs=pl.BlockSpec((1, ts, WQ), lambda b, i: (b, i, 0)),
        compiler_params=_params("arbitrary", "arbitrary"),
        name="swa_attn",
    )(q.reshape(batch, seq, WQ), k.reshape(batch, seq, WK), v.reshape(batch, seq, WK),
      bias_prev, bias_cur, sinks)
    return out.reshape(batch * seq, WQ)


def _merge_kernel(x_ref, mod_ref, lru_ref, diff_ref, swa_ref, gl_ref, wb_ref, wo_ref, o_ref):
    D = x_ref.shape[1]
    merged = None
    for n, br in enumerate((lru_ref, diff_ref, swa_ref)):
        gate = _sigmoid(gl_ref[:, n * D:(n + 1) * D].astype(F32))
        term = gate * _dot(br[...], wb_ref[n])
        merged = term if merged is None else merged + term
    out = _dot(merged.astype(BF16), wo_ref[...])
    o_ref[...] = x_ref[...] + mod_ref[0, 2:3, :] * out


def _merge(x2, mod, o_lru, o_diff, o_swa, gl, w_branch, w_out, seq):
    T, D = x2.shape
    tm = min(MERGE_TM, seq)
    per_b = seq // tm
    row = lambda i: (i, 0)
    return pl.pallas_call(
        _merge_kernel,
        out_shape=jax.ShapeDtypeStruct((T, D), F32),
        grid=(T // tm,),
        in_specs=[
            pl.BlockSpec((tm, D), row),
            pl.BlockSpec((1, 6, D), lambda i: (i // per_b, 0, 0)),
            pl.BlockSpec((tm, BRANCH_WIDTH), row),
            pl.BlockSpec((tm, BRANCH_WIDTH), row),
            pl.BlockSpec((tm, BRANCH_WIDTH), row),
            pl.BlockSpec((tm, N_BRANCHES * D), row),
            pl.BlockSpec(w_branch.shape, lambda i: (0, 0, 0)),
            pl.BlockSpec(w_out.shape, lambda i: (0, 0)),
        ],
        out_specs=pl.BlockSpec((tm, D), row),
        compiler_params=_params("arbitrary"),
        name="merge_outproj",
    )(x2, mod, o_lru, o_diff, o_swa, gl, w_branch, w_out)


def _router_kernel(x_ref, mod_ref, gain_ref, wr_ref, br_ref, tri_ref,
                   hp_ref, e_ref, pt_ref, rank_ref, cnt_ref, run_ref):
    @pl.when(pl.program_id(0) == 0)
    def _():
        run_ref[...] = jnp.zeros_like(run_ref)

    x = x_ref[...]
    tm, D = x.shape
    h = _rms_mod(x, gain_ref[...], mod_ref[0, 4:5, :], mod_ref[0, 3:4, :])
    for c in range(D // LANES):
        hp_ref[pl.ds(c, tm, stride=SUBLANES), :] = h[:, LANES * c:LANES * (c + 1)]

    logits = _dot_nt(wr_ref[...], h, precision=HIGHEST) + br_ref[...]
    eidx = lax.broadcasted_iota(jnp.int32, logits.shape, 0)
    vals, idxs, hots = [], [], []
    for _ in range(TOP_K):
        mx = jnp.max(logits, axis=0, keepdims=True)
        sel = jnp.min(jnp.where(logits == mx, eidx, N_EXPERTS), axis=0, keepdims=True)
        hot = eidx == sel
        vals.append(mx)
        idxs.append(sel)
        hots.append(hot)
        logits = jnp.where(hot, -jnp.inf, logits)
    e_ref[...] = jnp.concatenate(idxs, axis=0)
    top_v = jnp.concatenate(vals, axis=0)
    ex = jnp.exp(top_v - top_v[0:1])
    p = ex / jnp.sum(ex, axis=0, keepdims=True)
    pt_ref[...] = jnp.concatenate([p, jnp.zeros((LANES - TOP_K, tm), F32)], axis=0).T

    member = hots[0]
    for hot in hots[1:]:
        member = member | hot
    member = jnp.where(member, 1.0, 0.0)
    before = _dot(member.astype(BF16), tri_ref[...]) + run_ref[:, 0:1]
    ranks = [jnp.sum(jnp.where(hot, before, 0.0), axis=0, keepdims=True) for hot in hots]
    rank_ref[...] = jnp.concatenate(ranks, axis=0).astype(jnp.int32)
    run_ref[...] = run_ref[...] + jnp.sum(member, axis=1, keepdims=True)
    cnt_ref[...] = run_ref[...]


def _router(x2, mod, gain, w_router_t, b_router, seq):
    T, D = x2.shape
    tm = min(ROUTER_TM, seq)
    per_b = seq // tm
    tri = (jnp.arange(tm)[:, None] < jnp.arange(tm)[None, :]).astype(BF16)
    const = lambda i: (0, 0)
    return pl.pallas_call(
        _router_kernel,
        out_shape=[jax.ShapeDtypeStruct((T * SUBLANES, LANES), F32),
                   jax.ShapeDtypeStruct((TOP_K, T), jnp.int32),
                   jax.ShapeDtypeStruct((T, LANES), F32),
                   jax.ShapeDtypeStruct((TOP_K, T), jnp.int32),
                   jax.ShapeDtypeStruct((N_EXPERTS, LANES), F32)],
        grid=(T // tm,),
        in_specs=[
            pl.BlockSpec((tm, D), lambda i: (i, 0)),
            pl.BlockSpec((1, 6, D), lambda i: (i // per_b, 0, 0)),
            pl.BlockSpec((1, D), const),
            pl.BlockSpec((N_EXPERTS, D), const),
            pl.BlockSpec((N_EXPERTS, 1), const),
            pl.BlockSpec((tm, tm), const),
        ],
        out_specs=[pl.BlockSpec((tm * SUBLANES, LANES), lambda i: (i, 0)),
                   pl.BlockSpec((TOP_K, tm), lambda i: (0, i)),
                   pl.BlockSpec((tm, LANES), lambda i: (i, 0)),
                   pl.BlockSpec((TOP_K, tm), lambda i: (0, i)),
                   pl.BlockSpec((N_EXPERTS, LANES), const)],
        scratch_shapes=[pltpu.VMEM((N_EXPERTS, LANES), F32)],
        compiler_params=_params("arbitrary"),
        name="router",
    )(x2, mod, gain, w_router_t, b_router, tri)


def _tile_rows(idx, tm):
    K, T = idx.shape
    nb = T // tm
    return idx.reshape(K, nb, tm).transpose(1, 0, 2).reshape(nb, 1, K * tm)


def _expert_kernel(be_ref, nused_ref, tok_ref, tnext_ref, hp_hbm, w1_ref, b1_ref, w2_ref, b2_ref, y_ref,
                   xbuf, sem, w1b_ref, w2s_ref, w2b_ref, act_ref, *, tm):
    i = pl.program_id(0)
    slot = i % 2
    used = i < nused_ref[0]
    chunks = w1_ref.shape[3] // (2 * LANES)
    hl = LANES // 2

    def fetch(idx_ref, s, r):
        src = pl.multiple_of(idx_ref[0, 0, r] * SUBLANES, SUBLANES)
        dst = pl.multiple_of(r * SUBLANES, SUBLANES)
        pltpu.make_async_copy(hp_hbm.at[pl.ds(src, SUBLANES), :], xbuf.at[s, pl.ds(dst, SUBLANES), :],
                              sem.at[s]).start()

    def wait_rows(s):
        pltpu.make_async_copy(hp_hbm.at[pl.ds(0, tm * SUBLANES), :], xbuf.at[s], sem.at[s]).wait()

    @pl.when((i == 0) & used)
    def _():
        def issue(r, carry):
            fetch(tok_ref, 0, r)
            return carry

        lax.fori_loop(0, tm, issue, 0, unroll=8)

    @pl.when(used & ((i == 0) | (be_ref[i] != be_ref[jnp.maximum(i - 1, 0)])))
    def _():
        for q in range(chunks):
            cs = slice(2 * LANES * q, 2 * LANES * (q + 1))
            w1b_ref[:, cs] = w1_ref[0, 0, :, cs].astype(BF16)
        cols = w2_ref.shape[3] // LANES
        for q in range(w2_ref.shape[2] // LANES):
            for s in range(2):
                rows = w2_ref[0, 0, LANES * q + hl * s:LANES * q + hl * (s + 1), :]
                for c in range(cols):
                    w2s_ref[c, pl.ds(LANES * q + s, hl, stride=2), :] = rows[:, LANES * c:LANES * (c + 1)]
        for c in range(cols):
            w2b_ref[:, LANES * c:LANES * (c + 1)] = w2s_ref[c].astype(BF16)

    @pl.when(used)
    def _():
        wait_rows(slot)
        xs = xbuf.at[slot]
        x = jnp.concatenate([xs[pl.ds(c, tm, stride=SUBLANES), :] for c in range(SUBLANES)],
                            axis=1).astype(BF16)
        even = lax.broadcasted_iota(jnp.int32, (tm, LANES), 1) % 2 == 0
        per_chunk = tm // chunks
        for q in range(chunks):
            for r in range(per_chunk * q, per_chunk * (q + 1)):
                fetch(tnext_ref, 1 - slot, r)
            cs = slice(2 * LANES * q, 2 * LANES * (q + 1))
            hq = _dot(x, w1b_ref[:, cs]) + b1_ref[0, 0, :, cs]
            lo, hi = hq[:, :LANES], hq[:, LANES:]
            glu = jnp.where(even, lo, pltpu.roll(hi, 1, axis=1))
            lin = jnp.where(even, pltpu.roll(lo, LANES - 1, axis=1), hi)
            glu = jnp.minimum(glu, SWIGLU_LIMIT)
            lin = jnp.clip(lin, -SWIGLU_LIMIT, SWIGLU_LIMIT)
            act = glu * _sigmoid(SWIGLU_ALPHA * glu) * (lin + 1.0)
            act_ref[:, LANES * q:LANES * (q + 1)] = act.astype(BF16)
        y = _dot(act_ref[...], w2b_ref[...]) + b2_ref[0, 0]
        for c in range(SUBLANES):
            y_ref[pl.ds(c, tm, stride=SUBLANES), :] = y[:, LANES * c:LANES * (c + 1)]

    @pl.when((i > 0) & (i == nused_ref[0]))
    def _():
        wait_rows(slot)

    @pl.when(jnp.logical_not(used))
    def _():
        y_ref[...] = jnp.zeros(y_ref.shape, y_ref.dtype)


def _experts(layer, block_expert, n_used, row_token, hp, w1, b1, w2, b2):
    _, E, D, F2 = w1.shape
    assert D == SUBLANES * LANES
    F = F2 // 2
    tm = MOE_TM
    n_rows = row_token.shape[0]
    nb = n_rows // tm
    rows = row_token.reshape(nb, 1, tm)
    wmap = lambda i, be, nu: (layer, be[i], 0, 0)
    return pl.pallas_call(
        functools.partial(_expert_kernel, tm=tm),
        out_shape=jax.ShapeDtypeStruct((n_rows * SUBLANES, LANES), F32),
        grid_spec=pltpu.PrefetchScalarGridSpec(
            num_scalar_prefetch=2,
            grid=(nb,),
            in_specs=[
                pl.BlockSpec((1, 1, tm), lambda i, be, nu: (i, 0, 0), memory_space=pltpu.SMEM),
                pl.BlockSpec((1, 1, tm), lambda i, be, nu: (jnp.minimum(i + 1, nb - 1), 0, 0),
                             memory_space=pltpu.SMEM),
                pl.BlockSpec(memory_space=pl.ANY),
                pl.BlockSpec((1, 1, D, F2), wmap),
                pl.BlockSpec((1, 1, 1, F2), wmap),
                pl.BlockSpec((1, 1, F, D), wmap),
                pl.BlockSpec((1, 1, 1, D), wmap),
            ],
            out_specs=pl.BlockSpec((tm * SUBLANES, LANES), lambda i, be, nu: (i, 0)),
            scratch_shapes=[pltpu.VMEM((2, tm * SUBLANES, LANES), F32), pltpu.SemaphoreType.DMA((2,)),
                            pltpu.VMEM((D, F2), BF16), pltpu.VMEM((D // LANES, F, LANES), F32),
                            pltpu.VMEM((F, D), BF16),
                            pltpu.VMEM((tm, F), BF16)],
        ),
        compiler_params=_params("arbitrary"),
        name="moe_experts",
    )(block_expert, n_used, rows, rows, hp, w1, b1, w2, b2)


def _combine_kernel(dest_ref, dnext_ref, y_hbm, pt_ref, x_ref, mod_ref, o_ref, ybuf, sem, *, tm):
    n = TOP_K * tm
    i = pl.program_id(0)
    slot = i % 2

    def gather(idx_ref, s):
        def issue(r, carry):
            src = pl.multiple_of(idx_ref[0, 0, r] * SUBLANES, SUBLANES)
            dst = pl.multiple_of(r * SUBLANES, SUBLANES)
            pltpu.make_async_copy(y_hbm.at[pl.ds(src, SUBLANES), :], ybuf.at[s, pl.ds(dst, SUBLANES), :],
                                  sem.at[s]).start()
            return carry

        lax.fori_loop(0, n, issue, 0, unroll=8)

    @pl.when(i == 0)
    def _():
        gather(dest_ref, 0)

    @pl.when(i + 1 < pl.num_programs(0))
    def _():
        gather(dnext_ref, 1 - slot)

    pltpu.make_async_copy(y_hbm.at[pl.ds(0, n * SUBLANES), :], ybuf.at[slot], sem.at[slot]).wait()

    rows = ybuf.at[slot]
    for c in range(SUBLANES):
        cs = slice(LANES * c, LANES * (c + 1))
        acc = rows[pl.ds(c, tm, stride=SUBLANES), :] * pt_ref[:, 0:1]
        for k in range(1, TOP_K):
            acc = acc + rows[pl.ds(k * tm * SUBLANES + c, tm, stride=SUBLANES), :] * pt_ref[:, k:k + 1]
        o_ref[:, cs] = x_ref[:, cs] + mod_ref[0, 5:6, cs] * acc


def _combine(dest, y, pt, x2, mod, seq):
    T, D = x2.shape
    tm = min(COMB_TM, seq)
    per_b = seq // tm
    nb = T // tm
    rows = _tile_rows(dest, tm)
    return pl.pallas_call(
        functools.partial(_combine_kernel, tm=tm),
        out_shape=jax.ShapeDtypeStruct((T, D), F32),
        grid=(nb,),
        in_specs=[
            pl.BlockSpec((1, 1, TOP_K * tm), lambda i: (i, 0, 0), memory_space=pltpu.SMEM),
            pl.BlockSpec((1, 1, TOP_K * tm), lambda i: (jnp.minimum(i + 1, nb - 1), 0, 0),
                         memory_space=pltpu.SMEM),
            pl.BlockSpec(memory_space=pl.ANY),
            pl.BlockSpec((tm, LANES), lambda i: (i, 0)),
            pl.BlockSpec((tm, D), lambda i: (i, 0)),
            pl.BlockSpec((1, 6, D), lambda i: (i // per_b, 0, 0)),
        ],
        out_specs=pl.BlockSpec((tm, D), lambda i: (i, 0)),
        scratch_shapes=[pltpu.VMEM((2, TOP_K * tm * SUBLANES, LANES), F32), pltpu.SemaphoreType.DMA((2,))],
        compiler_params=_params("arbitrary"),
        name="moe_combine",
    )(rows, rows, y, pt, x2, mod)


def _t5_bucket(dist):
    exact = REL_BUCKETS // 2
    log_ratio = jnp.log(jnp.maximum(dist, 1).astype(F32) / exact) / math.log(REL_MAX_DIST / exact)
    large = exact + (log_ratio * (REL_BUCKETS - exact)).astype(jnp.int32)
    return jnp.where(dist < exact, dist, jnp.minimum(large, REL_BUCKETS - 1))


def _diff_bias_tiles(rel_bias, tb):
    table = rel_bias[:, :DIFF_HEADS].astype(F32)
    shifted = (table - table[REL_BUCKETS - 1]) * LOG2E
    kk = jnp.arange(tb)[:, None]
    qq = jnp.arange(tb)[None, :]

    def tile(dist):
        return _bucket_lookup(shifted, _t5_bucket(jnp.maximum(dist, 0)))

    diag = jnp.where((qq - kk >= 0)[None], tile(qq - kk), NEG_INF)
    prev = tile(qq + tb - kk)
    return prev, diag


def _bucket_lookup(table, bucket):
    out = jnp.zeros((table.shape[1],) + bucket.shape, F32)
    for b in range(table.shape[0]):
        out = out + jnp.where(bucket[None] == b, table[b].reshape((-1,) + (1,) * bucket.ndim), 0.0)
    return out


def _swa_bias_tiles(rel_bias):
    qi = jnp.arange(WINDOW)[:, None]
    kj = jnp.arange(2 * WINDOW)[None, :]
    dist = WINDOW + qi - kj
    ok = (dist >= 0) & (dist < WINDOW)
    bias = _bucket_lookup(rel_bias[:, DIFF_HEADS:].astype(F32),
                          _t5_bucket(jnp.clip(dist, 0, WINDOW - 1)))
    bias = jnp.where(ok[None], bias, NEG_INF)
    bias = bias.reshape(SWA_KV_HEADS, SWA_GROUP * WINDOW, 2 * WINDOW)
    return bias[..., :WINDOW], bias[..., WINDOW:]


def _block_diag(w):
    nb, n, _ = w.shape
    eye = jnp.eye(nb, dtype=w.dtype)
    return (eye[:, None, :, None] * w[:, :, None, :]).reshape(nb * n, nb * n)


def _routing(counts, top_e, rank, tm):
    K, T = top_e.shape
    counts = counts.astype(jnp.int32)
    padded = (counts + tm - 1) // tm * tm
    pad_ends = jnp.cumsum(padded)
    pad_starts = pad_ends - padded
    onehot = top_e[..., None] == jnp.arange(N_EXPERTS, dtype=jnp.int32)
    dest = rank + jnp.sum(jnp.where(onehot, pad_starts, 0), axis=-1)
    n_rows = K * T + N_EXPERTS * tm
    nb = n_rows // tm
    n_used = pad_ends[-1] // tm
    blk = jnp.minimum(jnp.arange(nb), n_used - 1) * tm
    block_expert = jnp.sum(blk[:, None] >= pad_ends[None, :], axis=-1)
    fill_ends = jnp.cumsum(padded - counts)
    fill_expert = jnp.sum(jnp.arange(N_EXPERTS * tm)[:, None] >= fill_ends[None, :], axis=-1)
    keys = jnp.concatenate([(top_e * T + jnp.arange(T, dtype=jnp.int32)[None, :]).reshape(-1),
                            (fill_expert * T + (T - 1)).astype(jnp.int32)])
    row_token = jnp.sort(keys) % T
    return (block_expert.astype(jnp.int32), n_used.astype(jnp.int32).reshape(1), dest.astype(jnp.int32),
            row_token.astype(jnp.int32))


def kernel(x, c, w_ada, b_ada, norm_mix, norm_ffn, w_in, conv_w, conv_b, lru_wa, lru_ba, lru_wx, lru_bx,
           lru_lambda, diff_qnorm, diff_knorm, diff_lambda, diff_subln, swa_qnorm, swa_knorm, swa_sinks,
           rel_bias, w_branch, w_out, w_router, b_router, w1, b1, w2, b2):
    B, S, D = x.shape
    L = w_ada.shape[0]
    T = B * S
    tb = min(ATT_TB, S)

    mods = _adaln(c, w_ada, b_ada)
    bprev, bdiag = _diff_bias_tiles(rel_bias, tb)
    sbias_prev, sbias_cur = _swa_bias_tiles(rel_bias)
    gidx = jnp.arange(512) // DIFF_HD
    gmat = jnp.where(gidx[:, None] == gidx[None, :], 1.0 / DIFF_HD, 0.0).astype(BF16)

    x2 = x.reshape(T, D)
    for l in range(L):
        lambda_init = 0.8 - 0.6 * math.exp(-0.3 * l)
        mod = mods[l]
        qkg = jnp.stack([
            jnp.tile(diff_qnorm[l], 8) * (DIFF_HD ** -0.5 * LOG2E),
            jnp.tile(diff_knorm[l], 8),
            jnp.tile(swa_qnorm[l], 8) * SWA_HD ** -0.5,
            jnp.tile(swa_knorm[l], 8),
        ]).astype(F32)
        w_in_l = w_in[l].astype(BF16)
        xg, q1, q2, dk, dvt, sq, sk, sv, gl = _inproj(
            x2, mod, norm_mix[l].reshape(1, D), w_in_l, w_in_l[:, 2048:2560].T, gmat, qkg, S)

        wg = jnp.concatenate([_block_diag(lru_wa[l]), _block_diag(lru_wx[l])], axis=1).astype(BF16)
        bg = jnp.concatenate([lru_ba[l], lru_bx[l]]).reshape(1, 2 * LRU_WIDTH)
        sp = jax.nn.softplus(-lru_lambda[l].astype(F32)).reshape(1, LRU_WIDTH)
        o_lru = _rglru(xg, conv_w[l], conv_b[l].reshape(1, LRU_WIDTH), wg, bg, sp, B, S)

        o_diff = _diff_attn(q1, q2, dk, dvt, bprev, bdiag, diff_lambda[l],
                            diff_subln[l].reshape(2 * DIFF_HD, 1), lambda_init, B, S)

        sinks = jnp.repeat(swa_sinks[l].astype(F32).reshape(SWA_KV_HEADS, SWA_GROUP), WINDOW, axis=1)
        o_swa = _swa(sq, sk, sv, sbias_prev, sbias_cur, sinks.reshape(SWA_KV_HEADS, -1, 1), B, S)

        x2 = _merge(x2, mod, o_lru, o_diff, o_swa, gl, w_branch[l].astype(BF16), w_out[l].astype(BF16), S)

        hp, top_e, pt, rank, counts = _router(x2, mod, norm_ffn[l].reshape(1, D), w_router[l].T,
                                              b_router[l].reshape(N_EXPERTS, 1), S)
        block_expert, n_used, dest, row_token = _routing(counts[:, 0], top_e, rank, MOE_TM)
        y = _experts(l, block_expert, n_used, row_token, hp, w1, b1[:, :, None, :], w2, b2[:, :, None, :])
        x2 = _combine(dest, y, pt, x2, mod, S)
    return x2.reshape(B, S, D)
```

```python
import functools
import math

import jax
import jax.numpy as jnp
from jax import lax
from jax.experimental import pallas as pl
from jax.experimental.pallas import tpu as pltpu

F32 = jnp.float32
BF16 = jnp.bfloat16

LRU_WIDTH = 512
LRU_BLOCKS = 8
LRU_C = 8.0
CONV_W = 4
DIFF_HEADS = 4
DIFF_HD = 64
SWA_HEADS = 8
SWA_KV_HEADS = 2
SWA_GROUP = SWA_HEADS // SWA_KV_HEADS
SWA_HD = 64
WINDOW = 128
N_BRANCHES = 3
BRANCH_WIDTH = 512
REL_BUCKETS = 32
REL_MAX_DIST = 128
N_EXPERTS = 32
TOP_K = 4
SWIGLU_LIMIT = 7.0
SWIGLU_ALPHA = 1.702
NORM_EPS = 1e-6
NEG_INF = -1e30
LOG2E = math.log2(math.e)

VMEM_LIMIT_BYTES = 56 * 1024 * 1024
LANES = 128
SUBLANES = 8

ADALN_TN = 1536
PROJ_TM = 512
LRU_TS = 512
ATT_TB = 512
FAR_GROUP = 4
SWA_TS = 512
MERGE_TM = 512
ROUTER_TM = 512
MOE_TM = 512
COMB_TM = 128

HIGHEST = lax.Precision.HIGHEST


def _params(*sem):
    return pltpu.CompilerParams(dimension_semantics=sem, vmem_limit_bytes=VMEM_LIMIT_BYTES)


def _dot(a, b, **kw):
    return jnp.dot(a, b, preferred_element_type=F32, **kw)


def _dot_nt(a, b, **kw):
    return lax.dot_general(a, b, (((1,), (1,)), ((), ())), preferred_element_type=F32, **kw)


def _sigmoid(x):
    return 1.0 / (1.0 + jnp.exp(-x))


def _adaln_kernel(c_ref, w_ref, b_ref, o_ref):
    c = c_ref[...]
    cond = c * _sigmoid(c)
    o_ref[0] = _dot(cond, w_ref[0], precision=HIGHEST) + b_ref[0]


def _adaln(c, w_ada, b_ada):
    L, D, N = w_ada.shape
    B = c.shape[0]
    rows = 8
    cp = jnp.zeros((rows, D), F32).at[:B].set(c)
    out = pl.pallas_call(
        _adaln_kernel,
        out_shape=jax.ShapeDtypeStruct((L, rows, N), F32),
        grid=(L, N // ADALN_TN),
        in_specs=[
            pl.BlockSpec((rows, D), lambda l, j: (0, 0)),
            pl.BlockSpec((1, D, ADALN_TN), lambda l, j: (l, 0, j)),
            pl.BlockSpec((1, 1, ADALN_TN), lambda l, j: (l, 0, j)),
        ],
        out_specs=pl.BlockSpec((1, rows, ADALN_TN), lambda l, j: (l, 0, j)),
        compiler_params=_params("arbitrary", "arbitrary"),
        name="adaln",
    )(cp, w_ada, b_ada.reshape(L, 1, N))
    return out[:, :B].reshape(L, B, 6, D)


def _rms_mod(x, gain, scale, shift):
    ms = jnp.mean(x * x, axis=-1, keepdims=True)
    return (x * lax.rsqrt(ms + NORM_EPS) * gain) * (1.0 + scale) + shift


def _group_rms(x, gmat):
    sq = x * x
    hi = sq.astype(BF16)
    lo = (sq - hi.astype(F32)).astype(BF16)
    ms = _dot(hi, gmat) + _dot(lo, gmat)
    return x * lax.rsqrt(ms + NORM_EPS)


def _inproj_kernel(x_ref, mod_ref, gain_ref, w_ref, wvt_ref, gmat_ref, qkg_ref,
                   xg_ref, q1_ref, q2_ref, dk_ref, dvt_ref, sq_ref, sk_ref, sv_ref, gl_ref):
    x = x_ref[...]
    h = _rms_mod(x, gain_ref[...], mod_ref[0, 1:2, :], mod_ref[0, 0:1, :]).astype(BF16)
    gmat = gmat_ref[...]

    xg_ref[...] = _dot(h, w_ref[:, 0:1024]).astype(BF16)

    dd = _dot(h, w_ref[:, 1024:2048])
    qn = _group_rms(dd[:, 0:512], gmat) * qkg_ref[0:1, :]
    kn = _group_rms(dd[:, 512:1024], gmat) * qkg_ref[1:2, :]
    lane = lax.broadcasted_iota(jnp.int32, qn.shape, 1) % LANES
    q1_ref[...] = jnp.where(lane < DIFF_HD, qn, 0.0).astype(BF16)
    q2_ref[...] = jnp.where(lane >= DIFF_HD, qn, 0.0).astype(BF16)
    dk_ref[...] = kn.astype(BF16)
    dvt_ref[0] = _dot_nt(wvt_ref[...], h).astype(BF16)

    ss = _dot(h, w_ref[:, 2560:3328])
    sq_ref[...] = (_group_rms(ss[:, 0:512], gmat) * qkg_ref[2:3, :]).astype(BF16)
    sk_ref[...] = (_group_rms(ss[:, 512:640], gmat[0:128, 0:128]) * qkg_ref[3:4, 0:128]).astype(BF16)
    sv_ref[...] = ss[:, 640:768].astype(BF16)

    gl_ref[...] = _dot(h, w_ref[:, 3328:6400]).astype(BF16)


def _inproj(x2, mod, gain, w_in, wvt, gmat, qkg, seq):
    T, D = x2.shape
    tm = min(PROJ_TM, seq)
    per_b = seq // tm
    VW = wvt.shape[0]
    widths = (1024, 512, 512, 512, None, 512, 128, 128, 3072)
    row = lambda i: (i, 0)
    const = lambda i: (0, 0)
    vt_shape = jax.ShapeDtypeStruct((T // seq, VW, seq), BF16)
    vt_spec = pl.BlockSpec((1, VW, tm), lambda i: (i // per_b, 0, i % per_b))
    return pl.pallas_call(
        _inproj_kernel,
        out_shape=[vt_shape if w is None else jax.ShapeDtypeStruct((T, w), BF16) for w in widths],
        grid=(T // tm,),
        in_specs=[
            pl.BlockSpec((tm, D), row),
            pl.BlockSpec((1, 6, D), lambda i: (i // per_b, 0, 0)),
            pl.BlockSpec((1, D), const),
            pl.BlockSpec(w_in.shape, const, pipeline_mode=pl.Buffered(1)),
            pl.BlockSpec(wvt.shape, const),
            pl.BlockSpec(gmat.shape, const),
            pl.BlockSpec(qkg.shape, const),
        ],
        out_specs=[vt_spec if w is None else pl.BlockSpec((tm, w), row) for w in widths],
        compiler_params=_params("arbitrary"),
        name="inproj",
    )(x2, mod, gain, w_in, wvt, gmat, qkg)


def _rglru_kernel(xg_ref, cw_ref, cb_ref, wg_ref, bg_ref, sp_ref, o_ref, ext_ref, hc_ref):
    ts = xg_ref.shape[0]
    C = LRU_WIDTH

    @pl.when(pl.program_id(1) == 0)
    def _():
        ext_ref[0:8, :] = jnp.zeros((8, C), F32)
        hc_ref[...] = jnp.zeros_like(hc_ref)

    xr = xg_ref[:, 0:C].astype(F32)
    ext_ref[8:8 + ts, :] = xr
    xc = cb_ref[...] + xr * cw_ref[CONV_W - 1:CONV_W, :]
    for back in range(1, CONV_W):
        tap = CONV_W - 1 - back
        xc = xc + ext_ref[8 - back:8 - back + ts, :] * cw_ref[tap:tap + 1, :]
    ext_ref[0:8, :] = xr[ts - 8:ts, :]

    gates = _dot(xc.astype(BF16), wg_ref[...]) + bg_ref[...]
    r = _sigmoid(gates[:, 0:C])
    gi = _sigmoid(gates[:, C:2 * C])
    log_a = (-LRU_C) * r * sp_ref[...]
    a = jnp.exp(log_a)
    b = xc * gi * jnp.sqrt(1.0 - a * a)

    rows = lax.broadcasted_iota(jnp.int32, (ts, C), 0)
    d = 1
    while d < ts:
        keep = rows >= d
        a_sh = pltpu.roll(a, d, axis=0)
        b_sh = pltpu.roll(b, d, axis=0)
        b = jnp.where(keep, a * b_sh + b, b)
        a = jnp.where(keep, a * a_sh, a)
        d *= 2
    h = b + a * hc_ref[...]
    hc_ref[...] = h[ts - 1:ts, :]

    gr = xg_ref[:, C:2 * C].astype(F32)
    gelu = 0.5 * gr * (1.0 + jnp.tanh(math.sqrt(2.0 / math.pi) * (gr + 0.044715 * gr * gr * gr)))
    o_ref[...] = (h * gelu).astype(BF16)


def _rglru(xg, conv_w, conv_b, wg, bg, softplus_neg_lam, batch, seq):
    T = xg.shape[0]
    C = LRU_WIDTH
    ts = min(LRU_TS, seq)
    per_b = seq // ts
    const = lambda b, i: (0, 0)
    return pl.pallas_call(
        _rglru_kernel,
        out_shape=jax.ShapeDtypeStruct((T, C), BF16),
        grid=(batch, per_b),
        in_specs=[
            pl.BlockSpec((ts, 2 * C), lambda b, i: (b * per_b + i, 0)),
            pl.BlockSpec((CONV_W, C), const),
            pl.BlockSpec((1, C), const),
            pl.BlockSpec((C, 2 * C), const),
            pl.BlockSpec((1, 2 * C), const),
            pl.BlockSpec((1, C), const),
        ],
        out_specs=pl.BlockSpec((ts, C), lambda b, i: (b * per_b + i, 0)),
        scratch_shapes=[pltpu.VMEM((ts + 8, C), F32), pltpu.VMEM((1, C), F32)],
        compiler_params=_params("arbitrary", "arbitrary"),
        name="rglru",
    )(xg, conv_w, conv_b, wg, bg, softplus_neg_lam)


def _diff_attn_kernel(q1_ref, q2_ref, k_ref, vt_ref, bprev_ref, bdiag_ref, lam_ref, sub_ref, o_ref,
                      m_ref, l_ref, acc_ref, *, tb, lambda_init):
    i = pl.program_id(2)
    q = (q1_ref[0], q2_ref[0])

    m_ref[...] = jnp.full(m_ref.shape, NEG_INF, F32)
    l_ref[...] = jnp.zeros(l_ref.shape, F32)
    acc_ref[...] = jnp.zeros(acc_ref.shape, F32)

    def attend(start, tk, bias):
        kb = k_ref[0, pl.ds(start, tk), :]
        vb = vt_ref[0, :, pl.ds(start, tk)]
        scores = [_dot_nt(kb, q[mp]) for mp in range(2)]
        for mp in range(2):
            s = scores[mp]
            if bias is not None:
                s = s + bias
            m_old = m_ref[mp]
            m_new = jnp.maximum(m_old, jnp.max(s, axis=0, keepdims=True))
            alpha = jnp.exp2(m_old - m_new)
            p = jnp.exp2(s - m_new)
            l_ref[mp] = alpha * l_ref[mp] + jnp.sum(p, axis=0, keepdims=True)
            acc_ref[mp] = alpha * acc_ref[mp] + _dot(vb, p.astype(BF16))
            m_ref[mp] = m_new

    n_far = jnp.maximum(i - 1, 0)
    group = FAR_GROUP * tb

    def far(j, carry):
        attend(pl.multiple_of(j * group, group), group, None)
        return carry

    lax.fori_loop(0, n_far // FAR_GROUP, far, 0)

    @pl.when(n_far % FAR_GROUP >= 2)
    def _():
        attend(pl.multiple_of((n_far // FAR_GROUP) * group, 2 * tb), 2 * tb, None)

    @pl.when(n_far % 2 == 1)
    def _():
        attend(pl.multiple_of((n_far - 1) * tb, tb), tb, None)

    @pl.when(i > 0)
    def _():
        attend(pl.multiple_of((i - 1) * tb, tb), tb, bprev_ref[0])

    attend(pl.multiple_of(i * tb, tb), tb, bdiag_ref[0])

    lv = lam_ref[...]
    lam = (jnp.exp(jnp.sum(lv[0:1] * lv[1:2], axis=-1, keepdims=True))
           - jnp.exp(jnp.sum(lv[2:3] * lv[3:4], axis=-1, keepdims=True)) + lambda_init)
    o = acc_ref[0] / l_ref[0] - lam * (acc_ref[1] / l_ref[1])
    ms = jnp.mean(o * o, axis=0, keepdims=True)
    o = o * lax.rsqrt(ms + NORM_EPS) * sub_ref[...] * (1.0 - lambda_init)
    o_ref[0] = o.T.astype(BF16)


def _diff_attn(q1, q2, k, vt, bprev, bdiag, lam_vecs, subln, lambda_init, batch, seq):
    W = DIFF_HEADS * 2 * DIFF_HD
    tb = min(ATT_TB, seq)
    shp = (batch, seq, W)
    qspec = pl.BlockSpec((1, tb, LANES), lambda b, h, i: (b, i, h))
    kspec = pl.BlockSpec((1, seq, LANES), lambda b, h, i: (b, 0, h))
    vspec = pl.BlockSpec((1, LANES, seq), lambda b, h, i: (b, h, 0))
    bspec = pl.BlockSpec((1, tb, tb), lambda b, h, i: (h, 0, 0))
    const = lambda b, h, i: (0, 0)
    out = pl.pallas_call(
        functools.partial(_diff_attn_kernel, tb=tb, lambda_init=lambda_init),
        out_shape=jax.ShapeDtypeStruct(shp, BF16),
        grid=(batch, DIFF_HEADS, seq // tb),
        in_specs=[qspec, qspec, kspec, vspec, bspec, bspec,
                  pl.BlockSpec((4, DIFF_HD), const), pl.BlockSpec((2 * DIFF_HD, 1), const)],
        out_specs=qspec,
        scratch_shapes=[pltpu.VMEM((2, 1, tb), F32), pltpu.VMEM((2, 1, tb), F32),
                        pltpu.VMEM((2, LANES, tb), F32)],
        compiler_params=_params("arbitrary", "arbitrary", "arbitrary"),
        name="diff_attn",
    )(q1.reshape(shp), q2.reshape(shp), k.reshape(shp), vt, bprev, bdiag, lam_vecs, subln)
    return out.reshape(batch * seq, W)


def _swa_kernel(q_ref, k_ref, v_ref, bp_ref, bc_ref, sink_ref, o_ref, *, ts):
    i = pl.program_id(1)
    blk = WINDOW
    for sub in range(ts // blk):
        start = i * ts + sub * blk
        has_prev = start > 0
        pstart = pl.multiple_of(jnp.maximum(start - blk, 0), blk)
        cstart = pl.multiple_of(start, blk)
        kp = k_ref[0, pl.ds(pstart, blk), :]
        kc = k_ref[0, pl.ds(cstart, blk), :]
        vp = v_ref[0, pl.ds(pstart, blk), :]
        vc = v_ref[0, pl.ds(cstart, blk), :]
        qs = q_ref[0, sub * blk:(sub + 1) * blk, :]
        outs = []
        for hk in range(SWA_KV_HEADS):
            c0 = hk * SWA_GROUP * SWA_HD
            qh = jnp.concatenate(
                [qs[:, c0 + g * SWA_HD:c0 + (g + 1) * SWA_HD] for g in range(SWA_GROUP)], axis=0)
            ksl = slice(hk * SWA_HD, (hk + 1) * SWA_HD)
            s_p = _dot_nt(qh, kp[:, ksl]) + bp_ref[hk]
            s_p = jnp.where(has_prev, s_p, NEG_INF)
            s_c = _dot_nt(qh, kc[:, ksl]) + bc_ref[hk]
            sink = sink_ref[hk]
            m = jnp.maximum(jnp.maximum(jnp.max(s_p, axis=-1, keepdims=True),
                                        jnp.max(s_c, axis=-1, keepdims=True)), sink)
            p_p = jnp.exp(s_p - m)
            p_c = jnp.exp(s_c - m)
            den = (jnp.sum(p_p, axis=-1, keepdims=True) + jnp.sum(p_c, axis=-1, keepdims=True)
                   + jnp.exp(sink - m))
            o = _dot(p_p.astype(BF16), vp[:, ksl]) + _dot(p_c.astype(BF16), vc[:, ksl])
            o = o / den
            outs.extend(o[g * blk:(g + 1) * blk, :] for g in range(SWA_GROUP))
        o_ref[0, sub * blk:(sub + 1) * blk, :] = jnp.concatenate(outs, axis=1).astype(BF16)


def _swa(q, k, v, bias_prev, bias_cur, sinks, batch, seq):
    ts = min(SWA_TS, seq)
    WQ = SWA_HEADS * SWA_HD
    WK = SWA_KV_HEADS * SWA_HD
    rows = SWA_GROUP * WINDOW
    const3 = lambda b, i: (0, 0, 0)
    out = pl.pallas_call(
        functools.partial(_swa_kernel, ts=ts),
        out_shape=jax.ShapeDtypeStruct((batch, seq, WQ), BF16),
        grid=(batch, seq // ts),
        in_specs=[
            pl.BlockSpec((1, ts, WQ), lambda b, i: (b, i, 0)),
            pl.BlockSpec((1, seq, WK), lambda b, i: (b, 0, 0)),
            pl.BlockSpec((1, seq, WK), lambda b, i: (b, 0, 0)),
            pl.BlockSpec((SWA_KV_HEADS, rows, WINDOW), const3),
            pl.BlockSpec((SWA_KV_HEADS, rows, WINDOW), const3),
            pl.BlockSpec((SWA_KV_HEADS, rows, 1), const3),
        ],
        out_specs=pl.BlockSpec((1, ts, WQ), lambda b, i: (b, i, 0)),
        compiler_params=_params("arbitrary", "arbitrary"),
        name="swa_attn",
    )(q.reshape(batch, seq, WQ), k.reshape(batch, seq, WK), v.reshape(batch, seq, WK),
      bias_prev, bias_cur, sinks)
    return out.reshape(batch * seq, WQ)


def _merge_kernel(x_ref, mod_ref, lru_ref, diff_ref, swa_ref, gl_ref, wb_ref, wo_ref, o_ref):
    D = x_ref.shape[1]
    merged = None
    for n, br in enumerate((lru_ref, diff_ref, swa_ref)):
        gate = _sigmoid(gl_ref[:, n * D:(n + 1) * D].astype(F32))
        term = gate * _dot(br[...], wb_ref[n])
        merged = term if merged is None else merged + term
    out = _dot(merged.astype(BF16), wo_ref[...])
    o_ref[...] = x_ref[...] + mod_ref[0, 2:3, :] * out


def _merge(x2, mod, o_lru, o_diff, o_swa, gl, w_branch, w_out, seq):
    T, D = x2.shape
    tm = min(MERGE_TM, seq)
    per_b = seq // tm
    row = lambda i: (i, 0)
    return pl.pallas_call(
        _merge_kernel,
        out_shape=jax.ShapeDtypeStruct((T, D), F32),
        grid=(T // tm,),
        in_specs=[
            pl.BlockSpec((tm, D), row),
            pl.BlockSpec((1, 6, D), lambda i: (i // per_b, 0, 0)),
            pl.BlockSpec((tm, BRANCH_WIDTH), row),
            pl.BlockSpec((tm, BRANCH_WIDTH), row),
            pl.BlockSpec((tm, BRANCH_WIDTH), row),
            pl.BlockSpec((tm, N_BRANCHES * D), row),
            pl.BlockSpec(w_branch.shape, lambda i: (0, 0, 0)),
            pl.BlockSpec(w_out.shape, lambda i: (0, 0)),
        ],
        out_specs=pl.BlockSpec((tm, D), row),
        compiler_params=_params("arbitrary"),
        name="merge_outproj",
    )(x2, mod, o_lru, o_diff, o_swa, gl, w_branch, w_out)


def _router_kernel(x_ref, mod_ref, gain_ref, wr_ref, br_ref, tri_ref,
                   hp_ref, e_ref, pt_ref, rank_ref, cnt_ref, run_ref):
    @pl.when(pl.program_id(0) == 0)
    def _():
        run_ref[...] = jnp.zeros_like(run_ref)

    x = x_ref[...]
    tm, D = x.shape
    h = _rms_mod(x, gain_ref[...], mod_ref[0, 4:5, :], mod_ref[0, 3:4, :])
    for c in range(D // LANES):
        hp_ref[pl.ds(c, tm, stride=SUBLANES), :] = h[:, LANES * c:LANES * (c + 1)]

    logits = _dot_nt(wr_ref[...], h, precision=HIGHEST) + br_ref[...]
    eidx = lax.broadcasted_iota(jnp.int32, logits.shape, 0)
    vals, idxs, hots = [], [], []
    for _ in range(TOP_K):
        mx = jnp.max(logits, axis=0, keepdims=True)
        sel = jnp.min(jnp.where(logits == mx, eidx, N_EXPERTS), axis=0, keepdims=True)
        hot = eidx == sel
        vals.append(mx)
        idxs.append(sel)
        hots.append(hot)
        logits = jnp.where(hot, -jnp.inf, logits)
    e_ref[...] = jnp.concatenate(idxs, axis=0)
    top_v = jnp.concatenate(vals, axis=0)
    ex = jnp.exp(top_v - top_v[0:1])
    p = ex / jnp.sum(ex, axis=0, keepdims=True)
    pt_ref[...] = jnp.concatenate([p, jnp.zeros((LANES - TOP_K, tm), F32)], axis=0).T

    member = hots[0]
    for hot in hots[1:]:
        member = member | hot
    member = jnp.where(member, 1.0, 0.0)
    before = _dot(member.astype(BF16), tri_ref[...]) + run_ref[:, 0:1]
    ranks = [jnp.sum(jnp.where(hot, before, 0.0), axis=0, keepdims=True) for hot in hots]
    rank_ref[...] = jnp.concatenate(ranks, axis=0).astype(jnp.int32)
    run_ref[...] = run_ref[...] + jnp.sum(member, axis=1, keepdims=True)
    cnt_ref[...] = run_ref[...]


def _router(x2, mod, gain, w_router_t, b_router, seq):
    T, D = x2.shape
    tm = min(ROUTER_TM, seq)
    per_b = seq // tm
    tri = (jnp.arange(tm)[:, None] < jnp.arange(tm)[None, :]).astype(BF16)
    const = lambda i: (0, 0)
    return pl.pallas_call(
        _router_kernel,
        out_shape=[jax.ShapeDtypeStruct((T * SUBLANES, LANES), F32),
                   jax.ShapeDtypeStruct((TOP_K, T), jnp.int32),
                   jax.ShapeDtypeStruct((T, LANES), F32),
                   jax.ShapeDtypeStruct((TOP_K, T), jnp.int32),
                   jax.ShapeDtypeStruct((N_EXPERTS, LANES), F32)],
        grid=(T // tm,),
        in_specs=[
            pl.BlockSpec((tm, D), lambda i: (i, 0)),
            pl.BlockSpec((1, 6, D), lambda i: (i // per_b, 0, 0)),
            pl.BlockSpec((1, D), const),
            pl.BlockSpec((N_EXPERTS, D), const),
            pl.BlockSpec((N_EXPERTS, 1), const),
            pl.BlockSpec((tm, tm), const),
        ],
        out_specs=[pl.BlockSpec((tm * SUBLANES, LANES), lambda i: (i, 0)),
                   pl.BlockSpec((TOP_K, tm), lambda i: (0, i)),
                   pl.BlockSpec((tm, LANES), lambda i: (i, 0)),
                   pl.BlockSpec((TOP_K, tm), lambda i: (0, i)),
                   pl.BlockSpec((N_EXPERTS, LANES), const)],
        scratch_shapes=[pltpu.VMEM((N_EXPERTS, LANES), F32)],
        compiler_params=_params("arbitrary"),
        name="router",
    )(x2, mod, gain, w_router_t, b_router, tri)


def _tile_rows(idx, tm):
    K, T = idx.shape
    nb = T // tm
    return idx.reshape(K, nb, tm).transpose(1, 0, 2).reshape(nb, 1, K * tm)


def _expert_kernel(be_ref, nused_ref, tok_ref, tnext_ref, hp_hbm, w1_ref, b1_ref, w2_ref, b2_ref, y_ref,
                   xbuf, sem, w1b_ref, w2s_ref, w2b_ref, act_ref, *, tm):
    i = pl.program_id(0)
    slot = i % 2
    used = i < nused_ref[0]
    next_used = (i + 1 < nused_ref[0]) & (i + 1 < pl.num_programs(0))
    chunks = w1_ref.shape[3] // (2 * LANES)
    hl = LANES // 2

    def gather(idx_ref, s):
        def issue(r, carry):
            src = pl.multiple_of(idx_ref[0, 0, r] * SUBLANES, SUBLANES)
            dst = pl.multiple_of(r * SUBLANES, SUBLANES)
            pltpu.make_async_copy(hp_hbm.at[pl.ds(src, SUBLANES), :], xbuf.at[s, pl.ds(dst, SUBLANES), :],
                                  sem.at[s]).start()
            return carry

        lax.fori_loop(0, tm, issue, 0, unroll=8)

    @pl.when((i == 0) & used)
    def _():
        gather(tok_ref, 0)

    @pl.when(next_used)
    def _():
        gather(tnext_ref, 1 - slot)

    @pl.when(used & ((i == 0) | (be_ref[i] != be_ref[jnp.maximum(i - 1, 0)])))
    def _():
        for q in range(chunks):
            cs = slice(2 * LANES * q, 2 * LANES * (q + 1))
            w1b_ref[:, cs] = w1_ref[0, 0, :, cs].astype(BF16)
        cols = w2_ref.shape[3] // LANES
        for q in range(w2_ref.shape[2] // LANES):
            for s in range(2):
                rows = w2_ref[0, 0, LANES * q + hl * s:LANES * q + hl * (s + 1), :]
                for c in range(cols):
                    w2s_ref[c, pl.ds(LANES * q + s, hl, stride=2), :] = rows[:, LANES * c:LANES * (c + 1)]
        for c in range(cols):
            w2b_ref[:, LANES * c:LANES * (c + 1)] = w2s_ref[c].astype(BF16)

    @pl.when(used)
    def _():
        pltpu.make_async_copy(hp_hbm.at[pl.ds(0, tm * SUBLANES), :], xbuf.at[slot], sem.at[slot]).wait()
        xs = xbuf.at[slot]
        x = jnp.concatenate([xs[pl.ds(c, tm, stride=SUBLANES), :] for c in range(SUBLANES)],
                            axis=1).astype(BF16)
        even = lax.broadcasted_iota(jnp.int32, (tm, LANES), 1) % 2 == 0
        for q in range(chunks):
            cs = slice(2 * LANES * q, 2 * LANES * (q + 1))
            hq = _dot(x, w1b_ref[:, cs]) + b1_ref[0, 0, :, cs]
            lo, hi = hq[:, :LANES], hq[:, LANES:]
            glu = jnp.where(even, lo, pltpu.roll(hi, 1, axis=1))
            lin = jnp.where(even, pltpu.roll(lo, LANES - 1, axis=1), hi)
            glu = jnp.minimum(glu, SWIGLU_LIMIT)
            lin = jnp.clip(lin, -SWIGLU_LIMIT, SWIGLU_LIMIT)
            act = glu * _sigmoid(SWIGLU_ALPHA * glu) * (lin + 1.0)
            act_ref[:, LANES * q:LANES * (q + 1)] = act.astype(BF16)
        y = _dot(act_ref[...], w2b_ref[...]) + b2_ref[0, 0]
        for c in range(SUBLANES):
            y_ref[pl.ds(c, tm, stride=SUBLANES), :] = y[:, LANES * c:LANES * (c + 1)]

    @pl.when(jnp.logical_not(used))
    def _():
        y_ref[...] = jnp.zeros(y_ref.shape, y_ref.dtype)


def _experts(layer, block_expert, n_used, row_token, hp, w1, b1, w2, b2):
    _, E, D, F2 = w1.shape
    assert D == SUBLANES * LANES
    F = F2 // 2
    tm = MOE_TM
    n_rows = row_token.shape[0]
    nb = n_rows // tm
    rows = row_token.reshape(nb, 1, tm)
    wmap = lambda i, be, nu: (layer, be[i], 0, 0)
    return pl.pallas_call(
        functools.partial(_expert_kernel, tm=tm),
        out_shape=jax.ShapeDtypeStruct((n_rows * SUBLANES, LANES), F32),
        grid_spec=pltpu.PrefetchScalarGridSpec(
            num_scalar_prefetch=2,
            grid=(nb,),
            in_specs=[
                pl.BlockSpec((1, 1, tm), lambda i, be, nu: (i, 0, 0), memory_space=pltpu.SMEM),
                pl.BlockSpec((1, 1, tm), lambda i, be, nu: (jnp.minimum(i + 1, nb - 1), 0, 0),
                             memory_space=pltpu.SMEM),
                pl.BlockSpec(memory_space=pl.ANY),
                pl.BlockSpec((1, 1, D, F2), wmap),
                pl.BlockSpec((1, 1, 1, F2), wmap),
                pl.BlockSpec((1, 1, F, D), wmap),
                pl.BlockSpec((1, 1, 1, D), wmap),
            ],
            out_specs=pl.BlockSpec((tm * SUBLANES, LANES), lambda i, be, nu: (i, 0)),
            scratch_shapes=[pltpu.VMEM((2, tm * SUBLANES, LANES), F32), pltpu.SemaphoreType.DMA((2,)),
                            pltpu.VMEM((D, F2), BF16), pltpu.VMEM((D // LANES, F, LANES), F32),
                            pltpu.VMEM((F, D), BF16),
                            pltpu.VMEM((tm, F), BF16)],
        ),
        compiler_params=_params("arbitrary"),
        name="moe_experts",
    )(block_expert, n_used, rows, rows, hp, w1, b1, w2, b2)


def _combine_kernel(dest_ref, dnext_ref, y_hbm, pt_ref, x_ref, mod_ref, o_ref, ybuf, sem, *, tm):
    n = TOP_K * tm
    i = pl.program_id(0)
    slot = i % 2

    def gather(idx_ref, s):
        def issue(r, carry):
            src = pl.multiple_of(idx_ref[0, 0, r] * SUBLANES, SUBLANES)
            dst = pl.multiple_of(r * SUBLANES, SUBLANES)
            pltpu.make_async_copy(y_hbm.at[pl.ds(src, SUBLANES), :], ybuf.at[s, pl.ds(dst, SUBLANES), :],
                                  sem.at[s]).start()
            return carry

        lax.fori_loop(0, n, issue, 0, unroll=8)

    @pl.when(i == 0)
    def _():
        gather(dest_ref, 0)

    @pl.when(i + 1 < pl.num_programs(0))
    def _():
        gather(dnext_ref, 1 - slot)

    pltpu.make_async_copy(y_hbm.at[pl.ds(0, n * SUBLANES), :], ybuf.at[slot], sem.at[slot]).wait()

    rows = ybuf.at[slot]
    for c in range(SUBLANES):
        cs = slice(LANES * c, LANES * (c + 1))
        acc = rows[pl.ds(c, tm, stride=SUBLANES), :] * pt_ref[:, 0:1]
        for k in range(1, TOP_K):
            acc = acc + rows[pl.ds(k * tm * SUBLANES + c, tm, stride=SUBLANES), :] * pt_ref[:, k:k + 1]
        o_ref[:, cs] = x_ref[:, cs] + mod_ref[0, 5:6, cs] * acc


def _combine(dest, y, pt, x2, mod, seq):
    T, D = x2.shape
    tm = min(COMB_TM, seq)
    per_b = seq // tm
    nb = T // tm
    rows = _tile_rows(dest, tm)
    return pl.pallas_call(
        functools.partial(_combine_kernel, tm=tm),
        out_shape=jax.ShapeDtypeStruct((T, D), F32),
        grid=(nb,),
        in_specs=[
            pl.BlockSpec((1, 1, TOP_K * tm), lambda i: (i, 0, 0), memory_space=pltpu.SMEM),
            pl.BlockSpec((1, 1, TOP_K * tm), lambda i: (jnp.minimum(i + 1, nb - 1), 0, 0),
                         memory_space=pltpu.SMEM),
            pl.BlockSpec(memory_space=pl.ANY),
            pl.BlockSpec((tm, LANES), lambda i: (i, 0)),
            pl.BlockSpec((tm, D), lambda i: (i, 0)),
            pl.BlockSpec((1, 6, D), lambda i: (i // per_b, 0, 0)),
        ],
        out_specs=pl.BlockSpec((tm, D), lambda i: (i, 0)),
        scratch_shapes=[pltpu.VMEM((2, TOP_K * tm * SUBLANES, LANES), F32), pltpu.SemaphoreType.DMA((2,))],
        compiler_params=_params("arbitrary"),
        name="moe_combine",
    )(rows, rows, y, pt, x2, mod)


def _t5_bucket(dist):
    exact = REL_BUCKETS // 2
    log_ratio = jnp.log(jnp.maximum(dist, 1).astype(F32) / exact) / math.log(REL_MAX_DIST / exact)
    large = exact + (log_ratio * (REL_BUCKETS - exact)).astype(jnp.int32)
    return jnp.where(dist < exact, dist, jnp.minimum(large, REL_BUCKETS - 1))


def _diff_bias_tiles(rel_bias, tb):
    table = rel_bias[:, :DIFF_HEADS].astype(F32)
    shifted = (table - table[REL_BUCKETS - 1]) * LOG2E
    kk = jnp.arange(tb)[:, None]
    qq = jnp.arange(tb)[None, :]

    def tile(dist):
        return _bucket_lookup(shifted, _t5_bucket(jnp.maximum(dist, 0)))

    diag = jnp.where((qq - kk >= 0)[None], tile(qq - kk), NEG_INF)
    prev = tile(qq + tb - kk)
    return prev, diag


def _bucket_lookup(table, bucket):
    out = jnp.zeros((table.shape[1],) + bucket.shape, F32)
    for b in range(table.shape[0]):
        out = out + jnp.where(bucket[None] == b, table[b].reshape((-1,) + (1,) * bucket.ndim), 0.0)
    return out


def _swa_bias_tiles(rel_bias):
    qi = jnp.arange(WINDOW)[:, None]
    kj = jnp.arange(2 * WINDOW)[None, :]
    dist = WINDOW + qi - kj
    ok = (dist >= 0) & (dist < WINDOW)
    bias = _bucket_lookup(rel_bias[:, DIFF_HEADS:].astype(F32),
                          _t5_bucket(jnp.clip(dist, 0, WINDOW - 1)))
    bias = jnp.where(ok[None], bias, NEG_INF)
    bias = bias.reshape(SWA_KV_HEADS, SWA_GROUP * WINDOW, 2 * WINDOW)
    return bias[..., :WINDOW], bias[..., WINDOW:]


def _block_diag(w):
    nb, n, _ = w.shape
    eye = jnp.eye(nb, dtype=w.dtype)
    return (eye[:, None, :, None] * w[:, :, None, :]).reshape(nb * n, nb * n)


def _routing(counts, top_e, rank, tm):
    K, T = top_e.shape
    counts = counts.astype(jnp.int32)
    padded = (counts + tm - 1) // tm * tm
    pad_ends = jnp.cumsum(padded)
    pad_starts = pad_ends - padded
    onehot = top_e[..., None] == jnp.arange(N_EXPERTS, dtype=jnp.int32)
    dest = rank + jnp.sum(jnp.where(onehot, pad_starts, 0), axis=-1)
    n_rows = K * T + N_EXPERTS * tm
    nb = n_rows // tm
    n_used = pad_ends[-1] // tm
    blk = jnp.minimum(jnp.arange(nb), n_used - 1) * tm
    block_expert = jnp.sum(blk[:, None] >= pad_ends[None, :], axis=-1)
    fill_ends = jnp.cumsum(padded - counts)
    fill_expert = jnp.sum(jnp.arange(N_EXPERTS * tm)[:, None] >= fill_ends[None, :], axis=-1)
    keys = jnp.concatenate([(top_e * T + jnp.arange(T, dtype=jnp.int32)[None, :]).reshape(-1),
                            (fill_expert * T + (T - 1)).astype(jnp.int32)])
    row_token = jnp.sort(keys) % T
    return (block_expert.astype(jnp.int32), n_used.astype(jnp.int32).reshape(1), dest.astype(jnp.int32),
            row_token.astype(jnp.int32))


def kernel(x, c, w_ada, b_ada, norm_mix, norm_ffn, w_in, conv_w, conv_b, lru_wa, lru_ba, lru_wx, lru_bx,
           lru_lambda, diff_qnorm, diff_knorm, diff_lambda, diff_subln, swa_qnorm, swa_knorm, swa_sinks,
           rel_bias, w_branch, w_out, w_router, b_router, w1, b1, w2, b2):
    B, S, D = x.shape
    L = w_ada.shape[0]
    T = B * S
    tb = min(ATT_TB, S)

    mods = _adaln(c, w_ada, b_ada)
    bprev, bdiag = _diff_bias_tiles(rel_bias, tb)
    sbias_prev, sbias_cur = _swa_bias_tiles(rel_bias)
    gidx = jnp.arange(512) // DIFF_HD
    gmat = jnp.where(gidx[:, None] == gidx[None, :], 1.0 / DIFF_HD, 0.0).astype(BF16)

    x2 = x.reshape(T, D)
    for l in range(L):
        lambda_init = 0.8 - 0.6 * math.exp(-0.3 * l)
        mod = mods[l]
        qkg = jnp.stack([
            jnp.tile(diff_qnorm[l], 8) * (DIFF_HD ** -0.5 * LOG2E),
            jnp.tile(diff_knorm[l], 8),
            jnp.tile(swa_qnorm[l], 8) * SWA_HD ** -0.5,
            jnp.tile(swa_knorm[l], 8),
        ]).astype(F32)
        w_in_l = w_in[l].astype(BF16)
        xg, q1, q2, dk, dvt, sq, sk, sv, gl = _inproj(
            x2, mod, norm_mix[l].reshape(1, D), w_in_l, w_in_l[:, 2048:2560].T, gmat, qkg, S)

        wg = jnp.concatenate([_block_diag(lru_wa[l]), _block_diag(lru_wx[l])], axis=1).astype(BF16)
        bg = jnp.concatenate([lru_ba[l], lru_bx[l]]).reshape(1, 2 * LRU_WIDTH)
        sp = jax.nn.softplus(-lru_lambda[l].astype(F32)).reshape(1, LRU_WIDTH)
        o_lru = _rglru(xg, conv_w[l], conv_b[l].reshape(1, LRU_WIDTH), wg, bg, sp, B, S)

        o_diff = _diff_attn(q1, q2, dk, dvt, bprev, bdiag, diff_lambda[l],
                            diff_subln[l].reshape(2 * DIFF_HD, 1), lambda_init, B, S)

        sinks = jnp.repeat(swa_sinks[l].astype(F32).reshape(SWA_KV_HEADS, SWA_GROUP), WINDOW, axis=1)
        o_swa = _swa(sq, sk, sv, sbias_prev, sbias_cur, sinks.reshape(SWA_KV_HEADS, -1, 1), B, S)

        x2 = _merge(x2, mod, o_lru, o_diff, o_swa, gl, w_branch[l].astype(BF16), w_out[l].astype(BF16), S)

        hp, top_e, pt, rank, counts = _router(x2, mod, norm_ffn[l].reshape(1, D), w_router[l].T,
                                              b_router[l].reshape(N_EXPERTS, 1), S)
        block_expert, n_used, dest, row_token = _routing(counts[:, 0], top_e, rank, MOE_TM)
        y = _experts(l, block_expert, n_used, row_token, hp, w1, b1[:, :, None, :], w2, b2[:, :, None, :])
        x2 = _combine(dest, y, pt, x2, mod, S)
    return x2.reshape(B, S, D)
```

```python
import functools
import math

import jax
import jax.numpy as jnp
from jax import lax
from jax.experimental import pallas as pl
from jax.experimental.pallas import tpu as pltpu

F32 = jnp.float32
BF16 = jnp.bfloat16

LRU_WIDTH = 512
LRU_BLOCKS = 8
LRU_C = 8.0
CONV_W = 4
DIFF_HEADS = 4
DIFF_HD = 64
SWA_HEADS = 8
SWA_KV_HEADS = 2
SWA_GROUP = SWA_HEADS // SWA_KV_HEADS
SWA_HD = 64
WINDOW = 128
N_BRANCHES = 3
BRANCH_WIDTH = 512
REL_BUCKETS = 32
REL_MAX_DIST = 128
N_EXPERTS = 32
TOP_K = 4
SWIGLU_LIMIT = 7.0
SWIGLU_ALPHA = 1.702
NORM_EPS = 1e-6
NEG_INF = -1e30
LOG2E = math.log2(math.e)

VMEM_LIMIT_BYTES = 56 * 1024 * 1024
LANES = 128
SUBLANES = 8

ADALN_TN = 1536
PROJ_TM = 512
LRU_TS = 512
ATT_TB = 512
FAR_GROUP = 4
SWA_TS = 512
MERGE_TM = 512
ROUTER_TM = 512
MOE_TM = 512
COMB_TM = 256

HIGHEST = lax.Precision.HIGHEST


def _params(*sem):
    return pltpu.CompilerParams(dimension_semantics=sem, vmem_limit_bytes=VMEM_LIMIT_BYTES)


def _dot(a, b, **kw):
    return jnp.dot(a, b, preferred_element_type=F32, **kw)


def _dot_nt(a, b, **kw):
    return lax.dot_general(a, b, (((1,), (1,)), ((), ())), preferred_element_type=F32, **kw)


def _sigmoid(x):
    return 1.0 / (1.0 + jnp.exp(-x))


def _adaln_kernel(c_ref, w_ref, b_ref, o_ref):
    c = c_ref[...]
    cond = c * _sigmoid(c)
    o_ref[0] = _dot(cond, w_ref[0], precision=HIGHEST) + b_ref[0]


def _adaln(c, w_ada, b_ada):
    L, D, N = w_ada.shape
    B = c.shape[0]
    rows = 8
    cp = jnp.zeros((rows, D), F32).at[:B].set(c)
    out = pl.pallas_call(
        _adaln_kernel,
        out_shape=jax.ShapeDtypeStruct((L, rows, N), F32),
        grid=(L, N // ADALN_TN),
        in_specs=[
            pl.BlockSpec((rows, D), lambda l, j: (0, 0)),
            pl.BlockSpec((1, D, ADALN_TN), lambda l, j: (l, 0, j)),
            pl.BlockSpec((1, 1, ADALN_TN), lambda l, j: (l, 0, j)),
        ],
        out_specs=pl.BlockSpec((1, rows, ADALN_TN), lambda l, j: (l, 0, j)),
        compiler_params=_params("arbitrary", "arbitrary"),
        name="adaln",
    )(cp, w_ada, b_ada.reshape(L, 1, N))
    return out[:, :B].reshape(L, B, 6, D)


def _rms_mod(x, gain, scale, shift):
    ms = jnp.mean(x * x, axis=-1, keepdims=True)
    return (x * lax.rsqrt(ms + NORM_EPS) * gain) * (1.0 + scale) + shift


def _group_rms(x, gmat):
    ms = _dot((x * x).astype(BF16), gmat)
    return x * lax.rsqrt(ms + NORM_EPS)


def _inproj_kernel(x_ref, mod_ref, gain_ref, w_ref, wvt_ref, gmat_ref, qkg_ref,
                   xg_ref, q1_ref, q2_ref, dk_ref, dvt_ref, sq_ref, sk_ref, sv_ref, gl_ref):
    x = x_ref[...]
    h = _rms_mod(x, gain_ref[...], mod_ref[0, 1:2, :], mod_ref[0, 0:1, :]).astype(BF16)
    gmat = gmat_ref[...]

    xg_ref[...] = _dot(h, w_ref[:, 0:1024]).astype(BF16)

    dd = _dot(h, w_ref[:, 1024:2048])
    qn = _group_rms(dd[:, 0:512], gmat) * qkg_ref[0:1, :]
    kn = _group_rms(dd[:, 512:1024], gmat) * qkg_ref[1:2, :]
    lane = lax.broadcasted_iota(jnp.int32, qn.shape, 1) % LANES
    q1_ref[...] = jnp.where(lane < DIFF_HD, qn, 0.0).astype(BF16)
    q2_ref[...] = jnp.where(lane >= DIFF_HD, qn, 0.0).astype(BF16)
    dk_ref[...] = kn.astype(BF16)
    dvt_ref[0] = _dot_nt(wvt_ref[...], h).astype(BF16)

    ss = _dot(h, w_ref[:, 2560:3328])
    sq_ref[...] = (_group_rms(ss[:, 0:512], gmat) * qkg_ref[2:3, :]).astype(BF16)
    sk_ref[...] = (_group_rms(ss[:, 512:640], gmat[0:128, 0:128]) * qkg_ref[3:4, 0:128]).astype(BF16)
    sv_ref[...] = ss[:, 640:768].astype(BF16)

    gl_ref[...] = _dot(h, w_ref[:, 3328:6400]).astype(BF16)


def _inproj(x2, mod, gain, w_in, wvt, gmat, qkg, seq):
    T, D = x2.shape
    tm = min(PROJ_TM, seq)
    per_b = seq // tm
    VW = wvt.shape[0]
    widths = (1024, 512, 512, 512, None, 512, 128, 128, 3072)
    row = lambda i: (i, 0)
    const = lambda i: (0, 0)
    vt_shape = jax.ShapeDtypeStruct((T // seq, VW, seq), BF16)
    vt_spec = pl.BlockSpec((1, VW, tm), lambda i: (i // per_b, 0, i % per_b))
    return pl.pallas_call(
        _inproj_kernel,
        out_shape=[vt_shape if w is None else jax.ShapeDtypeStruct((T, w), BF16) for w in widths],
        grid=(T // tm,),
        in_specs=[
            pl.BlockSpec((tm, D), row),
            pl.BlockSpec((1, 6, D), lambda i: (i // per_b, 0, 0)),
            pl.BlockSpec((1, D), const),
            pl.BlockSpec(w_in.shape, const, pipeline_mode=pl.Buffered(1)),
            pl.BlockSpec(wvt.shape, const),
            pl.BlockSpec(gmat.shape, const),
            pl.BlockSpec(qkg.shape, const),
        ],
        out_specs=[vt_spec if w is None else pl.BlockSpec((tm, w), row) for w in widths],
        compiler_params=_params("arbitrary"),
        name="inproj",
    )(x2, mod, gain, w_in, wvt, gmat, qkg)


def _rglru_kernel(xg_ref, cw_ref, cb_ref, wg_ref, bg_ref, sp_ref, o_ref, ext_ref, hc_ref):
    ts = xg_ref.shape[0]
    C = LRU_WIDTH

    @pl.when(pl.program_id(1) == 0)
    def _():
        ext_ref[0:8, :] = jnp.zeros((8, C), F32)
        hc_ref[...] = jnp.zeros_like(hc_ref)

    xr = xg_ref[:, 0:C].astype(F32)
    ext_ref[8:8 + ts, :] = xr
    xc = cb_ref[...] + xr * cw_ref[CONV_W - 1:CONV_W, :]
    for back in range(1, CONV_W):
        tap = CONV_W - 1 - back
        xc = xc + ext_ref[8 - back:8 - back + ts, :] * cw_ref[tap:tap + 1, :]
    ext_ref[0:8, :] = xr[ts - 8:ts, :]

    gates = _dot(xc.astype(BF16), wg_ref[...]) + bg_ref[...]
    r = _sigmoid(gates[:, 0:C])
    gi = _sigmoid(gates[:, C:2 * C])
    log_a = (-LRU_C) * r * sp_ref[...]
    a = jnp.exp(log_a)
    b = xc * gi * jnp.sqrt(1.0 - a * a)

    rows = lax.broadcasted_iota(jnp.int32, (ts, C), 0)
    d = 1
    while d < ts:
        keep = rows >= d
        a_sh = pltpu.roll(a, d, axis=0)
        b_sh = pltpu.roll(b, d, axis=0)
        b = jnp.where(keep, a * b_sh + b, b)
        a = jnp.where(keep, a * a_sh, a)
        d *= 2
    h = b + a * hc_ref[...]
    hc_ref[...] = h[ts - 1:ts, :]

    gr = xg_ref[:, C:2 * C].astype(F32)
    gelu = 0.5 * gr * (1.0 + jnp.tanh(math.sqrt(2.0 / math.pi) * (gr + 0.044715 * gr * gr * gr)))
    o_ref[...] = (h * gelu).astype(BF16)


def _rglru(xg, conv_w, conv_b, wg, bg, softplus_neg_lam, batch, seq):
    T = xg.shape[0]
    C = LRU_WIDTH
    ts = min(LRU_TS, seq)
    per_b = seq // ts
    const = lambda b, i: (0, 0)
    return pl.pallas_call(
        _rglru_kernel,
        out_shape=jax.ShapeDtypeStruct((T, C), BF16),
        grid=(batch, per_b),
        in_specs=[
            pl.BlockSpec((ts, 2 * C), lambda b, i: (b * per_b + i, 0)),
            pl.BlockSpec((CONV_W, C), const),
            pl.BlockSpec((1, C), const),
            pl.BlockSpec((C, 2 * C), const),
            pl.BlockSpec((1, 2 * C), const),
            pl.BlockSpec((1, C), const),
        ],
        out_specs=pl.BlockSpec((ts, C), lambda b, i: (b * per_b + i, 0)),
        scratch_shapes=[pltpu.VMEM((ts + 8, C), F32), pltpu.VMEM((1, C), F32)],
        compiler_params=_params("arbitrary", "arbitrary"),
        name="rglru",
    )(xg, conv_w, conv_b, wg, bg, softplus_neg_lam)


def _diff_attn_kernel(q1_ref, q2_ref, k_ref, vt_ref, bprev_ref, bdiag_ref, lam_ref, sub_ref, o_ref,
                      m_ref, l_ref, acc_ref, *, tb, lambda_init):
    i = pl.program_id(2)
    q = (q1_ref[0], q2_ref[0])

    m_ref[...] = jnp.full(m_ref.shape, NEG_INF, F32)
    l_ref[...] = jnp.zeros(l_ref.shape, F32)
    acc_ref[...] = jnp.zeros(acc_ref.shape, F32)

    def attend(start, tk, bias):
        kb = k_ref[0, pl.ds(start, tk), :]
        vb = vt_ref[0, :, pl.ds(start, tk)]
        scores = [_dot_nt(kb, q[mp]) for mp in range(2)]
        for mp in range(2):
            s = scores[mp]
            if bias is not None:
                s = s + bias
            m_old = m_ref[mp]
            m_new = jnp.maximum(m_old, jnp.max(s, axis=0, keepdims=True))
            alpha = jnp.exp2(m_old - m_new)
            p = jnp.exp2(s - m_new)
            l_ref[mp] = alpha * l_ref[mp] + jnp.sum(p, axis=0, keepdims=True)
            acc_ref[mp] = alpha * acc_ref[mp] + _dot(vb, p.astype(BF16))
            m_ref[mp] = m_new

    n_far = jnp.maximum(i - 1, 0)
    group = FAR_GROUP * tb

    def far(j, carry):
        attend(pl.multiple_of(j * group, group), group, None)
        return carry

    lax.fori_loop(0, n_far // FAR_GROUP, far, 0)

    @pl.when(n_far % FAR_GROUP >= 2)
    def _():
        attend(pl.multiple_of((n_far // FAR_GROUP) * group, 2 * tb), 2 * tb, None)

    @pl.when(n_far % 2 == 1)
    def _():
        attend(pl.multiple_of((n_far - 1) * tb, tb), tb, None)

    @pl.when(i > 0)
    def _():
        attend(pl.multiple_of((i - 1) * tb, tb), tb, bprev_ref[0])

    attend(pl.multiple_of(i * tb, tb), tb, bdiag_ref[0])

    lv = lam_ref[...]
    lam = (jnp.exp(jnp.sum(lv[0:1] * lv[1:2], axis=-1, keepdims=True))
           - jnp.exp(jnp.sum(lv[2:3] * lv[3:4], axis=-1, keepdims=True)) + lambda_init)
    o = acc_ref[0] / l_ref[0] - lam * (acc_ref[1] / l_ref[1])
    ms = jnp.mean(o * o, axis=0, keepdims=True)
    o = o * lax.rsqrt(ms + NORM_EPS) * sub_ref[...] * (1.0 - lambda_init)
    o_ref[0] = o.T.astype(BF16)


def _diff_attn(q1, q2, k, vt, bprev, bdiag, lam_vecs, subln, lambda_init, batch, seq):
    W = DIFF_HEADS * 2 * DIFF_HD
    tb = min(ATT_TB, seq)
    shp = (batch, seq, W)
    qspec = pl.BlockSpec((1, tb, LANES), lambda b, h, i: (b, i, h))
    kspec = pl.BlockSpec((1, seq, LANES), lambda b, h, i: (b, 0, h))
    vspec = pl.BlockSpec((1, LANES, seq), lambda b, h, i: (b, h, 0))
    bspec = pl.BlockSpec((1, tb, tb), lambda b, h, i: (h, 0, 0))
    const = lambda b, h, i: (0, 0)
    out = pl.pallas_call(
        functools.partial(_diff_attn_kernel, tb=tb, lambda_init=lambda_init),
        out_shape=jax.ShapeDtypeStruct(shp, BF16),
        grid=(batch, DIFF_HEADS, seq // tb),
        in_specs=[qspec, qspec, kspec, vspec, bspec, bspec,
                  pl.BlockSpec((4, DIFF_HD), const), pl.BlockSpec((2 * DIFF_HD, 1), const)],
        out_specs=qspec,
        scratch_shapes=[pltpu.VMEM((2, 1, tb), F32), pltpu.VMEM((2, 1, tb), F32),
                        pltpu.VMEM((2, LANES, tb), F32)],
        compiler_params=_params("arbitrary", "arbitrary", "arbitrary"),
        name="diff_attn",
    )(q1.reshape(shp), q2.reshape(shp), k.reshape(shp), vt, bprev, bdiag, lam_vecs, subln)
    return out.reshape(batch * seq, W)


def _swa_kernel(q_ref, k_ref, v_ref, bp_ref, bc_ref, sink_ref, o_ref, *, ts):
    i = pl.program_id(1)
    blk = WINDOW
    for sub in range(ts // blk):
        start = i * ts + sub * blk
        has_prev = start > 0
        pstart = pl.multiple_of(jnp.maximum(start - blk, 0), blk)
        cstart = pl.multiple_of(start, blk)
        kp = k_ref[0, pl.ds(pstart, blk), :]
        kc = k_ref[0, pl.ds(cstart, blk), :]
        vp = v_ref[0, pl.ds(pstart, blk), :]
        vc = v_ref[0, pl.ds(cstart, blk), :]
        qs = q_ref[0, sub * blk:(sub + 1) * blk, :]
        outs = []
        for hk in range(SWA_KV_HEADS):
            c0 = hk * SWA_GROUP * SWA_HD
            qh = jnp.concatenate(
                [qs[:, c0 + g * SWA_HD:c0 + (g + 1) * SWA_HD] for g in range(SWA_GROUP)], axis=0)
            ksl = slice(hk * SWA_HD, (hk + 1) * SWA_HD)
            s_p = _dot_nt(qh, kp[:, ksl]) + bp_ref[hk]
            s_p = jnp.where(has_prev, s_p, NEG_INF)
            s_c = _dot_nt(qh, kc[:, ksl]) + bc_ref[hk]
            sink = sink_ref[hk]
            m = jnp.maximum(jnp.maximum(jnp.max(s_p, axis=-1, keepdims=True),
                                        jnp.max(s_c, axis=-1, keepdims=True)), sink)
            p_p = jnp.exp(s_p - m)
            p_c = jnp.exp(s_c - m)
            den = (jnp.sum(p_p, axis=-1, keepdims=True) + jnp.sum(p_c, axis=-1, keepdims=True)
                   + jnp.exp(sink - m))
            o = _dot(p_p.astype(BF16), vp[:, ksl]) + _dot(p_c.astype(BF16), vc[:, ksl])
            o = o / den
            outs.extend(o[g * blk:(g + 1) * blk, :] for g in range(SWA_GROUP))
        o_ref[0, sub * blk:(sub + 1) * blk, :] = jnp.concatenate(outs, axis=1).astype(BF16)


def _swa(q, k, v, bias_prev, bias_cur, sinks, batch, seq):
    ts = min(SWA_TS, seq)
    WQ = SWA_HEADS * SWA_HD
    WK = SWA_KV_HEADS * SWA_HD
    rows = SWA_GROUP * WINDOW
    const3 = lambda b, i: (0, 0, 0)
    out = pl.pallas_call(
        functools.partial(_swa_kernel, ts=ts),
        out_shape=jax.ShapeDtypeStruct((batch, seq, WQ), BF16),
        grid=(batch, seq // ts),
        in_specs=[
            pl.BlockSpec((1, ts, WQ), lambda b, i: (b, i, 0)),
            pl.BlockSpec((1, seq, WK), lambda b, i: (b, 0, 0)),
            pl.BlockSpec((1, seq, WK), lambda b, i: (b, 0, 0)),
            pl.BlockSpec((SWA_KV_HEADS, rows, WINDOW), const3),
            pl.BlockSpec((SWA_KV_HEADS, rows, WINDOW), const3),
            pl.BlockSpec((SWA_KV_HEADS, rows, 1), const3),
        ],
        out_specs=pl.BlockSpec((1, ts, WQ), lambda b, i: (b, i, 0)),
        compiler_params=_params("arbitrary", "arbitrary"),
        name="swa_attn",
    )(q.reshape(batch, seq, WQ), k.reshape(batch, seq, WK), v.reshape(batch, seq, WK),
      bias_prev, bias_cur, sinks)
    return out.reshape(batch * seq, WQ)


def _merge_kernel(x_ref, mod_ref, lru_ref, diff_ref, swa_ref, gl_ref, wb_ref, wo_ref, o_ref):
    D = x_ref.shape[1]
    merged = None
    for n, br in enumerate((lru_ref, diff_ref, swa_ref)):
        gate = _sigmoid(gl_ref[:, n * D:(n + 1) * D].astype(F32))
        term = gate * _dot(br[...], wb_ref[n])
        merged = term if merged is None else merged + term
    out = _dot(merged.astype(BF16), wo_ref[...])
    o_ref[...] = x_ref[...] + mod_ref[0, 2:3, :] * out


def _merge(x2, mod, o_lru, o_diff, o_swa, gl, w_branch, w_out, seq):
    T, D = x2.shape
    tm = min(MERGE_TM, seq)
    per_b = seq // tm
    row = lambda i: (i, 0)
    return pl.pallas_call(
        _merge_kernel,
        out_shape=jax.ShapeDtypeStruct((T, D), F32),
        grid=(T // tm,),
        in_specs=[
            pl.BlockSpec((tm, D), row),
            pl.BlockSpec((1, 6, D), lambda i: (i // per_b, 0, 0)),
            pl.BlockSpec((tm, BRANCH_WIDTH), row),
            pl.BlockSpec((tm, BRANCH_WIDTH), row),
            pl.BlockSpec((tm, BRANCH_WIDTH), row),
            pl.BlockSpec((tm, N_BRANCHES * D), row),
            pl.BlockSpec(w_branch.shape, lambda i: (0, 0, 0)),
            pl.BlockSpec(w_out.shape, lambda i: (0, 0)),
        ],
        out_specs=pl.BlockSpec((tm, D), row),
        compiler_params=_params("arbitrary"),
        name="merge_outproj",
    )(x2, mod, o_lru, o_diff, o_swa, gl, w_branch, w_out)


def _router_kernel(x_ref, mod_ref, gain_ref, wr_ref, br_ref, tri_ref,
                   hp_ref, e_ref, pt_ref, rank_ref, cnt_ref, run_ref):
    @pl.when(pl.program_id(0) == 0)
    def _():
        run_ref[...] = jnp.zeros_like(run_ref)

    x = x_ref[...]
    tm, D = x.shape
    h = _rms_mod(x, gain_ref[...], mod_ref[0, 4:5, :], mod_ref[0, 3:4, :])
    for c in range(D // LANES):
        hp_ref[pl.ds(c, tm, stride=SUBLANES), :] = h[:, LANES * c:LANES * (c + 1)]

    logits = _dot_nt(wr_ref[...], h, precision=HIGHEST) + br_ref[...]
    eidx = lax.broadcasted_iota(jnp.int32, logits.shape, 0)
    vals, idxs, hots = [], [], []
    for _ in range(TOP_K):
        mx = jnp.max(logits, axis=0, keepdims=True)
        sel = jnp.min(jnp.where(logits == mx, eidx, N_EXPERTS), axis=0, keepdims=True)
        hot = eidx == sel
        vals.append(mx)
        idxs.append(sel)
        hots.append(hot)
        logits = jnp.where(hot, -jnp.inf, logits)
    e_ref[...] = jnp.concatenate(idxs, axis=0)
    top_v = jnp.concatenate(vals, axis=0)
    ex = jnp.exp(top_v - top_v[0:1])
    p = ex / jnp.sum(ex, axis=0, keepdims=True)
    pt_ref[...] = jnp.concatenate([p, jnp.zeros((LANES - TOP_K, tm), F32)], axis=0).T

    member = hots[0]
    for hot in hots[1:]:
        member = member | hot
    member = jnp.where(member, 1.0, 0.0)
    before = _dot(member.astype(BF16), tri_ref[...]) + run_ref[:, 0:1]
    ranks = [jnp.sum(jnp.where(hot, before, 0.0), axis=0, keepdims=True) for hot in hots]
    rank_ref[...] = jnp.concatenate(ranks, axis=0).astype(jnp.int32)
    run_ref[...] = run_ref[...] + jnp.sum(member, axis=1, keepdims=True)
    cnt_ref[...] = run_ref[...]


def _router(x2, mod, gain, w_router_t, b_router, seq):
    T, D = x2.shape
    tm = min(ROUTER_TM, seq)
    per_b = seq // tm
    tri = (jnp.arange(tm)[:, None] < jnp.arange(tm)[None, :]).astype(BF16)
    const = lambda i: (0, 0)
    return pl.pallas_call(
        _router_kernel,
        out_shape=[jax.ShapeDtypeStruct((T * SUBLANES, LANES), F32),
                   jax.ShapeDtypeStruct((TOP_K, T), jnp.int32),
                   jax.ShapeDtypeStruct((T, LANES), F32),
                   jax.ShapeDtypeStruct((TOP_K, T), jnp.int32),
                   jax.ShapeDtypeStruct((N_EXPERTS, LANES), F32)],
        grid=(T // tm,),
        in_specs=[
            pl.BlockSpec((tm, D), lambda i: (i, 0)),
            pl.BlockSpec((1, 6, D), lambda i: (i // per_b, 0, 0)),
            pl.BlockSpec((1, D), const),
            pl.BlockSpec((N_EXPERTS, D), const),
            pl.BlockSpec((N_EXPERTS, 1), const),
            pl.BlockSpec((tm, tm), const),
        ],
        out_specs=[pl.BlockSpec((tm * SUBLANES, LANES), lambda i: (i, 0)),
                   pl.BlockSpec((TOP_K, tm), lambda i: (0, i)),
                   pl.BlockSpec((tm, LANES), lambda i: (i, 0)),
                   pl.BlockSpec((TOP_K, tm), lambda i: (0, i)),
                   pl.BlockSpec((N_EXPERTS, LANES), const)],
        scratch_shapes=[pltpu.VMEM((N_EXPERTS, LANES), F32)],
        compiler_params=_params("arbitrary"),
        name="router",
    )(x2, mod, gain, w_router_t, b_router, tri)


def _tile_rows(idx, tm):
    K, T = idx.shape
    nb = T // tm
    return idx.reshape(K, nb, tm).transpose(1, 0, 2).reshape(nb, 1, K * tm)


def _expert_kernel(be_ref, nused_ref, tok_ref, tnext_ref, hp_hbm, w1_ref, b1_ref, w2_ref, b2_ref, y_ref,
                   xbuf, sem, w1b_ref, w2s_ref, w2b_ref, act_ref, *, tm):
    i = pl.program_id(0)
    slot = i % 2
    used = i < nused_ref[0]
    next_used = (i + 1 < nused_ref[0]) & (i + 1 < pl.num_programs(0))
    chunks = w1_ref.shape[3] // (2 * LANES)
    hl = LANES // 2

    def gather(idx_ref, s):
        def issue(r, carry):
            src = pl.multiple_of(idx_ref[0, 0, r] * SUBLANES, SUBLANES)
            dst = pl.multiple_of(r * SUBLANES, SUBLANES)
            pltpu.make_async_copy(hp_hbm.at[pl.ds(src, SUBLANES), :], xbuf.at[s, pl.ds(dst, SUBLANES), :],
                                  sem.at[s]).start()
            return carry

        lax.fori_loop(0, tm, issue, 0, unroll=8)

    @pl.when((i == 0) & used)
    def _():
        gather(tok_ref, 0)

    for s in range(2):
        @pl.when(next_used & (slot != s))
        def _(s=s):
            gather(tnext_ref, s)

    @pl.when(used & ((i == 0) | (be_ref[i] != be_ref[jnp.maximum(i - 1, 0)])))
    def _():
        for q in range(chunks):
            cs = slice(2 * LANES * q, 2 * LANES * (q + 1))
            w1b_ref[:, cs] = w1_ref[0, 0, :, cs].astype(BF16)
        cols = w2_ref.shape[3] // LANES
        for q in range(w2_ref.shape[2] // LANES):
            for s in range(2):
                rows = w2_ref[0, 0, LANES * q + hl * s:LANES * q + hl * (s + 1), :]
                for c in range(cols):
                    w2s_ref[c, pl.ds(LANES * q + s, hl, stride=2), :] = rows[:, LANES * c:LANES * (c + 1)]
        for c in range(cols):
            w2b_ref[:, LANES * c:LANES * (c + 1)] = w2s_ref[c].astype(BF16)

    @pl.when(used)
    def _():
        pltpu.make_async_copy(hp_hbm.at[pl.ds(0, tm * SUBLANES), :], xbuf.at[slot], sem.at[slot]).wait()
        xs = xbuf.at[slot]
        x = jnp.concatenate([xs[pl.ds(c, tm, stride=SUBLANES), :] for c in range(SUBLANES)],
                            axis=1).astype(BF16)
        even = lax.broadcasted_iota(jnp.int32, (tm, LANES), 1) % 2 == 0
        for q in range(chunks):
            cs = slice(2 * LANES * q, 2 * LANES * (q + 1))
            hq = _dot(x, w1b_ref[:, cs]) + b1_ref[0, 0, :, cs]
            lo, hi = hq[:, :LANES], hq[:, LANES:]
            glu = jnp.where(even, lo, pltpu.roll(hi, 1, axis=1))
            lin = jnp.where(even, pltpu.roll(lo, LANES - 1, axis=1), hi)
            glu = jnp.minimum(glu, SWIGLU_LIMIT)
            lin = jnp.clip(lin, -SWIGLU_LIMIT, SWIGLU_LIMIT)
            act = glu * _sigmoid(SWIGLU_ALPHA * glu) * (lin + 1.0)
            act_ref[:, LANES * q:LANES * (q + 1)] = act.astype(BF16)
        y = _dot(act_ref[...], w2b_ref[...]) + b2_ref[0, 0]
        for c in range(SUBLANES):
            y_ref[pl.ds(c, tm, stride=SUBLANES), :] = y[:, LANES * c:LANES * (c + 1)]

    @pl.when(jnp.logical_not(used))
    def _():
        y_ref[...] = jnp.zeros(y_ref.shape, y_ref.dtype)


def _experts(layer, block_expert, n_used, row_token, hp, w1, b1, w2, b2):
    _, E, D, F2 = w1.shape
    assert D == SUBLANES * LANES
    F = F2 // 2
    tm = MOE_TM
    n_rows = row_token.shape[0]
    nb = n_rows // tm
    rows = row_token.reshape(nb, 1, tm)
    wmap = lambda i, be, nu: (layer, be[i], 0, 0)
    return pl.pallas_call(
        functools.partial(_expert_kernel, tm=tm),
        out_shape=jax.ShapeDtypeStruct((n_rows * SUBLANES, LANES), F32),
        grid_spec=pltpu.PrefetchScalarGridSpec(
            num_scalar_prefetch=2,
            grid=(nb,),
            in_specs=[
                pl.BlockSpec((1, 1, tm), lambda i, be, nu: (i, 0, 0), memory_space=pltpu.SMEM),
                pl.BlockSpec((1, 1, tm), lambda i, be, nu: (jnp.minimum(i + 1, nb - 1), 0, 0),
                             memory_space=pltpu.SMEM),
                pl.BlockSpec(memory_space=pl.ANY),
                pl.BlockSpec((1, 1, D, F2), wmap),
                pl.BlockSpec((1, 1, 1, F2), wmap),
                pl.BlockSpec((1, 1, F, D), wmap),
                pl.BlockSpec((1, 1, 1, D), wmap),
            ],
            out_specs=pl.BlockSpec((tm * SUBLANES, LANES), lambda i, be, nu: (i, 0)),
            scratch_shapes=[pltpu.VMEM((2, tm * SUBLANES, LANES), F32), pltpu.SemaphoreType.DMA((2,)),
                            pltpu.VMEM((D, F2), BF16), pltpu.VMEM((D // LANES, F, LANES), F32),
                            pltpu.VMEM((F, D), BF16),
                            pltpu.VMEM((tm, F), BF16)],
        ),
        compiler_params=_params("arbitrary"),
        name="moe_experts",
    )(block_expert, n_used, rows, rows, hp, w1, b1, w2, b2)


def _combine_kernel(dest_ref, dnext_ref, y_hbm, pt_ref, x_ref, mod_ref, o_ref, ybuf, sem, *, tm):
    n = TOP_K * tm
    i = pl.program_id(0)
    slot = i % 2

    def gather(idx_ref, s):
        def issue(r, carry):
            src = pl.multiple_of(idx_ref[0, 0, r] * SUBLANES, SUBLANES)
            dst = pl.multiple_of(r * SUBLANES, SUBLANES)
            pltpu.make_async_copy(y_hbm.at[pl.ds(src, SUBLANES), :], ybuf.at[s, pl.ds(dst, SUBLANES), :],
                                  sem.at[s]).start()
            return carry

        lax.fori_loop(0, n, issue, 0, unroll=8)

    @pl.when(i == 0)
    def _():
        gather(dest_ref, 0)

    for s in range(2):
        @pl.when((i + 1 < pl.num_programs(0)) & (slot != s))
        def _(s=s):
            gather(dnext_ref, s)

    pltpu.make_async_copy(y_hbm.at[pl.ds(0, n * SUBLANES), :], ybuf.at[slot], sem.at[slot]).wait()

    rows = ybuf.at[slot]
    for c in range(SUBLANES):
        cs = slice(LANES * c, LANES * (c + 1))
        acc = rows[pl.ds(c, tm, stride=SUBLANES), :] * pt_ref[:, 0:1]
        for k in range(1, TOP_K):
            acc = acc + rows[pl.ds(k * tm * SUBLANES + c, tm, stride=SUBLANES), :] * pt_ref[:, k:k + 1]
        o_ref[:, cs] = x_ref[:, cs] + mod_ref[0, 5:6, cs] * acc


def _combine(dest, y, pt, x2, mod, seq):
    T, D = x2.shape
    tm = min(COMB_TM, seq)
    per_b = seq // tm
    nb = T // tm
    rows = _tile_rows(dest, tm)
    return pl.pallas_call(
        functools.partial(_combine_kernel, tm=tm),
        out_shape=jax.ShapeDtypeStruct((T, D), F32),
        grid=(nb,),
        in_specs=[
            pl.BlockSpec((1, 1, TOP_K * tm), lambda i: (i, 0, 0), memory_space=pltpu.SMEM),
            pl.BlockSpec((1, 1, TOP_K * tm), lambda i: (jnp.minimum(i + 1, nb - 1), 0, 0),
                         memory_space=pltpu.SMEM),
            pl.BlockSpec(memory_space=pl.ANY),
            pl.BlockSpec((tm, LANES), lambda i: (i, 0)),
            pl.BlockSpec((tm, D), lambda i: (i, 0)),
            pl.BlockSpec((1, 6, D), lambda i: (i // per_b, 0, 0)),
        ],
        out_specs=pl.BlockSpec((tm, D), lambda i: (i, 0)),
        scratch_shapes=[pltpu.VMEM((2, TOP_K * tm * SUBLANES, LANES), F32), pltpu.SemaphoreType.DMA((2,))],
        compiler_params=_params("arbitrary"),
        name="moe_combine",
    )(rows, rows, y, pt, x2, mod)


def _t5_bucket(dist):
    exact = REL_BUCKETS // 2
    log_ratio = jnp.log(jnp.maximum(dist, 1).astype(F32) / exact) / math.log(REL_MAX_DIST / exact)
    large = exact + (log_ratio * (REL_BUCKETS - exact)).astype(jnp.int32)
    return jnp.where(dist < exact, dist, jnp.minimum(large, REL_BUCKETS - 1))


def _diff_bias_tiles(rel_bias, tb):
    table = rel_bias[:, :DIFF_HEADS].astype(F32)
    shifted = (table - table[REL_BUCKETS - 1]) * LOG2E
    kk = jnp.arange(tb)[:, None]
    qq = jnp.arange(tb)[None, :]

    def tile(dist):
        return _bucket_lookup(shifted, _t5_bucket(jnp.maximum(dist, 0)))

    diag = jnp.where((qq - kk >= 0)[None], tile(qq - kk), NEG_INF)
    prev = tile(qq + tb - kk)
    return prev, diag


def _bucket_lookup(table, bucket):
    out = jnp.zeros((table.shape[1],) + bucket.shape, F32)
    for b in range(table.shape[0]):
        out = out + jnp.where(bucket[None] == b, table[b].reshape((-1,) + (1,) * bucket.ndim), 0.0)
    return out


def _swa_bias_tiles(rel_bias):
    qi = jnp.arange(WINDOW)[:, None]
    kj = jnp.arange(2 * WINDOW)[None, :]
    dist = WINDOW + qi - kj
    ok = (dist >= 0) & (dist < WINDOW)
    bias = _bucket_lookup(rel_bias[:, DIFF_HEADS:].astype(F32),
                          _t5_bucket(jnp.clip(dist, 0, WINDOW - 1)))
    bias = jnp.where(ok[None], bias, NEG_INF)
    bias = bias.reshape(SWA_KV_HEADS, SWA_GROUP * WINDOW, 2 * WINDOW)
    return bias[..., :WINDOW], bias[..., WINDOW:]


def _block_diag(w):
    nb, n, _ = w.shape
    eye = jnp.eye(nb, dtype=w.dtype)
    return (eye[:, None, :, None] * w[:, :, None, :]).reshape(nb * n, nb * n)


def _routing(counts, top_e, rank, tm):
    K, T = top_e.shape
    counts = counts.astype(jnp.int32)
    padded = (counts + tm - 1) // tm * tm
    pad_ends = jnp.cumsum(padded)
    pad_starts = pad_ends - padded
    onehot = top_e[..., None] == jnp.arange(N_EXPERTS, dtype=jnp.int32)
    dest = rank + jnp.sum(jnp.where(onehot, pad_starts, 0), axis=-1)
    n_rows = K * T + N_EXPERTS * tm
    nb = n_rows // tm
    n_used = pad_ends[-1] // tm
    blk = jnp.minimum(jnp.arange(nb), n_used - 1) * tm
    block_expert = jnp.sum(blk[:, None] >= pad_ends[None, :], axis=-1)
    fill_ends = jnp.cumsum(padded - counts)
    fill_expert = jnp.sum(jnp.arange(N_EXPERTS * tm)[:, None] >= fill_ends[None, :], axis=-1)
    keys = jnp.concatenate([(top_e * T + jnp.arange(T, dtype=jnp.int32)[None, :]).reshape(-1),
                            (fill_expert * T + (T - 1)).astype(jnp.int32)])
    row_token = jnp.sort(keys) % T
    return (block_expert.astype(jnp.int32), n_used.astype(jnp.int32).reshape(1), dest.astype(jnp.int32),
            row_token.astype(jnp.int32))


def kernel(x, c, w_ada, b_ada, norm_mix, norm_ffn, w_in, conv_w, conv_b, lru_wa, lru_ba, lru_wx, lru_bx,
           lru_lambda, diff_qnorm, diff_knorm, diff_lambda, diff_subln, swa_qnorm, swa_knorm, swa_sinks,
           rel_bias, w_branch, w_out, w_router, b_router, w1, b1, w2, b2):
    B, S, D = x.shape
    L = w_ada.shape[0]
    T = B * S
    tb = min(ATT_TB, S)

    mods = _adaln(c, w_ada, b_ada)
    bprev, bdiag = _diff_bias_tiles(rel_bias, tb)
    sbias_prev, sbias_cur = _swa_bias_tiles(rel_bias)
    gidx = jnp.arange(512) // DIFF_HD
    gmat = jnp.where(gidx[:, None] == gidx[None, :], 1.0 / DIFF_HD, 0.0).astype(BF16)

    x2 = x.reshape(T, D)
    for l in range(L):
        lambda_init = 0.8 - 0.6 * math.exp(-0.3 * l)
        mod = mods[l]
        qkg = jnp.stack([
            jnp.tile(diff_qnorm[l], 8) * (DIFF_HD ** -0.5 * LOG2E),
            jnp.tile(diff_knorm[l], 8),
            jnp.tile(swa_qnorm[l], 8) * SWA_HD ** -0.5,
            jnp.tile(swa_knorm[l], 8),
        ]).astype(F32)
        w_in_l = w_in[l].astype(BF16)
        xg, q1, q2, dk, dvt, sq, sk, sv, gl = _inproj(
            x2, mod, norm_mix[l].reshape(1, D), w_in_l, w_in_l[:, 2048:2560].T, gmat, qkg, S)

        wg = jnp.concatenate([_block_diag(lru_wa[l]), _block_diag(lru_wx[l])], axis=1).astype(BF16)
        bg = jnp.concatenate([lru_ba[l], lru_bx[l]]).reshape(1, 2 * LRU_WIDTH)
        sp = jax.nn.softplus(-lru_lambda[l].astype(F32)).reshape(1, LRU_WIDTH)
        o_lru = _rglru(xg, conv_w[l], conv_b[l].reshape(1, LRU_WIDTH), wg, bg, sp, B, S)

        o_diff = _diff_attn(q1, q2, dk, dvt, bprev, bdiag, diff_lambda[l],
                            diff_subln[l].reshape(2 * DIFF_HD, 1), lambda_init, B, S)

        sinks = jnp.repeat(swa_sinks[l].astype(F32).reshape(SWA_KV_HEADS, SWA_GROUP), WINDOW, axis=1)
        o_swa = _swa(sq, sk, sv, sbias_prev, sbias_cur, sinks.reshape(SWA_KV_HEADS, -1, 1), B, S)

        x2 = _merge(x2, mod, o_lru, o_diff, o_swa, gl, w_branch[l].astype(BF16), w_out[l].astype(BF16), S)

        hp, top_e, pt, rank, counts = _router(x2, mod, norm_ffn[l].reshape(1, D), w_router[l].T,
                                              b_router[l].reshape(N_EXPERTS, 1), S)
        block_expert, n_used, dest, row_token = _routing(counts[:, 0], top_e, rank, MOE_TM)
        y = _experts(l, block_expert, n_used, row_token, hp, w1, b1[:, :, None, :], w2, b2[:, :, None, :])
        x2 = _combine(dest, y, pt, x2, mod, S)
    return x2.reshape(B, S, D)
```

```python
import functools
import math

import jax
import jax.numpy as jnp
from jax import lax
from jax.experimental import pallas as pl
from jax.experimental.pallas import tpu as pltpu

F32 = jnp.float32
BF16 = jnp.bfloat16

LRU_WIDTH = 512
LRU_BLOCKS = 8
LRU_C = 8.0
CONV_W = 4
DIFF_HEADS = 4
DIFF_HD = 64
SWA_HEADS = 8
SWA_KV_HEADS = 2
SWA_GROUP = SWA_HEADS // SWA_KV_HEADS
SWA_HD = 64
WINDOW = 128
N_BRANCHES = 3
BRANCH_WIDTH = 512
REL_BUCKETS = 32
REL_MAX_DIST = 128
N_EXPERTS = 32
TOP_K = 4
SWIGLU_LIMIT = 7.0
SWIGLU_ALPHA = 1.702
NORM_EPS = 1e-6
NEG_INF = -1e30
LOG2E = math.log2(math.e)

VMEM_LIMIT_BYTES = 56 * 1024 * 1024
LANES = 128
SUBLANES = 8

ADALN_TN = 1536
PROJ_TM = 512
LRU_TS = 512
ATT_TB = 512
FAR_GROUP = 4
SWA_TS = 512
MERGE_TM = 512
ROUTER_TM = 512
MOE_TM = 512
COMB_TM = 256

HIGHEST = lax.Precision.HIGHEST


def _params(*sem):
    return pltpu.CompilerParams(dimension_semantics=sem, vmem_limit_bytes=VMEM_LIMIT_BYTES)


def _dot(a, b, **kw):
    return jnp.dot(a, b, preferred_element_type=F32, **kw)


def _dot_nt(a, b, **kw):
    return lax.dot_general(a, b, (((1,), (1,)), ((), ())), preferred_element_type=F32, **kw)


def _sigmoid(x):
    return 1.0 / (1.0 + jnp.exp(-x))


def _adaln_kernel(c_ref, w_ref, b_ref, o_ref):
    c = c_ref[...]
    cond = c * _sigmoid(c)
    o_ref[0] = _dot(cond, w_ref[0], precision=HIGHEST) + b_ref[0]


def _adaln(c, w_ada, b_ada):
    L, D, N = w_ada.shape
    B = c.shape[0]
    rows = 8
    cp = jnp.zeros((rows, D), F32).at[:B].set(c)
    out = pl.pallas_call(
        _adaln_kernel,
        out_shape=jax.ShapeDtypeStruct((L, rows, N), F32),
        grid=(L, N // ADALN_TN),
        in_specs=[
            pl.BlockSpec((rows, D), lambda l, j: (0, 0)),
            pl.BlockSpec((1, D, ADALN_TN), lambda l, j: (l, 0, j)),
            pl.BlockSpec((1, 1, ADALN_TN), lambda l, j: (l, 0, j)),
        ],
        out_specs=pl.BlockSpec((1, rows, ADALN_TN), lambda l, j: (l, 0, j)),
        compiler_params=_params("arbitrary", "arbitrary"),
        name="adaln",
    )(cp, w_ada, b_ada.reshape(L, 1, N))
    return out[:, :B].reshape(L, B, 6, D)


def _rms_mod(x, gain, scale, shift):
    ms = jnp.mean(x * x, axis=-1, keepdims=True)
    return (x * lax.rsqrt(ms + NORM_EPS) * gain) * (1.0 + scale) + shift


def _group_rms(x, gmat):
    ms = _dot((x * x).astype(BF16), gmat)
    return x * lax.rsqrt(ms + NORM_EPS)


def _inproj_kernel(x_ref, mod_ref, gain_ref, w_ref, wvt_ref, gmat_ref, qkg_ref,
                   xg_ref, q1_ref, q2_ref, dk_ref, dvt_ref, sq_ref, sk_ref, sv_ref, gl_ref):
    x = x_ref[...]
    h = _rms_mod(x, gain_ref[...], mod_ref[0, 1:2, :], mod_ref[0, 0:1, :]).astype(BF16)
    gmat = gmat_ref[...]

    xg_ref[...] = _dot(h, w_ref[:, 0:1024]).astype(BF16)

    dd = _dot(h, w_ref[:, 1024:2048])
    qn = _group_rms(dd[:, 0:512], gmat) * qkg_ref[0:1, :]
    kn = _group_rms(dd[:, 512:1024], gmat) * qkg_ref[1:2, :]
    lane = lax.broadcasted_iota(jnp.int32, qn.shape, 1) % LANES
    q1_ref[...] = jnp.where(lane < DIFF_HD, qn, 0.0).astype(BF16)
    q2_ref[...] = jnp.where(lane >= DIFF_HD, qn, 0.0).astype(BF16)
    dk_ref[...] = kn.astype(BF16)
    dvt_ref[0] = _dot_nt(wvt_ref[...], h).astype(BF16)

    ss = _dot(h, w_ref[:, 2560:3328])
    sq_ref[...] = (_group_rms(ss[:, 0:512], gmat) * qkg_ref[2:3, :]).astype(BF16)
    sk_ref[...] = (_group_rms(ss[:, 512:640], gmat[0:128, 0:128]) * qkg_ref[3:4, 0:128]).astype(BF16)
    sv_ref[...] = ss[:, 640:768].astype(BF16)

    gl_ref[...] = _dot(h, w_ref[:, 3328:6400]).astype(BF16)


def _inproj(x2, mod, gain, w_in, wvt, gmat, qkg, seq):
    T, D = x2.shape
    tm = min(PROJ_TM, seq)
    per_b = seq // tm
    VW = wvt.shape[0]
    widths = (1024, 512, 512, 512, None, 512, 128, 128, 3072)
    row = lambda i: (i, 0)
    const = lambda i: (0, 0)
    vt_shape = jax.ShapeDtypeStruct((T // seq, VW, seq), BF16)
    vt_spec = pl.BlockSpec((1, VW, tm), lambda i: (i // per_b, 0, i % per_b))
    return pl.pallas_call(
        _inproj_kernel,
        out_shape=[vt_shape if w is None else jax.ShapeDtypeStruct((T, w), BF16) for w in widths],
        grid=(T // tm,),
        in_specs=[
            pl.BlockSpec((tm, D), row),
            pl.BlockSpec((1, 6, D), lambda i: (i // per_b, 0, 0)),
            pl.BlockSpec((1, D), const),
            pl.BlockSpec(w_in.shape, const, pipeline_mode=pl.Buffered(1)),
            pl.BlockSpec(wvt.shape, const),
            pl.BlockSpec(gmat.shape, const),
            pl.BlockSpec(qkg.shape, const),
        ],
        out_specs=[vt_spec if w is None else pl.BlockSpec((tm, w), row) for w in widths],
        compiler_params=_params("arbitrary"),
        name="inproj",
    )(x2, mod, gain, w_in, wvt, gmat, qkg)


def _rglru_kernel(xg_ref, cw_ref, cb_ref, wg_ref, bg_ref, sp_ref, o_ref, ext_ref, hc_ref):
    ts = xg_ref.shape[0]
    C = LRU_WIDTH

    @pl.when(pl.program_id(1) == 0)
    def _():
        ext_ref[0:8, :] = jnp.zeros((8, C), F32)
        hc_ref[...] = jnp.zeros_like(hc_ref)

    xr = xg_ref[:, 0:C].astype(F32)
    ext_ref[8:8 + ts, :] = xr
    xc = cb_ref[...] + xr * cw_ref[CONV_W - 1:CONV_W, :]
    for back in range(1, CONV_W):
        tap = CONV_W - 1 - back
        xc = xc + ext_ref[8 - back:8 - back + ts, :] * cw_ref[tap:tap + 1, :]
    ext_ref[0:8, :] = xr[ts - 8:ts, :]

    gates = _dot(xc.astype(BF16), wg_ref[...]) + bg_ref[...]
    r = _sigmoid(gates[:, 0:C])
    gi = _sigmoid(gates[:, C:2 * C])
    log_a = (-LRU_C) * r * sp_ref[...]
    a = jnp.exp(log_a)
    b = xc * gi * jnp.sqrt(1.0 - a * a)

    rows = lax.broadcasted_iota(jnp.int32, (ts, C), 0)
    d = 1
    while d < ts:
        keep = rows >= d
        a_sh = pltpu.roll(a, d, axis=0)
        b_sh = pltpu.roll(b, d, axis=0)
        b = jnp.where(keep, a * b_sh + b, b)
        a = jnp.where(keep, a * a_sh, a)
        d *= 2
    h = b + a * hc_ref[...]
    hc_ref[...] = h[ts - 1:ts, :]

    gr = xg_ref[:, C:2 * C].astype(F32)
    gelu = 0.5 * gr * (1.0 + jnp.tanh(math.sqrt(2.0 / math.pi) * (gr + 0.044715 * gr * gr * gr)))
    o_ref[...] = (h * gelu).astype(BF16)


def _rglru(xg, conv_w, conv_b, wg, bg, softplus_neg_lam, batch, seq):
    T = xg.shape[0]
    C = LRU_WIDTH
    ts = min(LRU_TS, seq)
    per_b = seq // ts
    const = lambda b, i: (0, 0)
    return pl.pallas_call(
        _rglru_kernel,
        out_shape=jax.ShapeDtypeStruct((T, C), BF16),
        grid=(batch, per_b),
        in_specs=[
            pl.BlockSpec((ts, 2 * C), lambda b, i: (b * per_b + i, 0)),
            pl.BlockSpec((CONV_W, C), const),
            pl.BlockSpec((1, C), const),
            pl.BlockSpec((C, 2 * C), const),
            pl.BlockSpec((1, 2 * C), const),
            pl.BlockSpec((1, C), const),
        ],
        out_specs=pl.BlockSpec((ts, C), lambda b, i: (b * per_b + i, 0)),
        scratch_shapes=[pltpu.VMEM((ts + 8, C), F32), pltpu.VMEM((1, C), F32)],
        compiler_params=_params("arbitrary", "arbitrary"),
        name="rglru",
    )(xg, conv_w, conv_b, wg, bg, softplus_neg_lam)


def _diff_attn_kernel(q1_ref, q2_ref, k_ref, vt_ref, bprev_ref, bdiag_ref, lam_ref, sub_ref, o_ref,
                      m_ref, l_ref, acc_ref, *, tb, lambda_init):
    i = pl.program_id(2)
    q = (q1_ref[0], q2_ref[0])

    m_ref[...] = jnp.full(m_ref.shape, NEG_INF, F32)
    l_ref[...] = jnp.zeros(l_ref.shape, F32)
    acc_ref[...] = jnp.zeros(acc_ref.shape, F32)

    def attend(start, tk, bias):
        kb = k_ref[0, pl.ds(start, tk), :]
        vb = vt_ref[0, :, pl.ds(start, tk)]
        scores = [_dot_nt(kb, q[mp]) for mp in range(2)]
        for mp in range(2):
            s = scores[mp]
            if bias is not None:
                s = s + bias
            m_old = m_ref[mp]
            m_new = jnp.maximum(m_old, jnp.max(s, axis=0, keepdims=True))
            alpha = jnp.exp2(m_old - m_new)
            p = jnp.exp2(s - m_new)
            l_ref[mp] = alpha * l_ref[mp] + jnp.sum(p, axis=0, keepdims=True)
            acc_ref[mp] = alpha * acc_ref[mp] + _dot(vb, p.astype(BF16))
            m_ref[mp] = m_new

    n_far = jnp.maximum(i - 1, 0)
    group = FAR_GROUP * tb

    def far(j, carry):
        attend(pl.multiple_of(j * group, group), group, None)
        return carry

    lax.fori_loop(0, n_far // FAR_GROUP, far, 0)

    @pl.when(n_far % FAR_GROUP >= 2)
    def _():
        attend(pl.multiple_of((n_far // FAR_GROUP) * group, 2 * tb), 2 * tb, None)

    @pl.when(n_far % 2 == 1)
    def _():
        attend(pl.multiple_of((n_far - 1) * tb, tb), tb, None)

    @pl.when(i > 0)
    def _():
        attend(pl.multiple_of((i - 1) * tb, tb), tb, bprev_ref[0])

    attend(pl.multiple_of(i * tb, tb), tb, bdiag_ref[0])

    lv = lam_ref[...]
    lam = (jnp.exp(jnp.sum(lv[0:1] * lv[1:2], axis=-1, keepdims=True))
           - jnp.exp(jnp.sum(lv[2:3] * lv[3:4], axis=-1, keepdims=True)) + lambda_init)
    o = acc_ref[0] / l_ref[0] - lam * (acc_ref[1] / l_ref[1])
    ms = jnp.mean(o * o, axis=0, keepdims=True)
    o = o * lax.rsqrt(ms + NORM_EPS) * sub_ref[...] * (1.0 - lambda_init)
    o_ref[0] = o.T.astype(BF16)


def _diff_attn(q1, q2, k, vt, bprev, bdiag, lam_vecs, subln, lambda_init, batch, seq):
    W = DIFF_HEADS * 2 * DIFF_HD
    tb = min(ATT_TB, seq)
    shp = (batch, seq, W)
    qspec = pl.BlockSpec((1, tb, LANES), lambda b, h, i: (b, i, h))
    kspec = pl.BlockSpec((1, seq, LANES), lambda b, h, i: (b, 0, h))
    vspec = pl.BlockSpec((1, LANES, seq), lambda b, h, i: (b, h, 0))
    bspec = pl.BlockSpec((1, tb, tb), lambda b, h, i: (h, 0, 0))
    const = lambda b, h, i: (0, 0)
    out = pl.pallas_call(
        functools.partial(_diff_attn_kernel, tb=tb, lambda_init=lambda_init),
        out_shape=jax.ShapeDtypeStruct(shp, BF16),
        grid=(batch, DIFF_HEADS, seq // tb),
        in_specs=[qspec, qspec, kspec, vspec, bspec, bspec,
                  pl.BlockSpec((4, DIFF_HD), const), pl.BlockSpec((2 * DIFF_HD, 1), const)],
        out_specs=qspec,
        scratch_shapes=[pltpu.VMEM((2, 1, tb), F32), pltpu.VMEM((2, 1, tb), F32),
                        pltpu.VMEM((2, LANES, tb), F32)],
        compiler_params=_params("arbitrary", "arbitrary", "arbitrary"),
        name="diff_attn",
    )(q1.reshape(shp), q2.reshape(shp), k.reshape(shp), vt, bprev, bdiag, lam_vecs, subln)
    return out.reshape(batch * seq, W)


def _swa_kernel(q_ref, k_ref, v_ref, bp_ref, bc_ref, sink_ref, o_ref, *, ts):
    i = pl.program_id(1)
    blk = WINDOW
    for sub in range(ts // blk):
        start = i * ts + sub * blk
        has_prev = start > 0
        pstart = pl.multiple_of(jnp.maximum(start - blk, 0), blk)
        cstart = pl.multiple_of(start, blk)
        kp = k_ref[0, pl.ds(pstart, blk), :]
        kc = k_ref[0, pl.ds(cstart, blk), :]
        vp = v_ref[0, pl.ds(pstart, blk), :]
        vc = v_ref[0, pl.ds(cstart, blk), :]
        qs = q_ref[0, sub * blk:(sub + 1) * blk, :]
        outs = []
        for hk in range(SWA_KV_HEADS):
            c0 = hk * SWA_GROUP * SWA_HD
            qh = jnp.concatenate(
                [qs[:, c0 + g * SWA_HD:c0 + (g + 1) * SWA_HD] for g in range(SWA_GROUP)], axis=0)
            ksl = slice(hk * SWA_HD, (hk + 1) * SWA_HD)
            kb = jnp.concatenate([kp[:, ksl], kc[:, ksl]], axis=0)
            vb = jnp.concatenate([vp[:, ksl], vc[:, ksl]], axis=0)
            s = _dot_nt(qh, kb) + jnp.concatenate([bp_ref[hk], bc_ref[hk]], axis=1)
            col = lax.broadcasted_iota(jnp.int32, s.shape, 1)
            s = jnp.where(jnp.logical_or(has_prev, col >= blk), s, NEG_INF)
            sink = sink_ref[hk]
            m = jnp.maximum(jnp.max(s, axis=-1, keepdims=True), sink)
            p = jnp.exp(s - m)
            den = jnp.sum(p, axis=-1, keepdims=True) + jnp.exp(sink - m)
            o = _dot(p.astype(BF16), vb) / den
            outs.extend(o[g * blk:(g + 1) * blk, :] for g in range(SWA_GROUP))
        o_ref[0, sub * blk:(sub + 1) * blk, :] = jnp.concatenate(outs, axis=1).astype(BF16)


def _swa(q, k, v, bias_prev, bias_cur, sinks, batch, seq):
    ts = min(SWA_TS, seq)
    WQ = SWA_HEADS * SWA_HD
    WK = SWA_KV_HEADS * SWA_HD
    rows = SWA_GROUP * WINDOW
    const3 = lambda b, i: (0, 0, 0)
    out = pl.pallas_call(
        functools.partial(_swa_kernel, ts=ts),
        out_shape=jax.ShapeDtypeStruct((batch, seq, WQ), BF16),
        grid=(batch, seq // ts),
        in_specs=[
            pl.BlockSpec((1, ts, WQ), lambda b, i: (b, i, 0)),
            pl.BlockSpec((1, seq, WK), lambda b, i: (b, 0, 0)),
            pl.BlockSpec((1, seq, WK), lambda b, i: (b, 0, 0)),
            pl.BlockSpec((SWA_KV_HEADS, rows, WINDOW), const3),
            pl.BlockSpec((SWA_KV_HEADS, rows, WINDOW), const3),
            pl.BlockSpec((SWA_KV_HEADS, rows, 1), const3),
        ],
        out_specs=pl.BlockSpec((1, ts, WQ), lambda b, i: (b, i, 0)),
        compiler_params=_params("arbitrary", "arbitrary"),
        name="swa_attn",
    )(q.reshape(batch, seq, WQ), k.reshape(batch, seq, WK), v.reshape(batch, seq, WK),
      bias_prev, bias_cur, sinks)
    return out.reshape(batch * seq, WQ)


def _merge_kernel(x_ref, mod_ref, lru_ref, diff_ref, swa_ref, gl_ref, wb_ref, wo_ref, o_ref):
    D = x_ref.shape[1]
    merged = None
    for n, br in enumerate((lru_ref, diff_ref, swa_ref)):
        gate = _sigmoid(gl_ref[:, n * D:(n + 1) * D].astype(F32))
        term = gate * _dot(br[...], wb_ref[n])
        merged = term if merged is None else merged + term
    out = _dot(merged.astype(BF16), wo_ref[...])
    o_ref[...] = x_ref[...] + mod_ref[0, 2:3, :] * out


def _merge(x2, mod, o_lru, o_diff, o_swa, gl, w_branch, w_out, seq):
    T, D = x2.shape
    tm = min(MERGE_TM, seq)
    per_b = seq // tm
    row = lambda i: (i, 0)
    return pl.pallas_call(
        _merge_kernel,
        out_shape=jax.ShapeDtypeStruct((T, D), F32),
        grid=(T // tm,),
        in_specs=[
            pl.BlockSpec((tm, D), row),
            pl.BlockSpec((1, 6, D), lambda i: (i // per_b, 0, 0)),
            pl.BlockSpec((tm, BRANCH_WIDTH), row),
            pl.BlockSpec((tm, BRANCH_WIDTH), row),
            pl.BlockSpec((tm, BRANCH_WIDTH), row),
            pl.BlockSpec((tm, N_BRANCHES * D), row),
            pl.BlockSpec(w_branch.shape, lambda i: (0, 0, 0)),
            pl.BlockSpec(w_out.shape, lambda i: (0, 0)),
        ],
        out_specs=pl.BlockSpec((tm, D), row),
        compiler_params=_params("arbitrary"),
        name="merge_outproj",
    )(x2, mod, o_lru, o_diff, o_swa, gl, w_branch, w_out)


def _router_kernel(x_ref, mod_ref, gain_ref, wr_ref, br_ref, tri_ref,
                   hp_ref, e_ref, pt_ref, rank_ref, cnt_ref, run_ref):
    @pl.when(pl.program_id(0) == 0)
    def _():
        run_ref[...] = jnp.zeros_like(run_ref)

    x = x_ref[...]
    tm, D = x.shape
    h = _rms_mod(x, gain_ref[...], mod_ref[0, 4:5, :], mod_ref[0, 3:4, :])
    for c in range(D // LANES):
        hp_ref[pl.ds(c, tm, stride=SUBLANES), :] = h[:, LANES * c:LANES * (c + 1)]

    logits = _dot_nt(wr_ref[...], h, precision=HIGHEST) + br_ref[...]
    eidx = lax.broadcasted_iota(jnp.int32, logits.shape, 0)
    vals, idxs, hots = [], [], []
    for _ in range(TOP_K):
        mx = jnp.max(logits, axis=0, keepdims=True)
        sel = jnp.min(jnp.where(logits == mx, eidx, N_EXPERTS), axis=0, keepdims=True)
        hot = eidx == sel
        vals.append(mx)
        idxs.append(sel)
        hots.append(hot)
        logits = jnp.where(hot, -jnp.inf, logits)
    e_ref[...] = jnp.concatenate(idxs, axis=0)
    top_v = jnp.concatenate(vals, axis=0)
    ex = jnp.exp(top_v - top_v[0:1])
    p = ex / jnp.sum(ex, axis=0, keepdims=True)
    pt_ref[...] = jnp.concatenate([p, jnp.zeros((LANES - TOP_K, tm), F32)], axis=0).T

    member = hots[0]
    for hot in hots[1:]:
        member = member | hot
    member = jnp.where(member, 1.0, 0.0)
    before = _dot(member.astype(BF16), tri_ref[...]) + run_ref[:, 0:1]
    ranks = [jnp.sum(jnp.where(hot, before, 0.0), axis=0, keepdims=True) for hot in hots]
    rank_ref[...] = jnp.concatenate(ranks, axis=0).astype(jnp.int32)
    run_ref[...] = run_ref[...] + jnp.sum(member, axis=1, keepdims=True)
    cnt_ref[...] = run_ref[...]


def _router(x2, mod, gain, w_router_t, b_router, seq):
    T, D = x2.shape
    tm = min(ROUTER_TM, seq)
    per_b = seq // tm
    tri = (jnp.arange(tm)[:, None] < jnp.arange(tm)[None, :]).astype(BF16)
    const = lambda i: (0, 0)
    return pl.pallas_call(
        _router_kernel,
        out_shape=[jax.ShapeDtypeStruct((T * SUBLANES, LANES), F32),
                   jax.ShapeDtypeStruct((TOP_K, T), jnp.int32),
                   jax.ShapeDtypeStruct((T, LANES), F32),
                   jax.ShapeDtypeStruct((TOP_K, T), jnp.int32),
                   jax.ShapeDtypeStruct((N_EXPERTS, LANES), F32)],
        grid=(T // tm,),
        in_specs=[
            pl.BlockSpec((tm, D), lambda i: (i, 0)),
            pl.BlockSpec((1, 6, D), lambda i: (i // per_b, 0, 0)),
            pl.BlockSpec((1, D), const),
            pl.BlockSpec((N_EXPERTS, D), const),
            pl.BlockSpec((N_EXPERTS, 1), const),
            pl.BlockSpec((tm, tm), const),
        ],
        out_specs=[pl.BlockSpec((tm * SUBLANES, LANES), lambda i: (i, 0)),
                   pl.BlockSpec((TOP_K, tm), lambda i: (0, i)),
                   pl.BlockSpec((tm, LANES), lambda i: (i, 0)),
                   pl.BlockSpec((TOP_K, tm), lambda i: (0, i)),
                   pl.BlockSpec((N_EXPERTS, LANES), const)],
        scratch_shapes=[pltpu.VMEM((N_EXPERTS, LANES), F32)],
        compiler_params=_params("arbitrary"),
        name="router",
    )(x2, mod, gain, w_router_t, b_router, tri)


def _tile_rows(idx, tm):
    K, T = idx.shape
    nb = T // tm
    return idx.reshape(K, nb, tm).transpose(1, 0, 2).reshape(nb, 1, K * tm)


def _expert_kernel(be_ref, nused_ref, tok_ref, tnext_ref, hp_hbm, w1_ref, b1_ref, w2_ref, b2_ref, y_ref,
                   xbuf, sem, w1b_ref, w2s_ref, w2b_ref, act_ref, *, tm):
    i = pl.program_id(0)
    slot = i % 2
    used = i < nused_ref[0]
    next_used = (i + 1 < nused_ref[0]) & (i + 1 < pl.num_programs(0))
    chunks = w1_ref.shape[3] // (2 * LANES)
    hl = LANES // 2

    def gather(idx_ref, s):
        def issue(r, carry):
            src = pl.multiple_of(idx_ref[0, 0, r] * SUBLANES, SUBLANES)
            dst = pl.multiple_of(r * SUBLANES, SUBLANES)
            pltpu.make_async_copy(hp_hbm.at[pl.ds(src, SUBLANES), :], xbuf.at[s, pl.ds(dst, SUBLANES), :],
                                  sem.at[s]).start()
            return carry

        lax.fori_loop(0, tm, issue, 0, unroll=8)

    @pl.when((i == 0) & used)
    def _():
        gather(tok_ref, 0)

    for s in range(2):
        @pl.when(next_used & (slot != s))
        def _(s=s):
            gather(tnext_ref, s)

    @pl.when(used & ((i == 0) | (be_ref[i] != be_ref[jnp.maximum(i - 1, 0)])))
    def _():
        for q in range(chunks):
            cs = slice(2 * LANES * q, 2 * LANES * (q + 1))
            w1b_ref[:, cs] = w1_ref[0, 0, :, cs].astype(BF16)
        cols = w2_ref.shape[3] // LANES
        for q in range(w2_ref.shape[2] // LANES):
            for s in range(2):
                rows = w2_ref[0, 0, LANES * q + hl * s:LANES * q + hl * (s + 1), :]
                for c in range(cols):
                    w2s_ref[c, pl.ds(LANES * q + s, hl, stride=2), :] = rows[:, LANES * c:LANES * (c + 1)]
        for c in range(cols):
            w2b_ref[:, LANES * c:LANES * (c + 1)] = w2s_ref[c].astype(BF16)

    @pl.when(used)
    def _():
        pltpu.make_async_copy(hp_hbm.at[pl.ds(0, tm * SUBLANES), :], xbuf.at[slot], sem.at[slot]).wait()
        xs = xbuf.at[slot]
        x = jnp.concatenate([xs[pl.ds(c, tm, stride=SUBLANES), :] for c in range(SUBLANES)],
                            axis=1).astype(BF16)
        even = lax.broadcasted_iota(jnp.int32, (tm, LANES), 1) % 2 == 0
        for q in range(chunks):
            cs = slice(2 * LANES * q, 2 * LANES * (q + 1))
            hq = _dot(x, w1b_ref[:, cs]) + b1_ref[0, 0, :, cs]
            lo, hi = hq[:, :LANES], hq[:, LANES:]
            glu = jnp.where(even, lo, pltpu.roll(hi, 1, axis=1))
            lin = jnp.where(even, pltpu.roll(lo, LANES - 1, axis=1), hi)
            glu = jnp.minimum(glu, SWIGLU_LIMIT)
            lin = jnp.clip(lin, -SWIGLU_LIMIT, SWIGLU_LIMIT)
            act = glu * _sigmoid(SWIGLU_ALPHA * glu) * (lin + 1.0)
            act_ref[:, LANES * q:LANES * (q + 1)] = act.astype(BF16)
        y = _dot(act_ref[...], w2b_ref[...]) + b2_ref[0, 0]
        for c in range(SUBLANES):
            y_ref[pl.ds(c, tm, stride=SUBLANES), :] = y[:, LANES * c:LANES * (c + 1)]

    @pl.when(jnp.logical_not(used))
    def _():
        y_ref[...] = jnp.zeros(y_ref.shape, y_ref.dtype)


def _experts(layer, block_expert, n_used, row_token, hp, w1, b1, w2, b2):
    _, E, D, F2 = w1.shape
    assert D == SUBLANES * LANES
    F = F2 // 2
    tm = MOE_TM
    n_rows = row_token.shape[0]
    nb = n_rows // tm
    rows = row_token.reshape(nb, 1, tm)
    wmap = lambda i, be, nu: (layer, be[i], 0, 0)
    return pl.pallas_call(
        functools.partial(_expert_kernel, tm=tm),
        out_shape=jax.ShapeDtypeStruct((n_rows * SUBLANES, LANES), F32),
        grid_spec=pltpu.PrefetchScalarGridSpec(
            num_scalar_prefetch=2,
            grid=(nb,),
            in_specs=[
                pl.BlockSpec((1, 1, tm), lambda i, be, nu: (i, 0, 0), memory_space=pltpu.SMEM),
                pl.BlockSpec((1, 1, tm), lambda i, be, nu: (jnp.minimum(i + 1, nb - 1), 0, 0),
                             memory_space=pltpu.SMEM),
                pl.BlockSpec(memory_space=pl.ANY),
                pl.BlockSpec((1, 1, D, F2), wmap),
                pl.BlockSpec((1, 1, 1, F2), wmap),
                pl.BlockSpec((1, 1, F, D), wmap),
                pl.BlockSpec((1, 1, 1, D), wmap),
            ],
            out_specs=pl.BlockSpec((tm * SUBLANES, LANES), lambda i, be, nu: (i, 0)),
            scratch_shapes=[pltpu.VMEM((2, tm * SUBLANES, LANES), F32), pltpu.SemaphoreType.DMA((2,)),
                            pltpu.VMEM((D, F2), BF16), pltpu.VMEM((D // LANES, F, LANES), F32),
                            pltpu.VMEM((F, D), BF16),
                            pltpu.VMEM((tm, F), BF16)],
        ),
        compiler_params=_params("arbitrary"),
        name="moe_experts",
    )(block_expert, n_used, rows, rows, hp, w1, b1, w2, b2)


def _combine_kernel(dest_ref, dnext_ref, y_hbm, pt_ref, x_ref, mod_ref, o_ref, ybuf, sem, *, tm):
    n = TOP_K * tm
    i = pl.program_id(0)
    slot = i % 2

    def gather(idx_ref, s):
        def issue(r, carry):
            src = pl.multiple_of(idx_ref[0, 0, r] * SUBLANES, SUBLANES)
            dst = pl.multiple_of(r * SUBLANES, SUBLANES)
            pltpu.make_async_copy(y_hbm.at[pl.ds(src, SUBLANES), :], ybuf.at[s, pl.ds(dst, SUBLANES), :],
                                  sem.at[s]).start()
            return carry

        lax.fori_loop(0, n, issue, 0, unroll=8)

    @pl.when(i == 0)
    def _():
        gather(dest_ref, 0)

    for s in range(2):
        @pl.when((i + 1 < pl.num_programs(0)) & (slot != s))
        def _(s=s):
            gather(dnext_ref, s)

    pltpu.make_async_copy(y_hbm.at[pl.ds(0, n * SUBLANES), :], ybuf.at[slot], sem.at[slot]).wait()

    rows = ybuf.at[slot]
    for c in range(SUBLANES):
        cs = slice(LANES * c, LANES * (c + 1))
        acc = rows[pl.ds(c, tm, stride=SUBLANES), :] * pt_ref[:, 0:1]
        for k in range(1, TOP_K):
            acc = acc + rows[pl.ds(k * tm * SUBLANES + c, tm, stride=SUBLANES), :] * pt_ref[:, k:k + 1]
        o_ref[:, cs] = x_ref[:, cs] + mod_ref[0, 5:6, cs] * acc


def _combine(dest, y, pt, x2, mod, seq):
    T, D = x2.shape
    tm = min(COMB_TM, seq)
    per_b = seq // tm
    nb = T // tm
    rows = _tile_rows(dest, tm)
    return pl.pallas_call(
        functools.partial(_combine_kernel, tm=tm),
        out_shape=jax.ShapeDtypeStruct((T, D), F32),
        grid=(nb,),
        in_specs=[
            pl.BlockSpec((1, 1, TOP_K * tm), lambda i: (i, 0, 0), memory_space=pltpu.SMEM),
            pl.BlockSpec((1, 1, TOP_K * tm), lambda i: (jnp.minimum(i + 1, nb - 1), 0, 0),
                         memory_space=pltpu.SMEM),
            pl.BlockSpec(memory_space=pl.ANY),
            pl.BlockSpec((tm, LANES), lambda i: (i, 0)),
            pl.BlockSpec((tm, D), lambda i: (i, 0)),
            pl.BlockSpec((1, 6, D), lambda i: (i // per_b, 0, 0)),
        ],
        out_specs=pl.BlockSpec((tm, D), lambda i: (i, 0)),
        scratch_shapes=[pltpu.VMEM((2, TOP_K * tm * SUBLANES, LANES), F32), pltpu.SemaphoreType.DMA((2,))],
        compiler_params=_params("arbitrary"),
        name="moe_combine",
    )(rows, rows, y, pt, x2, mod)


def _t5_bucket(dist):
    exact = REL_BUCKETS // 2
    log_ratio = jnp.log(jnp.maximum(dist, 1).astype(F32) / exact) / math.log(REL_MAX_DIST / exact)
    large = exact + (log_ratio * (REL_BUCKETS - exact)).astype(jnp.int32)
    return jnp.where(dist < exact, dist, jnp.minimum(large, REL_BUCKETS - 1))


def _diff_bias_tiles(rel_bias, tb):
    table = rel_bias[:, :DIFF_HEADS].astype(F32)
    shifted = (table - table[REL_BUCKETS - 1]) * LOG2E
    kk = jnp.arange(tb)[:, None]
    qq = jnp.arange(tb)[None, :]

    def tile(dist):
        return _bucket_lookup(shifted, _t5_bucket(jnp.maximum(dist, 0)))

    diag = jnp.where((qq - kk >= 0)[None], tile(qq - kk), NEG_INF)
    prev = tile(qq + tb - kk)
    return prev, diag


def _bucket_lookup(table, bucket):
    out = jnp.zeros((table.shape[1],) + bucket.shape, F32)
    for b in range(table.shape[0]):
        out = out + jnp.where(bucket[None] == b, table[b].reshape((-1,) + (1,) * bucket.ndim), 0.0)
    return out


def _swa_bias_tiles(rel_bias):
    qi = jnp.arange(WINDOW)[:, None]
    kj = jnp.arange(2 * WINDOW)[None, :]
    dist = WINDOW + qi - kj
    ok = (dist >= 0) & (dist < WINDOW)
    bias = _bucket_lookup(rel_bias[:, DIFF_HEADS:].astype(F32),
                          _t5_bucket(jnp.clip(dist, 0, WINDOW - 1)))
    bias = jnp.where(ok[None], bias, NEG_INF)
    bias = bias.reshape(SWA_KV_HEADS, SWA_GROUP * WINDOW, 2 * WINDOW)
    return bias[..., :WINDOW], bias[..., WINDOW:]


def _block_diag(w):
    nb, n, _ = w.shape
    eye = jnp.eye(nb, dtype=w.dtype)
    return (eye[:, None, :, None] * w[:, :, None, :]).reshape(nb * n, nb * n)


def _routing(counts, top_e, rank, tm):
    K, T = top_e.shape
    counts = counts.astype(jnp.int32)
    padded = (counts + tm - 1) // tm * tm
    pad_ends = jnp.cumsum(padded)
    pad_starts = pad_ends - padded
    onehot = top_e[..., None] == jnp.arange(N_EXPERTS, dtype=jnp.int32)
    dest = rank + jnp.sum(jnp.where(onehot, pad_starts, 0), axis=-1)
    n_rows = K * T + N_EXPERTS * tm
    nb = n_rows // tm
    n_used = pad_ends[-1] // tm
    blk = jnp.minimum(jnp.arange(nb), n_used - 1) * tm
    block_expert = jnp.sum(blk[:, None] >= pad_ends[None, :], axis=-1)
    fill_ends = jnp.cumsum(padded - counts)
    fill_expert = jnp.sum(jnp.arange(N_EXPERTS * tm)[:, None] >= fill_ends[None, :], axis=-1)
    keys = jnp.concatenate([(top_e * T + jnp.arange(T, dtype=jnp.int32)[None, :]).reshape(-1),
                            (fill_expert * T + (T - 1)).astype(jnp.int32)])
    row_token = jnp.sort(keys) % T
    return (block_expert.astype(jnp.int32), n_used.astype(jnp.int32).reshape(1), dest.astype(jnp.int32),
            row_token.astype(jnp.int32))


def kernel(x, c, w_ada, b_ada, norm_mix, norm_ffn, w_in, conv_w, conv_b, lru_wa, lru_ba, lru_wx, lru_bx,
           lru_lambda, diff_qnorm, diff_knorm, diff_lambda, diff_subln, swa_qnorm, swa_knorm, swa_sinks,
           rel_bias, w_branch, w_out, w_router, b_router, w1, b1, w2, b2):
    B, S, D = x.shape
    L = w_ada.shape[0]
    T = B * S
    tb = min(ATT_TB, S)

    mods = _adaln(c, w_ada, b_ada)
    bprev, bdiag = _diff_bias_tiles(rel_bias, tb)
    sbias_prev, sbias_cur = _swa_bias_tiles(rel_bias)
    gidx = jnp.arange(512) // DIFF_HD
    gmat = jnp.where(gidx[:, None] == gidx[None, :], 1.0 / DIFF_HD, 0.0).astype(BF16)

    x2 = x.reshape(T, D)
    for l in range(L):
        lambda_init = 0.8 - 0.6 * math.exp(-0.3 * l)
        mod = mods[l]
        qkg = jnp.stack([
            jnp.tile(diff_qnorm[l], 8) * (DIFF_HD ** -0.5 * LOG2E),
            jnp.tile(diff_knorm[l], 8),
            jnp.tile(swa_qnorm[l], 8) * SWA_HD ** -0.5,
            jnp.tile(swa_knorm[l], 8),
        ]).astype(F32)
        w_in_l = w_in[l].astype(BF16)
        xg, q1, q2, dk, dvt, sq, sk, sv, gl = _inproj(
            x2, mod, norm_mix[l].reshape(1, D), w_in_l, w_in_l[:, 2048:2560].T, gmat, qkg, S)

        wg = jnp.concatenate([_block_diag(lru_wa[l]), _block_diag(lru_wx[l])], axis=1).astype(BF16)
        bg = jnp.concatenate([lru_ba[l], lru_bx[l]]).reshape(1, 2 * LRU_WIDTH)
        sp = jax.nn.softplus(-lru_lambda[l].astype(F32)).reshape(1, LRU_WIDTH)
        o_lru = _rglru(xg, conv_w[l], conv_b[l].reshape(1, LRU_WIDTH), wg, bg, sp, B, S)

        o_diff = _diff_attn(q1, q2, dk, dvt, bprev, bdiag, diff_lambda[l],
                            diff_subln[l].reshape(2 * DIFF_HD, 1), lambda_init, B, S)

        sinks = jnp.repeat(swa_sinks[l].astype(F32).reshape(SWA_KV_HEADS, SWA_GROUP), WINDOW, axis=1)
        o_swa = _swa(sq, sk, sv, sbias_prev, sbias_cur, sinks.reshape(SWA_KV_HEADS, -1, 1), B, S)

        x2 = _merge(x2, mod, o_lru, o_diff, o_swa, gl, w_branch[l].astype(BF16), w_out[l].astype(BF16), S)

        hp, top_e, pt, rank, counts = _router(x2, mod, norm_ffn[l].reshape(1, D), w_router[l].T,
                                              b_router[l].reshape(N_EXPERTS, 1), S)
        block_expert, n_used, dest, row_token = _routing(counts[:, 0], top_e, rank, MOE_TM)
        y = _experts(l, block_expert, n_used, row_token, hp, w1, b1[:, :, None, :], w2, b2[:, :, None, :])
        x2 = _combine(dest, y, pt, x2, mod, S)
    return x2.reshape(B, S, D)
```

```python
import functools
import math

import jax
import jax.numpy as jnp
from jax import lax
from jax.experimental import pallas as pl
from jax.experimental.pallas import tpu as pltpu

F32 = jnp.float32
BF16 = jnp.bfloat16

LRU_WIDTH = 512
LRU_BLOCKS = 8
LRU_C = 8.0
CONV_W = 4
DIFF_HEADS = 4
DIFF_HD = 64
SWA_HEADS = 8
SWA_KV_HEADS = 2
SWA_GROUP = SWA_HEADS // SWA_KV_HEADS
SWA_HD = 64
WINDOW = 128
N_BRANCHES = 3
BRANCH_WIDTH = 512
REL_BUCKETS = 32
REL_MAX_DIST = 128
N_EXPERTS = 32
TOP_K = 4
SWIGLU_LIMIT = 7.0
SWIGLU_ALPHA = 1.702
NORM_EPS = 1e-6
NEG_INF = -1e30
LOG2E = math.log2(math.e)

VMEM_LIMIT_BYTES = 56 * 1024 * 1024
LANES = 128
SUBLANES = 8

ADALN_TN = 1536
PROJ_TM = 512
LRU_TS = 512
ATT_TB = 512
FAR_GROUP = 4
SWA_TS = 512
MERGE_TM = 512
ROUTER_TM = 512
MOE_TM = 512
COMB_TM = 256
ISSUE_UNROLL = 8

HIGHEST = lax.Precision.HIGHEST


def _params(*sem):
    return pltpu.CompilerParams(dimension_semantics=sem, vmem_limit_bytes=VMEM_LIMIT_BYTES)


def _dot(a, b, **kw):
    return jnp.dot(a, b, preferred_element_type=F32, **kw)


def _dot_nt(a, b, **kw):
    return lax.dot_general(a, b, (((1,), (1,)), ((), ())), preferred_element_type=F32, **kw)


def _sigmoid(x):
    return 1.0 / (1.0 + jnp.exp(-x))


def _adaln_kernel(c_ref, w_ref, b_ref, o_ref):
    c = c_ref[...]
    cond = c * _sigmoid(c)
    o_ref[0] = _dot(cond, w_ref[0], precision=HIGHEST) + b_ref[0]


def _adaln(c, w_ada, b_ada):
    L, D, N = w_ada.shape
    B = c.shape[0]
    rows = 8
    cp = jnp.zeros((rows, D), F32).at[:B].set(c)
    out = pl.pallas_call(
        _adaln_kernel,
        out_shape=jax.ShapeDtypeStruct((L, rows, N), F32),
        grid=(L, N // ADALN_TN),
        in_specs=[
            pl.BlockSpec((rows, D), lambda l, j: (0, 0)),
            pl.BlockSpec((1, D, ADALN_TN), lambda l, j: (l, 0, j)),
            pl.BlockSpec((1, 1, ADALN_TN), lambda l, j: (l, 0, j)),
        ],
        out_specs=pl.BlockSpec((1, rows, ADALN_TN), lambda l, j: (l, 0, j)),
        compiler_params=_params("arbitrary", "arbitrary"),
        name="adaln",
    )(cp, w_ada, b_ada.reshape(L, 1, N))
    return out[:, :B].reshape(L, B, 6, D)


def _rms_mod(x, gain, scale, shift):
    ms = jnp.mean(x * x, axis=-1, keepdims=True)
    return (x * lax.rsqrt(ms + NORM_EPS) * gain) * (1.0 + scale) + shift


def _group_rms(x, gmat):
    ms = _dot((x * x).astype(BF16), gmat)
    return x * lax.rsqrt(ms + NORM_EPS)


def _inproj_kernel(x_ref, mod_ref, gain_ref, w_ref, wvt_ref, gmat_ref, qkg_ref,
                   xg_ref, q1_ref, q2_ref, dk_ref, dvt_ref, sq_ref, sk_ref, sv_ref, gl_ref):
    x = x_ref[...]
    h = _rms_mod(x, gain_ref[...], mod_ref[0, 1:2, :], mod_ref[0, 0:1, :]).astype(BF16)
    gmat = gmat_ref[...]

    xg_ref[...] = _dot(h, w_ref[:, 0:1024]).astype(BF16)

    dd = _dot(h, w_ref[:, 1024:2048])
    qn = _group_rms(dd[:, 0:512], gmat) * qkg_ref[0:1, :]
    kn = _group_rms(dd[:, 512:1024], gmat) * qkg_ref[1:2, :]
    lane = lax.broadcasted_iota(jnp.int32, qn.shape, 1) % LANES
    q1_ref[...] = jnp.where(lane < DIFF_HD, qn, 0.0).astype(BF16)
    q2_ref[...] = jnp.where(lane >= DIFF_HD, qn, 0.0).astype(BF16)
    dk_ref[...] = kn.astype(BF16)
    dvt_ref[0] = _dot_nt(wvt_ref[...], h).astype(BF16)

    ss = _dot(h, w_ref[:, 2560:3328])
    sq_ref[...] = (_group_rms(ss[:, 0:512], gmat) * qkg_ref[2:3, :]).astype(BF16)
    sk_ref[...] = (_group_rms(ss[:, 512:640], gmat[0:128, 0:128]) * qkg_ref[3:4, 0:128]).astype(BF16)
    sv_ref[...] = ss[:, 640:768].astype(BF16)

    gl_ref[...] = _dot(h, w_ref[:, 3328:6400]).astype(BF16)


def _inproj(x2, mod, gain, w_in, wvt, gmat, qkg, seq):
    T, D = x2.shape
    tm = min(PROJ_TM, seq)
    per_b = seq // tm
    VW = wvt.shape[0]
    widths = (1024, 512, 512, 512, None, 512, 128, 128, 3072)
    row = lambda i: (i, 0)
    const = lambda i: (0, 0)
    vt_shape = jax.ShapeDtypeStruct((T // seq, VW, seq), BF16)
    vt_spec = pl.BlockSpec((1, VW, tm), lambda i: (i // per_b, 0, i % per_b))
    return pl.pallas_call(
        _inproj_kernel,
        out_shape=[vt_shape if w is None else jax.ShapeDtypeStruct((T, w), BF16) for w in widths],
        grid=(T // tm,),
        in_specs=[
            pl.BlockSpec((tm, D), row),
            pl.BlockSpec((1, 6, D), lambda i: (i // per_b, 0, 0)),
            pl.BlockSpec((1, D), const),
            pl.BlockSpec(w_in.shape, const, pipeline_mode=pl.Buffered(1)),
            pl.BlockSpec(wvt.shape, const),
            pl.BlockSpec(gmat.shape, const),
            pl.BlockSpec(qkg.shape, const),
        ],
        out_specs=[vt_spec if w is None else pl.BlockSpec((tm, w), row) for w in widths],
        compiler_params=_params("arbitrary"),
        name="inproj",
    )(x2, mod, gain, w_in, wvt, gmat, qkg)


def _rglru_kernel(xg_ref, cw_ref, cb_ref, wg_ref, bg_ref, sp_ref, o_ref, ext_ref, hc_ref):
    ts = xg_ref.shape[0]
    C = LRU_WIDTH

    @pl.when(pl.program_id(1) == 0)
    def _():
        ext_ref[0:8, :] = jnp.zeros((8, C), F32)
        hc_ref[...] = jnp.zeros_like(hc_ref)

    xr = xg_ref[:, 0:C].astype(F32)
    ext_ref[8:8 + ts, :] = xr
    xc = cb_ref[...] + xr * cw_ref[CONV_W - 1:CONV_W, :]
    for back in range(1, CONV_W):
        tap = CONV_W - 1 - back
        xc = xc + ext_ref[8 - back:8 - back + ts, :] * cw_ref[tap:tap + 1, :]
    ext_ref[0:8, :] = xr[ts - 8:ts, :]

    gates = _dot(xc.astype(BF16), wg_ref[...]) + bg_ref[...]
    r = _sigmoid(gates[:, 0:C])
    gi = _sigmoid(gates[:, C:2 * C])
    log_a = (-LRU_C) * r * sp_ref[...]
    a = jnp.exp(log_a)
    b = xc * gi * jnp.sqrt(1.0 - a * a)

    rows = lax.broadcasted_iota(jnp.int32, (ts, C), 0)
    d = 1
    while d < ts:
        keep = rows >= d
        a_sh = pltpu.roll(a, d, axis=0)
        b_sh = pltpu.roll(b, d, axis=0)
        b = jnp.where(keep, a * b_sh + b, b)
        a = jnp.where(keep, a * a_sh, a)
        d *= 2
    h = b + a * hc_ref[...]
    hc_ref[...] = h[ts - 1:ts, :]

    gr = xg_ref[:, C:2 * C].astype(F32)
    gelu = 0.5 * gr * (1.0 + jnp.tanh(math.sqrt(2.0 / math.pi) * (gr + 0.044715 * gr * gr * gr)))
    o_ref[...] = (h * gelu).astype(BF16)


def _rglru(xg, conv_w, conv_b, wg, bg, softplus_neg_lam, batch, seq):
    T = xg.shape[0]
    C = LRU_WIDTH
    ts = min(LRU_TS, seq)
    per_b = seq // ts
    const = lambda b, i: (0, 0)
    return pl.pallas_call(
        _rglru_kernel,
        out_shape=jax.ShapeDtypeStruct((T, C), BF16),
        grid=(batch, per_b),
        in_specs=[
            pl.BlockSpec((ts, 2 * C), lambda b, i: (b * per_b + i, 0)),
            pl.BlockSpec((CONV_W, C), const),
            pl.BlockSpec((1, C), const),
            pl.BlockSpec((C, 2 * C), const),
            pl.BlockSpec((1, 2 * C), const),
            pl.BlockSpec((1, C), const),
        ],
        out_specs=pl.BlockSpec((ts, C), lambda b, i: (b * per_b + i, 0)),
        scratch_shapes=[pltpu.VMEM((ts + 8, C), F32), pltpu.VMEM((1, C), F32)],
        compiler_params=_params("arbitrary", "arbitrary"),
        name="rglru",
    )(xg, conv_w, conv_b, wg, bg, softplus_neg_lam)


def _diff_attn_kernel(q1_ref, q2_ref, k_ref, vt_ref, bprev_ref, bdiag_ref, lam_ref, sub_ref, o_ref,
                      m_ref, l_ref, acc_ref, *, tb, lambda_init):
    i = pl.program_id(2)
    q = (q1_ref[0], q2_ref[0])

    m_ref[...] = jnp.full(m_ref.shape, NEG_INF, F32)
    l_ref[...] = jnp.zeros(l_ref.shape, F32)
    acc_ref[...] = jnp.zeros(acc_ref.shape, F32)

    def attend(start, tk, bias):
        kb = k_ref[0, pl.ds(start, tk), :]
        vb = vt_ref[0, :, pl.ds(start, tk)]
        scores = [_dot_nt(kb, q[mp]) for mp in range(2)]
        for mp in range(2):
            s = scores[mp]
            if bias is not None:
                s = s + bias
            m_old = m_ref[mp]
            m_new = jnp.maximum(m_old, jnp.max(s, axis=0, keepdims=True))
            alpha = jnp.exp2(m_old - m_new)
            p = jnp.exp2(s - m_new)
            l_ref[mp] = alpha * l_ref[mp] + jnp.sum(p, axis=0, keepdims=True)
            acc_ref[mp] = alpha * acc_ref[mp] + _dot(vb, p.astype(BF16))
            m_ref[mp] = m_new

    n_far = jnp.maximum(i - 1, 0)
    group = FAR_GROUP * tb

    def far(j, carry):
        attend(pl.multiple_of(j * group, group), group, None)
        return carry

    lax.fori_loop(0, n_far // FAR_GROUP, far, 0)

    @pl.when(n_far % FAR_GROUP >= 2)
    def _():
        attend(pl.multiple_of((n_far // FAR_GROUP) * group, 2 * tb), 2 * tb, None)

    @pl.when(n_far % 2 == 1)
    def _():
        attend(pl.multiple_of((n_far - 1) * tb, tb), tb, None)

    @pl.when(i > 0)
    def _():
        attend(pl.multiple_of((i - 1) * tb, tb), tb, bprev_ref[0])

    attend(pl.multiple_of(i * tb, tb), tb, bdiag_ref[0])

    lv = lam_ref[...]
    lam = (jnp.exp(jnp.sum(lv[0:1] * lv[1:2], axis=-1, keepdims=True))
           - jnp.exp(jnp.sum(lv[2:3] * lv[3:4], axis=-1, keepdims=True)) + lambda_init)
    o = acc_ref[0] / l_ref[0] - lam * (acc_ref[1] / l_ref[1])
    ms = jnp.mean(o * o, axis=0, keepdims=True)
    o = o * lax.rsqrt(ms + NORM_EPS) * sub_ref[...] * (1.0 - lambda_init)
    o_ref[0] = o.T.astype(BF16)


def _diff_attn(q1, q2, k, vt, bprev, bdiag, lam_vecs, subln, lambda_init, batch, seq):
    W = DIFF_HEADS * 2 * DIFF_HD
    tb = min(ATT_TB, seq)
    shp = (batch, seq, W)
    qspec = pl.BlockSpec((1, tb, LANES), lambda b, h, i: (b, i, h))
    kspec = pl.BlockSpec((1, seq, LANES), lambda b, h, i: (b, 0, h))
    vspec = pl.BlockSpec((1, LANES, seq), lambda b, h, i: (b, h, 0))
    bspec = pl.BlockSpec((1, tb, tb), lambda b, h, i: (h, 0, 0))
    const = lambda b, h, i: (0, 0)
    out = pl.pallas_call(
        functools.partial(_diff_attn_kernel, tb=tb, lambda_init=lambda_init),
        out_shape=jax.ShapeDtypeStruct(shp, BF16),
        grid=(batch, DIFF_HEADS, seq // tb),
        in_specs=[qspec, qspec, kspec, vspec, bspec, bspec,
                  pl.BlockSpec((4, DIFF_HD), const), pl.BlockSpec((2 * DIFF_HD, 1), const)],
        out_specs=qspec,
        scratch_shapes=[pltpu.VMEM((2, 1, tb), F32), pltpu.VMEM((2, 1, tb), F32),
                        pltpu.VMEM((2, LANES, tb), F32)],
        compiler_params=_params("arbitrary", "arbitrary", "arbitrary"),
        name="diff_attn",
    )(q1.reshape(shp), q2.reshape(shp), k.reshape(shp), vt, bprev, bdiag, lam_vecs, subln)
    return out.reshape(batch * seq, W)


def _swa_kernel(q_ref, k_ref, v_ref, bp_ref, bc_ref, sink_ref, o_ref, *, ts):
    i = pl.program_id(1)
    blk = WINDOW
    for sub in range(ts // blk):
        start = i * ts + sub * blk
        has_prev = start > 0
        pstart = pl.multiple_of(jnp.maximum(start - blk, 0), blk)
        cstart = pl.multiple_of(start, blk)
        kp = k_ref[0, pl.ds(pstart, blk), :]
        kc = k_ref[0, pl.ds(cstart, blk), :]
        vp = v_ref[0, pl.ds(pstart, blk), :]
        vc = v_ref[0, pl.ds(cstart, blk), :]
        qs = q_ref[0, sub * blk:(sub + 1) * blk, :]
        outs = []
        for hk in range(SWA_KV_HEADS):
            c0 = hk * SWA_GROUP * SWA_HD
            qh = jnp.concatenate(
                [qs[:, c0 + g * SWA_HD:c0 + (g + 1) * SWA_HD] for g in range(SWA_GROUP)], axis=0)
            ksl = slice(hk * SWA_HD, (hk + 1) * SWA_HD)
            s_p = _dot_nt(qh, kp[:, ksl]) + bp_ref[hk]
            s_p = jnp.where(has_prev, s_p, NEG_INF)
            s_c = _dot_nt(qh, kc[:, ksl]) + bc_ref[hk]
            sink = sink_ref[hk]
            m = jnp.maximum(jnp.maximum(jnp.max(s_p, axis=-1, keepdims=True),
                                        jnp.max(s_c, axis=-1, keepdims=True)), sink)
            p_p = jnp.exp(s_p - m)
            p_c = jnp.exp(s_c - m)
            den = (jnp.sum(p_p, axis=-1, keepdims=True) + jnp.sum(p_c, axis=-1, keepdims=True)
                   + jnp.exp(sink - m))
            o = _dot(p_p.astype(BF16), vp[:, ksl]) + _dot(p_c.astype(BF16), vc[:, ksl])
            o = o / den
            outs.extend(o[g * blk:(g + 1) * blk, :] for g in range(SWA_GROUP))
        o_ref[0, sub * blk:(sub + 1) * blk, :] = jnp.concatenate(outs, axis=1).astype(BF16)


def _swa(q, k, v, bias_prev, bias_cur, sinks, batch, seq):
    ts = min(SWA_TS, seq)
    WQ = SWA_HEADS * SWA_HD
    WK = SWA_KV_HEADS * SWA_HD
    rows = SWA_GROUP * WINDOW
    const3 = lambda b, i: (0, 0, 0)
    out = pl.pallas_call(
        functools.partial(_swa_kernel, ts=ts),
        out_shape=jax.ShapeDtypeStruct((batch, seq, WQ), BF16),
        grid=(batch, seq // ts),
        in_specs=[
            pl.BlockSpec((1, ts, WQ), lambda b, i: (b, i, 0)),
            pl.BlockSpec((1, seq, WK), lambda b, i: (b, 0, 0)),
            pl.BlockSpec((1, seq, WK), lambda b, i: (b, 0, 0)),
            pl.BlockSpec((SWA_KV_HEADS, rows, WINDOW), const3),
            pl.BlockSpec((SWA_KV_HEADS, rows, WINDOW), const3),
            pl.BlockSpec((SWA_KV_HEADS, rows, 1), const3),
        ],
        out_specs=pl.BlockSpec((1, ts, WQ), lambda b, i: (b, i, 0)),
        compiler_params=_params("arbitrary", "arbitrary"),
        name="swa_attn",
    )(q.reshape(batch, seq, WQ), k.reshape(batch, seq, WK), v.reshape(batch, seq, WK),
      bias_prev, bias_cur, sinks)
    return out.reshape(batch * seq, WQ)


def _merge_kernel(x_ref, mod_ref, lru_ref, diff_ref, swa_ref, gl_ref, wb_ref, wo_ref, o_ref):
    D = x_ref.shape[1]
    merged = None
    for n, br in enumerate((lru_ref, diff_ref, swa_ref)):
        gate = _sigmoid(gl_ref[:, n * D:(n + 1) * D].astype(F32))
        term = gate * _dot(br[...], wb_ref[n])
        merged = term if merged is None else merged + term
    out = _dot(merged.astype(BF16), wo_ref[...])
    o_ref[...] = x_ref[...] + mod_ref[0, 2:3, :] * out


def _merge(x2, mod, o_lru, o_diff, o_swa, gl, w_branch, w_out, seq):
    T, D = x2.shape
    tm = min(MERGE_TM, seq)
    per_b = seq // tm
    row = lambda i: (i, 0)
    return pl.pallas_call(
        _merge_kernel,
        out_shape=jax.ShapeDtypeStruct((T, D), F32),
        grid=(T // tm,),
        in_specs=[
            pl.BlockSpec((tm, D), row),
            pl.BlockSpec((1, 6, D), lambda i: (i // per_b, 0, 0)),
            pl.BlockSpec((tm, BRANCH_WIDTH), row),
            pl.BlockSpec((tm, BRANCH_WIDTH), row),
            pl.BlockSpec((tm, BRANCH_WIDTH), row),
            pl.BlockSpec((tm, N_BRANCHES * D), row),
            pl.BlockSpec(w_branch.shape, lambda i: (0, 0, 0)),
            pl.BlockSpec(w_out.shape, lambda i: (0, 0)),
        ],
        out_specs=pl.BlockSpec((tm, D), row),
        compiler_params=_params("arbitrary"),
        name="merge_outproj",
    )(x2, mod, o_lru, o_diff, o_swa, gl, w_branch, w_out)


def _router_kernel(x_ref, mod_ref, gain_ref, wr_ref, br_ref, tri_ref,
                   hp_ref, e_ref, pt_ref, rank_ref, cnt_ref, run_ref):
    @pl.when(pl.program_id(0) == 0)
    def _():
        run_ref[...] = jnp.zeros_like(run_ref)

    x = x_ref[...]
    tm, D = x.shape
    h = _rms_mod(x, gain_ref[...], mod_ref[0, 4:5, :], mod_ref[0, 3:4, :])
    for c in range(D // LANES):
        hp_ref[pl.ds(c, tm, stride=SUBLANES), :] = h[:, LANES * c:LANES * (c + 1)]

    logits = _dot_nt(wr_ref[...], h, precision=HIGHEST) + br_ref[...]
    eidx = lax.broadcasted_iota(jnp.int32, logits.shape, 0)
    vals, idxs, hots = [], [], []
    for _ in range(TOP_K):
        mx = jnp.max(logits, axis=0, keepdims=True)
        sel = jnp.min(jnp.where(logits == mx, eidx, N_EXPERTS), axis=0, keepdims=True)
        hot = eidx == sel
        vals.append(mx)
        idxs.append(sel)
        hots.append(hot)
        logits = jnp.where(hot, -jnp.inf, logits)
    e_ref[...] = jnp.concatenate(idxs, axis=0)
    top_v = jnp.concatenate(vals, axis=0)
    ex = jnp.exp(top_v - top_v[0:1])
    p = ex / jnp.sum(ex, axis=0, keepdims=True)
    pt_ref[...] = jnp.concatenate([p, jnp.zeros((LANES - TOP_K, tm), F32)], axis=0).T

    member = hots[0]
    for hot in hots[1:]:
        member = member | hot
    member = jnp.where(member, 1.0, 0.0)
    before = _dot(member.astype(BF16), tri_ref[...]) + run_ref[:, 0:1]
    ranks = [jnp.sum(jnp.where(hot, before, 0.0), axis=0, keepdims=True) for hot in hots]
    rank_ref[...] = jnp.concatenate(ranks, axis=0).astype(jnp.int32)
    run_ref[...] = run_ref[...] + jnp.sum(member, axis=1, keepdims=True)
    cnt_ref[...] = run_ref[...]


def _router(x2, mod, gain, w_router_t, b_router, seq):
    T, D = x2.shape
    tm = min(ROUTER_TM, seq)
    per_b = seq // tm
    tri = (jnp.arange(tm)[:, None] < jnp.arange(tm)[None, :]).astype(BF16)
    const = lambda i: (0, 0)
    return pl.pallas_call(
        _router_kernel,
        out_shape=[jax.ShapeDtypeStruct((T * SUBLANES, LANES), F32),
                   jax.ShapeDtypeStruct((TOP_K, T), jnp.int32),
                   jax.ShapeDtypeStruct((T, LANES), F32),
                   jax.ShapeDtypeStruct((TOP_K, T), jnp.int32),
                   jax.ShapeDtypeStruct((N_EXPERTS, LANES), F32)],
        grid=(T // tm,),
        in_specs=[
            pl.BlockSpec((tm, D), lambda i: (i, 0)),
            pl.BlockSpec((1, 6, D), lambda i: (i // per_b, 0, 0)),
            pl.BlockSpec((1, D), const),
            pl.BlockSpec((N_EXPERTS, D), const),
            pl.BlockSpec((N_EXPERTS, 1), const),
            pl.BlockSpec((tm, tm), const),
        ],
        out_specs=[pl.BlockSpec((tm * SUBLANES, LANES), lambda i: (i, 0)),
                   pl.BlockSpec((TOP_K, tm), lambda i: (0, i)),
                   pl.BlockSpec((tm, LANES), lambda i: (i, 0)),
                   pl.BlockSpec((TOP_K, tm), lambda i: (0, i)),
                   pl.BlockSpec((N_EXPERTS, LANES), const)],
        scratch_shapes=[pltpu.VMEM((N_EXPERTS, LANES), F32)],
        compiler_params=_params("arbitrary"),
        name="router",
    )(x2, mod, gain, w_router_t, b_router, tri)


def _tile_rows(idx, tm):
    K, T = idx.shape
    nb = T // tm
    return idx.reshape(K, nb, tm).transpose(1, 0, 2).reshape(nb, 1, K * tm)


def _expert_kernel(be_ref, nused_ref, tok_ref, tnext_ref, hp_hbm, w1_ref, b1_ref, w2_ref, b2_ref, y_ref,
                   xbuf, sem, w1b_ref, w2s_ref, w2b_ref, act_ref, *, tm):
    i = pl.program_id(0)
    slot = i % 2
    used = i < nused_ref[0]
    next_used = (i + 1 < nused_ref[0]) & (i + 1 < pl.num_programs(0))
    chunks = w1_ref.shape[3] // (2 * LANES)
    hl = LANES // 2

    def gather(idx_ref, s):
        def issue(g, carry):
            for j in range(ISSUE_UNROLL):
                r = g * ISSUE_UNROLL + j
                src = pl.multiple_of(idx_ref[0, 0, r] * SUBLANES, SUBLANES)
                dst = pl.multiple_of(r * SUBLANES, SUBLANES)
                pltpu.make_async_copy(hp_hbm.at[pl.ds(src, SUBLANES), :], xbuf.at[s, pl.ds(dst, SUBLANES), :],
                                      sem.at[s]).start(priority=j % 2)
            return carry

        lax.fori_loop(0, tm // ISSUE_UNROLL, issue, 0)

    @pl.when((i == 0) & used)
    def _():
        gather(tok_ref, 0)

    for s in range(2):
        @pl.when(next_used & (slot != s))
        def _(s=s):
            gather(tnext_ref, s)

    @pl.when(used & ((i == 0) | (be_ref[i] != be_ref[jnp.maximum(i - 1, 0)])))
    def _():
        for q in range(chunks):
            cs = slice(2 * LANES * q, 2 * LANES * (q + 1))
            w1b_ref[:, cs] = w1_ref[0, 0, :, cs].astype(BF16)
        cols = w2_ref.shape[3] // LANES
        for q in range(w2_ref.shape[2] // LANES):
            for s in range(2):
                rows = w2_ref[0, 0, LANES * q + hl * s:LANES * q + hl * (s + 1), :]
                for c in range(cols):
                    w2s_ref[c, pl.ds(LANES * q + s, hl, stride=2), :] = rows[:, LANES * c:LANES * (c + 1)]
        for c in range(cols):
            w2b_ref[:, LANES * c:LANES * (c + 1)] = w2s_ref[c].astype(BF16)

    @pl.when(used)
    def _():
        pltpu.make_async_copy(hp_hbm.at[pl.ds(0, tm * SUBLANES), :], xbuf.at[slot], sem.at[slot]).wait()
        xs = xbuf.at[slot]
        x = jnp.concatenate([xs[pl.ds(c, tm, stride=SUBLANES), :] for c in range(SUBLANES)],
                            axis=1).astype(BF16)
        even = lax.broadcasted_iota(jnp.int32, (tm, LANES), 1) % 2 == 0
        for q in range(chunks):
            cs = slice(2 * LANES * q, 2 * LANES * (q + 1))
            hq = _dot(x, w1b_ref[:, cs]) + b1_ref[0, 0, :, cs]
            lo, hi = hq[:, :LANES], hq[:, LANES:]
            glu = jnp.where(even, lo, pltpu.roll(hi, 1, axis=1))
            lin = jnp.where(even, pltpu.roll(lo, LANES - 1, axis=1), hi)
            glu = jnp.minimum(glu, SWIGLU_LIMIT)
            lin = jnp.clip(lin, -SWIGLU_LIMIT, SWIGLU_LIMIT)
            act = glu * _sigmoid(SWIGLU_ALPHA * glu) * (lin + 1.0)
            act_ref[:, LANES * q:LANES * (q + 1)] = act.astype(BF16)
        y = _dot(act_ref[...], w2b_ref[...]) + b2_ref[0, 0]
        for c in range(SUBLANES):
            y_ref[pl.ds(c, tm, stride=SUBLANES), :] = y[:, LANES * c:LANES * (c + 1)]

    @pl.when(jnp.logical_not(used))
    def _():
        y_ref[...] = jnp.zeros(y_ref.shape, y_ref.dtype)


def _experts(layer, block_expert, n_used, row_token, hp, w1, b1, w2, b2):
    _, E, D, F2 = w1.shape
    assert D == SUBLANES * LANES
    F = F2 // 2
    tm = MOE_TM
    n_rows = row_token.shape[0]
    nb = n_rows // tm
    rows = row_token.reshape(nb, 1, tm)
    wmap = lambda i, be, nu: (layer, be[i], 0, 0)
    return pl.pallas_call(
        functools.partial(_expert_kernel, tm=tm),
        out_shape=jax.ShapeDtypeStruct((n_rows * SUBLANES, LANES), F32),
        grid_spec=pltpu.PrefetchScalarGridSpec(
            num_scalar_prefetch=2,
            grid=(nb,),
            in_specs=[
                pl.BlockSpec((1, 1, tm), lambda i, be, nu: (i, 0, 0), memory_space=pltpu.SMEM),
                pl.BlockSpec((1, 1, tm), lambda i, be, nu: (jnp.minimum(i + 1, nb - 1), 0, 0),
                             memory_space=pltpu.SMEM),
                pl.BlockSpec(memory_space=pl.ANY),
                pl.BlockSpec((1, 1, D, F2), wmap),
                pl.BlockSpec((1, 1, 1, F2), wmap),
                pl.BlockSpec((1, 1, F, D), wmap),
                pl.BlockSpec((1, 1, 1, D), wmap),
            ],
            out_specs=pl.BlockSpec((tm * SUBLANES, LANES), lambda i, be, nu: (i, 0)),
            scratch_shapes=[pltpu.VMEM((2, tm * SUBLANES, LANES), F32), pltpu.SemaphoreType.DMA((2,)),
                            pltpu.VMEM((D, F2), BF16), pltpu.VMEM((D // LANES, F, LANES), F32),
                            pltpu.VMEM((F, D), BF16),
                            pltpu.VMEM((tm, F), BF16)],
        ),
        compiler_params=_params("arbitrary"),
        name="moe_experts",
    )(block_expert, n_used, rows, rows, hp, w1, b1, w2, b2)


def _combine_kernel(dest_ref, dnext_ref, y_hbm, pt_ref, x_ref, mod_ref, o_ref, ybuf, sem, *, tm):
    n = TOP_K * tm
    i = pl.program_id(0)
    slot = i % 2

    def gather(idx_ref, s):
        def issue(g, carry):
            for j in range(ISSUE_UNROLL):
                r = g * ISSUE_UNROLL + j
                src = pl.multiple_of(idx_ref[0, 0, r] * SUBLANES, SUBLANES)
                dst = pl.multiple_of(r * SUBLANES, SUBLANES)
                pltpu.make_async_copy(y_hbm.at[pl.ds(src, SUBLANES), :], ybuf.at[s, pl.ds(dst, SUBLANES), :],
                                      sem.at[s]).start(priority=j % 2)
            return carry

        lax.fori_loop(0, n // ISSUE_UNROLL, issue, 0)

    @pl.when(i == 0)
    def _():
        gather(dest_ref, 0)

    for s in range(2):
        @pl.when((i + 1 < pl.num_programs(0)) & (slot != s))
        def _(s=s):
            gather(dnext_ref, s)

    pltpu.make_async_copy(y_hbm.at[pl.ds(0, n * SUBLANES), :], ybuf.at[slot], sem.at[slot]).wait()

    rows = ybuf.at[slot]
    for c in range(SUBLANES):
        cs = slice(LANES * c, LANES * (c + 1))
        acc = rows[pl.ds(c, tm, stride=SUBLANES), :] * pt_ref[:, 0:1]
        for k in range(1, TOP_K):
            acc = acc + rows[pl.ds(k * tm * SUBLANES + c, tm, stride=SUBLANES), :] * pt_ref[:, k:k + 1]
        o_ref[:, cs] = x_ref[:, cs] + mod_ref[0, 5:6, cs] * acc


def _combine(dest, y, pt, x2, mod, seq):
    T, D = x2.shape
    tm = min(COMB_TM, seq)
    per_b = seq // tm
    nb = T // tm
    rows = _tile_rows(dest, tm)
    return pl.pallas_call(
        functools.partial(_combine_kernel, tm=tm),
        out_shape=jax.ShapeDtypeStruct((T, D), F32),
        grid=(nb,),
        in_specs=[
            pl.BlockSpec((1, 1, TOP_K * tm), lambda i: (i, 0, 0), memory_space=pltpu.SMEM),
            pl.BlockSpec((1, 1, TOP_K * tm), lambda i: (jnp.minimum(i + 1, nb - 1), 0, 0),
                         memory_space=pltpu.SMEM),
            pl.BlockSpec(memory_space=pl.ANY),
            pl.BlockSpec((tm, LANES), lambda i: (i, 0)),
            pl.BlockSpec((tm, D), lambda i: (i, 0)),
            pl.BlockSpec((1, 6, D), lambda i: (i // per_b, 0, 0)),
        ],
        out_specs=pl.BlockSpec((tm, D), lambda i: (i, 0)),
        scratch_shapes=[pltpu.VMEM((2, TOP_K * tm * SUBLANES, LANES), F32), pltpu.SemaphoreType.DMA((2,))],
        compiler_params=_params("arbitrary"),
        name="moe_combine",
    )(rows, rows, y, pt, x2, mod)


def _t5_bucket(dist):
    exact = REL_BUCKETS // 2
    log_ratio = jnp.log(jnp.maximum(dist, 1).astype(F32) / exact) / math.log(REL_MAX_DIST / exact)
    large = exact + (log_ratio * (REL_BUCKETS - exact)).astype(jnp.int32)
    return jnp.where(dist < exact, dist, jnp.minimum(large, REL_BUCKETS - 1))


def _diff_bias_tiles(rel_bias, tb):
    table = rel_bias[:, :DIFF_HEADS].astype(F32)
    shifted = (table - table[REL_BUCKETS - 1]) * LOG2E
    kk = jnp.arange(tb)[:, None]
    qq = jnp.arange(tb)[None, :]

    def tile(dist):
        return _bucket_lookup(shifted, _t5_bucket(jnp.maximum(dist, 0)))

    diag = jnp.where((qq - kk >= 0)[None], tile(qq - kk), NEG_INF)
    prev = tile(qq + tb - kk)
    return prev, diag


def _bucket_lookup(table, bucket):
    out = jnp.zeros((table.shape[1],) + bucket.shape, F32)
    for b in range(table.shape[0]):
        out = out + jnp.where(bucket[None] == b, table[b].reshape((-1,) + (1,) * bucket.ndim), 0.0)
    return out


def _swa_bias_tiles(rel_bias):
    qi = jnp.arange(WINDOW)[:, None]
    kj = jnp.arange(2 * WINDOW)[None, :]
    dist = WINDOW + qi - kj
    ok = (dist >= 0) & (dist < WINDOW)
    bias = _bucket_lookup(rel_bias[:, DIFF_HEADS:].astype(F32),
                          _t5_bucket(jnp.clip(dist, 0, WINDOW - 1)))
    bias = jnp.where(ok[None], bias, NEG_INF)
    bias = bias.reshape(SWA_KV_HEADS, SWA_GROUP * WINDOW, 2 * WINDOW)
    return bias[..., :WINDOW], bias[..., WINDOW:]


def _block_diag(w):
    nb, n, _ = w.shape
    eye = jnp.eye(nb, dtype=w.dtype)
    return (eye[:, None, :, None] * w[:, :, None, :]).reshape(nb * n, nb * n)


def _routing(counts, top_e, rank, tm):
    K, T = top_e.shape
    counts = counts.astype(jnp.int32)
    padded = (counts + tm - 1) // tm * tm
    pad_ends = jnp.cumsum(padded)
    pad_starts = pad_ends - padded
    onehot = top_e[..., None] == jnp.arange(N_EXPERTS, dtype=jnp.int32)
    dest = rank + jnp.sum(jnp.where(onehot, pad_starts, 0), axis=-1)
    n_rows = K * T + N_EXPERTS * tm
    nb = n_rows // tm
    n_used = pad_ends[-1] // tm
    blk = jnp.minimum(jnp.arange(nb), n_used - 1) * tm
    block_expert = jnp.sum(blk[:, None] >= pad_ends[None, :], axis=-1)
    fill_ends = jnp.cumsum(padded - counts)
    fill_expert = jnp.sum(jnp.arange(N_EXPERTS * tm)[:, None] >= fill_ends[None, :], axis=-1)
    keys = jnp.concatenate([(top_e * T + jnp.arange(T, dtype=jnp.int32)[None, :]).reshape(-1),
                            (fill_expert * T + (T - 1)).astype(jnp.int32)])
    row_token = jnp.sort(keys) % T
    return (block_expert.astype(jnp.int32), n_used.astype(jnp.int32).reshape(1), dest.astype(jnp.int32),
            row_token.astype(jnp.int32))


def kernel(x, c, w_ada, b_ada, norm_mix, norm_ffn, w_in, conv_w, conv_b, lru_wa, lru_ba, lru_wx, lru_bx,
           lru_lambda, diff_qnorm, diff_knorm, diff_lambda, diff_subln, swa_qnorm, swa_knorm, swa_sinks,
           rel_bias, w_branch, w_out, w_router, b_router, w1, b1, w2, b2):
    B, S, D = x.shape
    L = w_ada.shape[0]
    T = B * S
    tb = min(ATT_TB, S)

    mods = _adaln(c, w_ada, b_ada)
    bprev, bdiag = _diff_bias_tiles(rel_bias, tb)
    sbias_prev, sbias_cur = _swa_bias_tiles(rel_bias)
    gidx = jnp.arange(512) // DIFF_HD
    gmat = jnp.where(gidx[:, None] == gidx[None, :], 1.0 / DIFF_HD, 0.0).astype(BF16)

    x2 = x.reshape(T, D)
    for l in range(L):
        lambda_init = 0.8 - 0.6 * math.exp(-0.3 * l)
        mod = mods[l]
        qkg = jnp.stack([
            jnp.tile(diff_qnorm[l], 8) * (DIFF_HD ** -0.5 * LOG2E),
            jnp.tile(diff_knorm[l], 8),
            jnp.tile(swa_qnorm[l], 8) * SWA_HD ** -0.5,
            jnp.tile(swa_knorm[l], 8),
        ]).astype(F32)
        w_in_l = w_in[l].astype(BF16)
        xg, q1, q2, dk, dvt, sq, sk, sv, gl = _inproj(
            x2, mod, norm_mix[l].reshape(1, D), w_in_l, w_in_l[:, 2048:2560].T, gmat, qkg, S)

        wg = jnp.concatenate([_block_diag(lru_wa[l]), _block_diag(lru_wx[l])], axis=1).astype(BF16)
        bg = jnp.concatenate([lru_ba[l], lru_bx[l]]).reshape(1, 2 * LRU_WIDTH)
        sp = jax.nn.softplus(-lru_lambda[l].astype(F32)).reshape(1, LRU_WIDTH)
        o_lru = _rglru(xg, conv_w[l], conv_b[l].reshape(1, LRU_WIDTH), wg, bg, sp, B, S)

        o_diff = _diff_attn(q1, q2, dk, dvt, bprev, bdiag, diff_lambda[l],
                            diff_subln[l].reshape(2 * DIFF_HD, 1), lambda_init, B, S)

        sinks = jnp.repeat(swa_sinks[l].astype(F32).reshape(SWA_KV_HEADS, SWA_GROUP), WINDOW, axis=1)
        o_swa = _swa(sq, sk, sv, sbias_prev, sbias_cur, sinks.reshape(SWA_KV_HEADS, -1, 1), B, S)

        x2 = _merge(x2, mod, o_lru, o_diff, o_swa, gl, w_branch[l].astype(BF16), w_out[l].astype(BF16), S)

        hp, top_e, pt, rank, counts = _router(x2, mod, norm_ffn[l].reshape(1, D), w_router[l].T,
                                              b_router[l].reshape(N_EXPERTS, 1), S)
        block_expert, n_used, dest, row_token = _routing(counts[:, 0], top_e, rank, MOE_TM)
        y = _experts(l, block_expert, n_used, row_token, hp, w1, b1[:, :, None, :], w2, b2[:, :, None, :])
        x2 = _combine(dest, y, pt, x2, mod, S)
    return x2.reshape(B, S, D)
```
